```python
import jax
import jax.numpy as jnp
from jax import lax
import numpy as np


D_MODEL = 1024
BATCH = 4
SEQ = 4096
DEPTH = 1

D_MIX = D_MODEL
RWKV_WIDTH = D_MIX // 2
RWKV_HEAD_DIM = 64
RWKV_HEADS = RWKV_WIDTH // RWKV_HEAD_DIM
DECAY_LORA = 64
ICLR_LORA = 64
GATE_LORA = 128
GLA_V_WIDTH = D_MIX - RWKV_WIDTH
GLA_HEADS = 4
GLA_DV = GLA_V_WIDTH // GLA_HEADS
GLA_DK = GLA_DV // 2
GLA_K_WIDTH = GLA_HEADS * GLA_DK
GLA_GATE_LORA = 16
GLA_TAU = 16.0
GLA_CHUNK = 16
N_GROUPS = 4
EXPERTS_PER_GROUP = 8
N_EXPERTS = N_GROUPS * EXPERTS_PER_GROUP
TOP_K = 2
D_EXPERT = 512
MOE_BLOCK = 128
RMS_EPS = 1e-6
RWKV_GN_EPS = 64e-5

RWKV_SPLITS = (RWKV_WIDTH, RWKV_WIDTH, RWKV_WIDTH, DECAY_LORA, DECAY_LORA, ICLR_LORA, ICLR_LORA, GATE_LORA)
GLA_SPLITS = (GLA_K_WIDTH, GLA_K_WIDTH, GLA_V_WIDTH, GLA_GATE_LORA, GLA_GATE_LORA, GLA_V_WIDTH)
RWKV_COLS = sum(RWKV_SPLITS)
D_IN_PROJ = RWKV_COLS + sum(GLA_SPLITS)

kernel_name = 'bidir_hybrid_rwkv7_gla_hmoe'


def _split(p, sizes):
    return jnp.split(p, np.cumsum(sizes)[:-1].tolist(), axis=-1)


def rmsnorm(x, w):
    xf = x.astype(jnp.float32)
    y = xf * lax.rsqrt(jnp.mean(xf * xf, axis=-1, keepdims=True) + RMS_EPS)
    return (y * w.astype(jnp.float32)).astype(x.dtype)


def centred_shift(p):
    prev = jnp.pad(p[:, :-1], ((0, 0), (1, 0), (0, 0)))
    nxt = jnp.pad(p[:, 1:], ((0, 0), (0, 1), (0, 0)))
    return 0.5 * (prev + nxt)


def rwkv7_bidir_scan(r, w, k, v, kk, b):
    def to_dir(z):
        z = jnp.stack([z[0], jnp.flip(z[1], axis=1)])
        return jnp.moveaxis(z, 2, 0)

    def both(z):
        return to_dir(jnp.stack([z, z]))

    xs = (both(r), to_dir(w), to_dir(k), both(v), both(kk), to_dir(b))
    bsz, _, h, n = r.shape
    s0 = jnp.zeros((2, bsz, h, n, n), jnp.float32)

    def step(s, inp):
        r_t, w_t, k_t, v_t, kk_t, b_t = inp
        sa = jnp.einsum('dbhvk,dbhk->dbhv', s, -kk_t)
        s = s * w_t[..., None, :] + sa[..., None] * b_t[..., None, :] + v_t[..., None] * k_t[..., None, :]
        return s, jnp.einsum('dbhvk,dbhk->dbhv', s, r_t)

    _, ys = lax.scan(step, s0, xs)
    ys = jnp.moveaxis(ys, 0, 2)
    return ys[0] + jnp.flip(ys[1], axis=1)


def rwkv7_mixer(p, mu, w0_f, w2_f, w0_b, w2_b, a0_f, a2_f, a0_b, a2_b, g2, k_k, k_a, r_k, ln_w, ln_b):
    p = p.astype(jnp.float32)
    p = p + mu * (centred_shift(p) - p)
    r, k, v, wl_f, wl_b, al_f, al_b, gl = _split(p, RWKV_SPLITS)

    def decay(w0, wl, w2):
        logw = -jax.nn.softplus(-(w0 + jnp.tanh(wl) @ w2)) - 0.5
        return jnp.exp(-jnp.exp(logw))

    def heads(z):
        return z.reshape(z.shape[:-1] + (RWKV_HEADS, RWKV_HEAD_DIM))

    w = jnp.stack([decay(w0_f, wl_f, w2_f), decay(w0_b, wl_b, w2_b)])
    a = jax.nn.sigmoid(jnp.stack([a0_f + al_f @ a2_f, a0_b + al_b @ a2_b]))
    g = jax.nn.sigmoid(gl) @ g2
    kk = heads(k * k_k)
    kk = kk / jnp.maximum(jnp.sqrt(jnp.sum(kk * kk, axis=-1, keepdims=True)), 1e-12)
    k_dir = k * (1.0 + (a - 1.0) * k_a)
    b = kk * heads(a)
    rh, vh = heads(r), heads(v)
    y = rwkv7_bidir_scan(rh, heads(w), heads(k_dir), vh, kk, b)
    mean = jnp.mean(y, axis=-1, keepdims=True)
    var = jnp.var(y, axis=-1, keepdims=True)
    y = (y - mean) * lax.rsqrt(var + RWKV_GN_EPS)
    y = y.reshape(y.shape[:-2] + (RWKV_WIDTH,)) * ln_w + ln_b
    bonus = jnp.sum(rh * heads(k_dir[0] + k_dir[1]) * r_k, axis=-1, keepdims=True) * vh
    return (y + bonus.reshape(y.shape)) * g


def gla_chunked(q, k, v, g):
    bsz, h, t, dk = q.shape
    dv = v.shape[-1]
    c = GLA_CHUNK
    n = t // c
    q, k, g = (z.reshape(bsz, h, n, c, dk) for z in (q, k, g))
    v = v.reshape(bsz, h, n, c, dv)
    bc = jnp.cumsum(g, axis=3)
    causal = jnp.tril(jnp.ones((c, c), dtype=bool))
    diff = bc[..., :, None, :] - bc[..., None, :, :]
    dec = jnp.exp(jnp.where(causal[:, :, None], diff, -jnp.inf))
    att = jnp.einsum('bhnid,bhnjd,bhnijd->bhnij', q, k, dec)
    o = jnp.einsum('bhnij,bhnjv->bhniv', att, v)
    b_last = bc[..., -1, :]
    u = jnp.einsum('bhncd,bhncv->nbhdv', k * jnp.exp(b_last[..., None, :] - bc), v)

    def step(s, inp):
        u_n, d_n = inp
        return s * d_n[..., None] + u_n, s

    s0 = jnp.zeros((bsz, h, dk, dv), jnp.float32)
    _, s_in = lax.scan(step, s0, (u, jnp.moveaxis(jnp.exp(b_last), 2, 0)))
    o = o + jnp.einsum('bhncd,nbhdv->bhncv', q * jnp.exp(bc), s_in)
    return o.reshape(bsz, h, t, dv)


def gla_mixer(p, gw2_f, gb_f, gw2_b, gb_b, norm_w):
    p = p.astype(jnp.float32)
    bsz, t, _ = p.shape
    q, k, v, gl_f, gl_b, og = _split(p, GLA_SPLITS)

    def heads(z, d):
        return jnp.moveaxis(z.reshape(z.shape[:-1] + (GLA_HEADS, d)), 2, 1)

    q = heads(q, GLA_DK) * (GLA_DK ** -0.5)
    k = heads(k, GLA_DK)
    v = heads(v, GLA_DV)
    g_f = heads(jax.nn.log_sigmoid(gl_f @ gw2_f + gb_f) / GLA_TAU, GLA_DK)
    g_b = heads(jax.nn.log_sigmoid(gl_b @ gw2_b + gb_b) / GLA_TAU, GLA_DK)

    def flip(z):
        return jnp.flip(z, axis=2)

    o = gla_chunked(q, k, v, g_f) + flip(gla_chunked(flip(q), flip(k), flip(v), flip(g_b)))
    o = o * lax.rsqrt(jnp.mean(o * o, axis=-1, keepdims=True) + RMS_EPS)
    o = jnp.moveaxis(o, 1, 2).reshape(bsz, t, GLA_V_WIDTH)
    return o * norm_w * jax.nn.silu(og)


def hier_moe(h, w_coarse, b_coarse, w_fine, b_fine, w_gate, w_up, w_down):
    bsz, t, d = h.shape
    n_tok = bsz * t
    hf = h.reshape(n_tok, d)
    tok_idx = jnp.arange(n_tok)
    coarse = (hf @ w_coarse + b_coarse).astype(jnp.float32)
    pc = jax.nn.softmax(coarse, axis=-1)
    gsel = jnp.argmax(coarse, axis=-1)
    pg = pc[tok_idx, gsel]
    fine = (hf @ w_fine + b_fine).astype(jnp.float32).reshape(n_tok, N_GROUPS, EXPERTS_PER_GROUP)
    pf = jax.nn.softmax(fine[tok_idx, gsel], axis=-1)
    top_p, top_e = lax.top_k(pf, TOP_K)
    comb = pg[:, None] * top_p / jnp.sum(top_p, axis=-1, keepdims=True)
    eid = (gsel[:, None] * EXPERTS_PER_GROUP + top_e).reshape(-1).astype(jnp.int32)
    n_asg = n_tok * TOP_K
    tok = jnp.repeat(tok_idx, TOP_K).astype(jnp.int32)
    wts = comb.reshape(-1)
    order = jnp.argsort(eid, stable=True)
    se, st, sw = eid[order], tok[order], wts[order]
    counts = jnp.bincount(eid, length=N_EXPERTS).astype(jnp.int32)
    starts = jnp.cumsum(counts) - counts
    padded = ((counts + MOE_BLOCK - 1) // MOE_BLOCK) * MOE_BLOCK
    pends = jnp.cumsum(padded)
    pstarts = pends - padded
    dest = pstarts[se] + jnp.arange(n_asg, dtype=jnp.int32) - starts[se]
    n_blocks = -(-n_asg // MOE_BLOCK) + N_EXPERTS
    n_rows = n_blocks * MOE_BLOCK
    row_tok = jnp.zeros((n_rows,), jnp.int32).at[dest].set(st)
    row_w = jnp.zeros((n_rows,), jnp.float32).at[dest].set(sw)
    block_e = jnp.minimum(jnp.searchsorted(pends, jnp.arange(n_blocks, dtype=jnp.int32) * MOE_BLOCK, side='right'), N_EXPERTS - 1)
    xs = hf[row_tok].reshape(n_blocks, MOE_BLOCK, d)

    def expert_block(args):
        xb, e = args
        return (jax.nn.silu(xb @ w_gate[e]) * (xb @ w_up[e])) @ w_down[e]

    ys = lax.map(expert_block, (xs, block_e)).reshape(n_rows, d)
    out = jax.ops.segment_sum(ys * row_w[:, None].astype(ys.dtype), row_tok, num_segments=n_tok)
    return out.reshape(bsz, t, d)


def setup_inputs(seed: int = 0) -> dict:
    key = jax.random.key(seed)
    ks = iter(jax.random.split(key, 40))

    def nrm(shape, scale):
        return jax.random.normal(next(ks), shape, jnp.float32) * scale

    def unif(shape, lo, hi):
        return jax.random.uniform(next(ks), shape, jnp.float32, lo, hi)

    L, D = DEPTH, D_MODEL
    return {
        'x': nrm((BATCH, SEQ, D), 1.0),
        'ln1_w': 1.0 + nrm((L, D), 0.05),
        'w_in': nrm((L, D, D_IN_PROJ), D ** -0.5),
        'rw_mu': unif((L, RWKV_COLS), 0.0, 1.0),
        'rw_w0_f': unif((L, RWKV_WIDTH), -6.0, -1.0),
        'rw_w2_f': nrm((L, DECAY_LORA, RWKV_WIDTH), 0.1),
        'rw_w0_b': unif((L, RWKV_WIDTH), -6.0, -1.0),
        'rw_w2_b': nrm((L, DECAY_LORA, RWKV_WIDTH), 0.1),
        'rw_a0_f': nrm((L, RWKV_WIDTH), 0.1),
        'rw_a2_f': nrm((L, ICLR_LORA, RWKV_WIDTH), 0.5 * ICLR_LORA ** -0.5),
        'rw_a0_b': nrm((L, RWKV_WIDTH), 0.1),
        'rw_a2_b': nrm((L, ICLR_LORA, RWKV_WIDTH), 0.5 * ICLR_LORA ** -0.5),
        'rw_g2': nrm((L, GATE_LORA, RWKV_WIDTH), GATE_LORA ** -0.5),
        'rw_k_k': 0.85 + nrm((L, RWKV_WIDTH), 0.05),
        'rw_k_a': 1.0 + nrm((L, RWKV_WIDTH), 0.05),
        'rw_r_k': nrm((L, RWKV_HEADS, RWKV_HEAD_DIM), 0.1),
        'rw_ln_w': 1.0 + nrm((L, RWKV_WIDTH), 0.05),
        'rw_ln_b': nrm((L, RWKV_WIDTH), 0.01),
        'gla_gw2_f': nrm((L, GLA_GATE_LORA, GLA_K_WIDTH), GLA_GATE_LORA ** -0.5),
        'gla_gb_f': 2.0 + nrm((L, GLA_K_WIDTH), 0.5),
        'gla_gw2_b': nrm((L, GLA_GATE_LORA, GLA_K_WIDTH), GLA_GATE_LORA ** -0.5),
        'gla_gb_b': 2.0 + nrm((L, GLA_K_WIDTH), 0.5),
        'gla_norm_w': 1.0 + nrm((L, GLA_V_WIDTH), 0.05),
        'w_out': nrm((L, D_MIX, D), D_MIX ** -0.5),
        'ln2_w': 1.0 + nrm((L, D), 0.05),
        'moe_w_coarse': nrm((L, D, N_GROUPS), D ** -0.5),
        'moe_b_coarse': nrm((L, N_GROUPS), 0.01),
        'moe_w_fine': nrm((L, D, N_EXPERTS), D ** -0.5),
        'moe_b_fine': nrm((L, N_EXPERTS), 0.01),
        'moe_w_gate': nrm((L, N_EXPERTS, D, D_EXPERT), D ** -0.5),
        'moe_w_up': nrm((L, N_EXPERTS, D, D_EXPERT), D ** -0.5),
        'moe_w_down': nrm((L, N_EXPERTS, D_EXPERT, D), D_EXPERT ** -0.5),
        'ln_f_w': 1.0 + nrm((D,), 0.05),
    }


def reference(x, ln1_w, w_in, rw_mu, rw_w0_f, rw_w2_f, rw_w0_b, rw_w2_b, rw_a0_f, rw_a2_f, rw_a0_b, rw_a2_b, rw_g2, rw_k_k, rw_k_a, rw_r_k, rw_ln_w, rw_ln_b, gla_gw2_f, gla_gb_f, gla_gw2_b, gla_gb_b, gla_norm_w, w_out, ln2_w, moe_w_coarse, moe_b_coarse, moe_w_fine, moe_b_fine, moe_w_gate, moe_w_up, moe_w_down, ln_f_w):
    for l in range(DEPTH):
        h = rmsnorm(x, ln1_w[l])
        p = h @ w_in[l]
        y_rw = rwkv7_mixer(p[..., :RWKV_COLS], rw_mu[l], rw_w0_f[l], rw_w2_f[l], rw_w0_b[l], rw_w2_b[l],
                           rw_a0_f[l], rw_a2_f[l], rw_a0_b[l], rw_a2_b[l], rw_g2[l], rw_k_k[l], rw_k_a[l],
                           rw_r_k[l], rw_ln_w[l], rw_ln_b[l])
        y_gla = gla_mixer(p[..., RWKV_COLS:], gla_gw2_f[l], gla_gb_f[l], gla_gw2_b[l], gla_gb_b[l], gla_norm_w[l])
        mix = jnp.concatenate([y_rw, y_gla], axis=-1).astype(x.dtype)
        x = x + mix @ w_out[l]
        x = x + hier_moe(rmsnorm(x, ln2_w[l]), moe_w_coarse[l], moe_b_coarse[l], moe_w_fine[l], moe_b_fine[l],
                         moe_w_gate[l], moe_w_up[l], moe_w_down[l])
    return rmsnorm(x, ln_f_w)
```

```python
import functools

import jax
import jax.numpy as jnp
from jax import lax
from jax.experimental import pallas as pl
from jax.experimental.pallas import tpu as pltpu

F32 = jnp.float32
BF16 = jnp.bfloat16

RMS_EPS = 1e-6
RWKV_GN_EPS = 64e-5
RWKV_WIDTH = 512
RWKV_HEAD_DIM = 64
LORA = 64
GATE_LORA = 128
GLA_HEADS = 4
GLA_DK = 64
GLA_DV = 128
GLA_K_WIDTH = GLA_HEADS * GLA_DK
GLA_V_WIDTH = GLA_HEADS * GLA_DV
GLA_GATE_LORA = 16
GLA_TAU = 16.0
N_GROUPS = 4
EXPERTS_PER_GROUP = 8
N_EXPERTS = N_GROUPS * EXPERTS_PER_GROUP

CHUNK = 64
LANES = 128

COL_R, COL_K, COL_V = 0, 512, 1024
COL_LORA = 1536
COL_GL = 1792
COL_GG = 1920
COL_GQ, COL_GK, COL_GV, COL_OG = 2048, 2304, 2560, 3072
NP = 3584

VMEM_LIMIT = 56 * 1024 * 1024


def _cparams(sem):
    return pltpu.CompilerParams(dimension_semantics=sem, vmem_limit_bytes=VMEM_LIMIT)


MM_DTYPE = BF16


def _mm(a, b, ca=1, cb=0):
    dt = MM_DTYPE
    return lax.dot_general(a.astype(dt), b.astype(dt), (((ca,), (cb,)), ((), ())),
                           preferred_element_type=F32)


def _split2(x):
    hi = x.astype(BF16)
    lo = (x - hi.astype(F32)).astype(BF16)
    return hi, lo


def _split3(x):
    hi = x.astype(BF16)
    r1 = x - hi.astype(F32)
    mid = r1.astype(BF16)
    lo = (r1 - mid.astype(F32)).astype(BF16)
    return hi, mid, lo


def _mm_ones(x, ones_bf16):
    hi, mid, lo = _split3(x)
    return (_mm(lo, ones_bf16) + _mm(mid, ones_bf16)) + _mm(hi, ones_bf16)


def _mm_x3(a, b, ca=1, cb=0):
    ah, al = _split2(a)
    bh, bl = _split2(b)
    return (_mm(al, bh, ca, cb) + _mm(ah, bl, ca, cb)) + _mm(ah, bh, ca, cb)


def _rmsnorm_rows(x, w):
    ms = jnp.mean(x * x, axis=-1, keepdims=True)
    return x * lax.rsqrt(ms + RMS_EPS) * w


def _softplus(x):
    return jnp.maximum(x, 0.0) + jnp.log(1.0 + jnp.exp(-jnp.abs(x)))


def _sigmoid(x):
    return 1.0 / (1.0 + jnp.exp(-x))


def _seg_cumsum(x, seg, rev):
    n = x.shape[0]
    pos = lax.broadcasted_iota(jnp.int32, x.shape, 0) & (seg - 1)
    sh = 1
    while sh < seg:
        if not rev:
            x = x + jnp.where(pos >= sh, pltpu.roll(x, sh, 0), 0.0)
        else:
            x = x + jnp.where(pos < seg - sh, pltpu.roll(x, n - sh, 0), 0.0)
        sh *= 2
    return x


def _blockdiag2(y, m0):
    return jnp.concatenate([jnp.where(m0, y, 0.0), jnp.where(m0, 0.0, y)], axis=0)


def _inproj_kernel(x_ref, xp_ref, xn_ref, lnw_ref, w_ref, mu_ref, o_ref, h_scr, hh_scr, *,
                   tiles_per_seq):
    i = pl.program_id(0)
    j = pl.program_id(1)

    @pl.when(j == 0)
    def _():
        lnw = lnw_ref[...]
        h_scr[...] = _rmsnorm_rows(x_ref[...], lnw).astype(BF16)
        t = i % tiles_per_seq
        hp = jnp.where(t == 0, 0.0, _rmsnorm_rows(xp_ref[...], lnw))
        hn = jnp.where(t == tiles_per_seq - 1, 0.0, _rmsnorm_rows(xn_ref[...], lnw))
        hh_scr[...] = jnp.concatenate([hp, hn], axis=0).astype(BF16)

    w = w_ref[...]
    p = jnp.dot(h_scr[...], w, preferred_element_type=F32)
    ph = jnp.dot(hh_scr[...], w, preferred_element_type=F32)
    tm = p.shape[0]
    row = lax.broadcasted_iota(jnp.int32, p.shape, 0)
    prev = jnp.where(row == 0, ph[7:8, :], pltpu.roll(p, 1, 0))
    nxt = jnp.where(row == tm - 1, ph[8:9, :], pltpu.roll(p, tm - 1, 0))
    o_ref[...] = p + mu_ref[...] * (0.5 * (prev + nxt) - p)


def _inproj(xf, lnw, w_pad, mu_pad, seq_len, tm=512, tn=512):
    n, d = xf.shape
    npad = w_pad.shape[1]
    assert seq_len % tm == 0 and npad % tn == 0
    tps = seq_len // tm
    nb8 = n // 8
    return pl.pallas_call(
        functools.partial(_inproj_kernel, tiles_per_seq=tps),
        grid=(n // tm, npad // tn),
        in_specs=[
            pl.BlockSpec((tm, d), lambda i, j: (i, 0)),
            pl.BlockSpec((8, d), lambda i, j: (jnp.maximum(i * (tm // 8) - 1, 0), 0)),
            pl.BlockSpec((8, d), lambda i, j: (jnp.minimum((i + 1) * (tm // 8), nb8 - 1), 0)),
            pl.BlockSpec((1, d), lambda i, j: (0, 0)),
            pl.BlockSpec((d, tn), lambda i, j: (0, j)),
            pl.BlockSpec((1, tn), lambda i, j: (0, j)),
        ],
        out_specs=pl.BlockSpec((tm, tn), lambda i, j: (i, j)),
        out_shape=jax.ShapeDtypeStruct((n, npad), F32),
        scratch_shapes=[pltpu.VMEM((tm, d), BF16), pltpu.VMEM((16, d), BF16)],
        compiler_params=_cparams(("parallel", "arbitrary")),
        name="inproj_shift",
    )(xf, xf, xf, lnw, w_pad, mu_pad)


def _rwkv_chunk(d, r_ref, k_ref, v_ref, l_ref, w0_ref, w2_ref, a0_ref, a2_ref,
                kk_ref, ka_ref, rk_ref, blk_ref, y_ref, bd_ref, m_scr):
    rev = d == 1
    c = CHUNK
    r = r_ref[...]
    k = k_ref[...]
    v = v_ref[...]
    lo = l_ref[...]
    blk = blk_ref[...]

    zw = w0_ref[d] + _mm(jnp.tanh(lo[:, 0:LANES]), w2_ref[d])
    lw = -jnp.exp(-_softplus(-zw) - 0.5)
    a = _sigmoid(a0_ref[d] + _mm(lo[:, LANES:2 * LANES], a2_ref[d]))
    kkr = k * kk_ref[...]
    ss = _mm_ones(kkr * kkr, blk)
    kk = kkr / jnp.maximum(jnp.sqrt(ss), 1e-12)
    kd = k * (1.0 + (a - 1.0) * ka_ref[...])
    b = kk * a
    bd_ref[...] = _mm_ones(r * kd * rk_ref[...], blk)

    cum = _seg_cumsum(lw, c, rev)
    tot = cum[0:1, :] if rev else cum[c - 1:c, :]
    gam = jnp.exp(cum)
    at_all = -kk * jnp.exp(cum - lw)
    rt_all = r * gam
    ginv = jnp.exp(-cum)
    bt_all = b * ginv
    kt_all = kd * ginv
    gend = jnp.exp(tot - cum)
    bh_all = b * gend
    kh_all = kd * gend
    gc_all = jnp.exp(tot)

    row = lax.broadcasted_iota(jnp.int32, (c, LANES), 0)
    lane = lax.broadcasted_iota(jnp.int32, (c, LANES), 1)
    s_idx = lane & (RWKV_HEAD_DIM - 1)
    m0 = lane < RWKV_HEAD_DIM
    if rev:
        strict = s_idx > row
        incl = s_idx >= row
    else:
        strict = s_idx < row
        incl = s_idx <= row
    diag = s_idx == row

    def fold(z):
        return jnp.where(m0, z[:c], 0.0) + jnp.where(m0, 0.0, z[c:])

    for p in range(RWKV_WIDTH // LANES):
        sl = slice(p * LANES, (p + 1) * LANES)
        at, rt, bt, kt = at_all[:, sl], rt_all[:, sl], bt_all[:, sl], kt_all[:, sl]
        bh, kh, vp = bh_all[:, sl], kh_all[:, sl], v[:, sl]

        x2 = jnp.concatenate([at, rt], axis=0)
        sb = _mm(x2, _blockdiag2(bt, m0), 1, 1)
        sk = _mm(x2, _blockdiag2(kt, m0), 1, 1)
        lab = jnp.where(strict, sb[:c], 0.0)
        prb = jnp.where(incl, sb[c:], 0.0)
        lak = jnp.where(strict, sk[:c], 0.0)
        prk = jnp.where(incl, sk[c:], 0.0)

        bdv = _blockdiag2(vp, m0)
        xa = at
        xv = _mm(lak, bdv)
        lp = lab
        n_it = 6
        for it in range(n_it):
            rhs = jnp.concatenate([_blockdiag2(xa, m0), _blockdiag2(xv, m0)], axis=1)
            upd = _mm(lp, rhs)
            xa = xa + upd[:, :LANES]
            xv = xv + upd[:, LANES:]
            if it < n_it - 1:
                lp = _mm(lp, _blockdiag2(lp, m0))

        rhs = jnp.concatenate([_blockdiag2(xa, m0), _blockdiag2(xv, m0)], axis=1)
        px = _mm(prb, rhs)
        qp = rt + px[:, :LANES]
        yl = px[:, LANES:] + _mm(prk, bdv)
        zb = _mm(bh, jnp.concatenate([xa, xv], axis=1), 0, 0)
        zk = _mm(kh, vp, 0, 0)
        g = jnp.where(diag, gc_all[:, sl], 0.0) + fold(zb[:, :LANES])
        h = fold(zb[:, LANES:]) + fold(zk)

        m = m_scr[d, p]
        bm = _blockdiag2(m, m0)
        y_ref[:, sl] = _mm(qp, bm) + yl
        m_scr[d, p] = _mm(g, bm) + h


def _rwkv_kernel(rf, kf, vf, lf, rb, kb, vb, lb, w0, w2, a0, a2, kk, ka, rk, blk,
                 yf, yb, bdf, bdb, m_scr):
    @pl.when(pl.program_id(1) == 0)
    def _():
        m_scr[...] = jnp.zeros(m_scr.shape, F32)

    _rwkv_chunk(0, rf, kf, vf, lf, w0, w2, a0, a2, kk, ka, rk, blk, yf, bdf, m_scr)
    _rwkv_chunk(1, rb, kb, vb, lb, w0, w2, a0, a2, kk, ka, rk, blk, yb, bdb, m_scr)


def _rwkv_scan(p, batch, seq_len, w0, w2, a0, a2, kk, ka, rk, blk):
    n = p.shape[0]
    nc = seq_len // CHUNK
    w = RWKV_WIDTH

    def fwd(col, width):
        return pl.BlockSpec((CHUNK, width), lambda b, c: (b * nc + c, col // width))

    def bwd(col, width):
        return pl.BlockSpec((CHUNK, width), lambda b, c: (b * nc + nc - 1 - c, col // width))

    def full(a):
        return pl.BlockSpec(a.shape, lambda b, c: (0,) * a.ndim)

    lora_w = 2 * LANES
    in_specs = [fwd(COL_R, w), fwd(COL_K, w), fwd(COL_V, w), fwd(COL_LORA, lora_w),
                bwd(COL_R, w), bwd(COL_K, w), bwd(COL_V, w), bwd(COL_LORA, lora_w),
                full(w0), full(w2), full(a0), full(a2), full(kk), full(ka), full(rk), full(blk)]
    ospec_f = pl.BlockSpec((CHUNK, w), lambda b, c: (b * nc + c, 0))
    ospec_b = pl.BlockSpec((CHUNK, w), lambda b, c: (b * nc + nc - 1 - c, 0))
    osh = jax.ShapeDtypeStruct((n, w), F32)
    return pl.pallas_call(
        _rwkv_kernel,
        grid=(batch, nc),
        in_specs=in_specs,
        out_specs=[ospec_f, ospec_b, ospec_f, ospec_b],
        out_shape=[osh, osh, osh, osh],
        scratch_shapes=[pltpu.VMEM((2, w // LANES, CHUNK, LANES), F32)],
        compiler_params=_cparams(("parallel", "arbitrary")),
        name="rwkv7_scan",
    )(p, p, p, p, p, p, p, p, w0, w2, a0, a2, kk, ka, rk, blk)


def _gla_chunk(d, g_ref, q_ref, k_ref, v_ref, gw_ref, gb_ref, o_ref, st_scr):
    rev = d == 1
    c = CHUNK
    q = q_ref[...] * (GLA_DK ** -0.5)
    k = k_ref[...]
    v = v_ref[...]
    xg = _mm(g_ref[...], gw_ref[d]) + gb_ref[d]
    g = -_softplus(-xg) / GLA_TAU

    row = lax.broadcasted_iota(jnp.int32, (c, LANES), 0)
    lane = lax.broadcasted_iota(jnp.int32, (c, LANES), 1)
    s_idx = lane & (GLA_DK - 1)
    m0 = lane < GLA_DK
    prog = (c - 1 - row) if rev else row
    wide_row = lax.broadcasted_iota(jnp.int32, g.shape, 0)
    wide_prog = (c - 1 - wide_row) if rev else wide_row

    levels = []
    s = 1
    while s < c:
        if s == 1:
            cs_p = g
            cs_o = g
        else:
            cs_p = _seg_cumsum(g, s, rev)
            cs_o = _seg_cumsum(g, s, not rev)
        upper = (wide_prog & (2 * s - 1)) >= s
        fq = jnp.where(upper, jnp.exp(cs_p), 0.0)
        fk = jnp.where(upper, 0.0, jnp.exp(cs_o - g))
        same = (row // (2 * s)) == (s_idx // (2 * s))
        levels.append((q * fq, k * fk, same))
        s *= 2
    bc = _seg_cumsum(g, c, rev)
    bo = _seg_cumsum(g, c, not rev)
    tot = bc[0:1, :] if rev else bc[c - 1:c, :]
    qt_all = q * jnp.exp(bc)
    kh_all = k * jnp.exp(bo - g)
    gc_all = jnp.exp(tot)
    diag = row == s_idx
    del prog

    for p in range(GLA_K_WIDTH // LANES):
        sl = slice(p * LANES, (p + 1) * LANES)
        a = jnp.where(diag, _mm(q[:, sl], _blockdiag2(k[:, sl], m0), 1, 1), 0.0)
        for ql, kl, same in levels:
            a = a + jnp.where(same, _mm(ql[:, sl], _blockdiag2(kl[:, sl], m0), 1, 1), 0.0)
        st = st_scr[d, p]
        v0 = v[:, (2 * p) * GLA_DV:(2 * p + 1) * GLA_DV]
        v1 = v[:, (2 * p + 1) * GLA_DV:(2 * p + 2) * GLA_DV]
        o = (_mm(_blockdiag2(qt_all[:, sl], m0), st, 1, 1)
             + _mm(_blockdiag2(a, m0), jnp.concatenate([v0, v1], axis=0)))
        o_ref[:, (2 * p) * GLA_DV:(2 * p + 1) * GLA_DV] = o[:c]
        o_ref[:, (2 * p + 1) * GLA_DV:(2 * p + 2) * GLA_DV] = o[c:]
        z = _mm(jnp.concatenate([v0, v1], axis=1), kh_all[:, sl], 0, 0)
        lane_v = lax.broadcasted_iota(jnp.int32, (GLA_DV, LANES), 1)
        mv = lane_v < GLA_DK
        st_scr[d, p] = (st * gc_all[:, sl] + jnp.where(mv, z[:GLA_DV], 0.0)
                        + jnp.where(mv, 0.0, z[GLA_DV:]))


def _gla_kernel(gf, qf, kf, vf, gb_, qb, kb, vb, gw, gbias, of, ob, st_scr):
    @pl.when(pl.program_id(1) == 0)
    def _():
        st_scr[...] = jnp.zeros(st_scr.shape, F32)

    _gla_chunk(0, gf, qf, kf, vf, gw, gbias, of, st_scr)
    _gla_chunk(1, gb_, qb, kb, vb, gw, gbias, ob, st_scr)


def _gla_scan(p, batch, seq_len, gw, gbias):
    n = p.shape[0]
    nc = seq_len // CHUNK

    def fwd(col, width):
        return pl.BlockSpec((CHUNK, width), lambda b, c: (b * nc + c, col // width))

    def bwd(col, width):
        return pl.BlockSpec((CHUNK, width), lambda b, c: (b * nc + nc - 1 - c, col // width))

    def full(a):
        return pl.BlockSpec(a.shape, lambda b, c: (0,) * a.ndim)

    kw, vw = GLA_K_WIDTH, GLA_V_WIDTH
    in_specs = [fwd(COL_GG, LANES), fwd(COL_GQ, kw), fwd(COL_GK, kw), fwd(COL_GV, vw),
                bwd(COL_GG, LANES), bwd(COL_GQ, kw), bwd(COL_GK, kw), bwd(COL_GV, vw),
                full(gw), full(gbias)]
    ospec_f = pl.BlockSpec((CHUNK, vw), lambda b, c: (b * nc + c, 0))
    ospec_b = pl.BlockSpec((CHUNK, vw), lambda b, c: (b * nc + nc - 1 - c, 0))
    osh = jax.ShapeDtypeStruct((n, vw), F32)
    return pl.pallas_call(
        _gla_kernel,
        grid=(batch, nc),
        in_specs=in_specs,
        out_specs=[ospec_f, ospec_b],
        out_shape=[osh, osh],
        scratch_shapes=[pltpu.VMEM((2, kw // LANES, GLA_DV, LANES), F32)],
        compiler_params=_cparams(("parallel", "arbitrary")),
        name="gla_scan",
    )(p, p, p, p, p, p, p, p, gw, gbias)


def _mix_out_kernel(x_ref, yf, yb, bdf, bdb, v_ref, gl_ref, of, ob, og_ref,
                    g2_ref, lnw_ref, lnb_ref, blk_ref, nw_ref, wo_ref, o_ref):
    blk = blk_ref[...]
    inv = 1.0 / RWKV_HEAD_DIM
    y = yf[...] + yb[...]
    mean = _mm_ones(y, blk) * inv
    yc = y - mean
    var = _mm_ones(yc * yc, blk) * inv
    yn = yc * lax.rsqrt(var + RWKV_GN_EPS) * lnw_ref[...] + lnb_ref[...]
    bonus = (bdf[...] + bdb[...]) * v_ref[...]
    gate = _mm(_sigmoid(gl_ref[...]), g2_ref[...])
    y_rw = (yn + bonus) * gate

    o = of[...] + ob[...]
    og = og_ref[...]
    gsil = og * _sigmoid(og)
    nw = nw_ref[...]
    acc = x_ref[...] + _mm(y_rw, wo_ref[0:RWKV_WIDTH, :])
    for h in range(GLA_HEADS):
        sl = slice(h * GLA_DV, (h + 1) * GLA_DV)
        oh = o[:, sl]
        ms = jnp.mean(oh * oh, axis=-1, keepdims=True)
        yg = oh * lax.rsqrt(ms + RMS_EPS) * nw[:, sl] * gsil[:, sl]
        acc = acc + _mm(yg, wo_ref[RWKV_WIDTH + h * GLA_DV:RWKV_WIDTH + (h + 1) * GLA_DV, :])
    o_ref[...] = acc


def _mix_out(xf, p, yf, yb, bdf, bdb, of, ob, g2, lnw, lnb, blk, nw, wo, tm=256):
    n, d = xf.shape
    w = RWKV_WIDTH

    def rows(width, col=0):
        return pl.BlockSpec((tm, width), lambda i: (i, col // width))

    def full(a):
        return pl.BlockSpec(a.shape, lambda i: (0,) * a.ndim)

    in_specs = [rows(d), rows(w), rows(w), rows(w), rows(w), rows(w, COL_V), rows(LANES, COL_GL),
                rows(w), rows(w), rows(w, COL_OG),
                full(g2), full(lnw), full(lnb), full(blk), full(nw), full(wo)]
    return pl.pallas_call(
        _mix_out_kernel,
        grid=(n // tm,),
        in_specs=in_specs,
        out_specs=rows(d),
        out_shape=jax.ShapeDtypeStruct((n, d), F32),
        compiler_params=_cparams(("parallel",)),
        name="mix_outproj",
    )(xf, yf, yb, bdf, bdb, p, p, of, ob, p, g2, lnw, lnb, blk, nw, wo)


def _router_kernel(x_ref, lnw_ref, wr_ref, br_ref, h_ref, wt_ref):
    h = _rmsnorm_rows(x_ref[...], lnw_ref[...])
    h_ref[...] = h.astype(BF16)
    logits = _mm_x3(wr_ref[...], h, 1, 1) + br_ref[...]
    tm = logits.shape[1]
    coarse = logits[0:N_GROUPS, :]
    fine = logits[8:8 + N_EXPERTS, :]
    rowg = lax.broadcasted_iota(jnp.int32, (N_GROUPS, tm), 0)
    cmax = jnp.max(coarse, axis=0, keepdims=True)
    gsel = jnp.min(jnp.where(coarse == cmax, rowg, N_GROUPS), axis=0, keepdims=True)
    pg = 1.0 / jnp.sum(jnp.exp(coarse - cmax), axis=0, keepdims=True)
    sel = jnp.zeros((EXPERTS_PER_GROUP, tm), F32)
    for g in range(N_GROUPS):
        sel = sel + jnp.where(gsel == g, fine[g * EXPERTS_PER_GROUP:(g + 1) * EXPERTS_PER_GROUP, :], 0.0)
    rowe = lax.broadcasted_iota(jnp.int32, (EXPERTS_PER_GROUP, tm), 0)
    l1 = jnp.max(sel, axis=0, keepdims=True)
    i1 = jnp.min(jnp.where(sel == l1, rowe, EXPERTS_PER_GROUP), axis=0, keepdims=True)
    sel2 = jnp.where(rowe == i1, -jnp.inf, sel)
    l2 = jnp.max(sel2, axis=0, keepdims=True)
    i2 = jnp.min(jnp.where(sel2 == l2, rowe, EXPERTS_PER_GROUP), axis=0, keepdims=True)
    t = jnp.exp(l2 - l1)
    w1 = pg / (1.0 + t)
    w2 = pg * t / (1.0 + t)
    e1 = gsel * EXPERTS_PER_GROUP + i1
    e2 = gsel * EXPERTS_PER_GROUP + i2
    rowx = lax.broadcasted_iota(jnp.int32, (N_EXPERTS, tm), 0)
    wt = jnp.where(rowx == e1, w1, 0.0) + jnp.where(rowx == e2, w2, 0.0)
    wt_ref[...] = jnp.transpose(wt)


def _router(x1, lnw, wr, br, tm=256):
    n, d = x1.shape

    def full(a):
        return pl.BlockSpec(a.shape, lambda i: (0,) * a.ndim)

    return pl.pallas_call(
        _router_kernel,
        grid=(n // tm,),
        in_specs=[pl.BlockSpec((tm, d), lambda i: (i, 0)), full(lnw), full(wr), full(br)],
        out_specs=[pl.BlockSpec((tm, d), lambda i: (i, 0)),
                   pl.BlockSpec((tm, N_EXPERTS), lambda i: (i, 0))],
        out_shape=[jax.ShapeDtypeStruct((n, d), BF16),
                   jax.ShapeDtypeStruct((n, N_EXPERTS), F32)],
        compiler_params=_cparams(("parallel",)),
        name="router",
    )(x1, lnw, wr, br)


def _moe_kernel(h_ref, wt_ref, wg_ref, wu_ref, wd_ref, x_ref, lnf_ref, o_ref, acc, *, final_norm):
    e = pl.program_id(1)

    @pl.when(e == 0)
    def _():
        acc[...] = jnp.zeros(acc.shape, F32)

    h = h_ref[...]
    a = jnp.dot(h, wg_ref[0], preferred_element_type=F32)
    u = jnp.dot(h, wu_ref[0], preferred_element_type=F32)
    wt = wt_ref[...]
    lane = lax.broadcasted_iota(jnp.int32, wt.shape, 1)
    wcol = jnp.sum(jnp.where(lane == e, wt, 0.0), axis=1, keepdims=True)
    hh = (a * _sigmoid(a)) * u * wcol
    acc[...] += jnp.dot(hh.astype(BF16), wd_ref[0], preferred_element_type=F32)

    @pl.when(e == pl.num_programs(1) - 1)
    def _():
        y = x_ref[...] + acc[...]
        o_ref[...] = _rmsnorm_rows(y, lnf_ref[...]) if final_norm else y


def _moe(h2, wt, wg, wu, wd, x1, lnf, final_norm, tm=512):
    n, d = x1.shape
    ne, _, de = wg.shape
    return pl.pallas_call(
        functools.partial(_moe_kernel, final_norm=final_norm),
        grid=(n // tm, ne),
        in_specs=[pl.BlockSpec((tm, d), lambda i, e: (i, 0)),
                  pl.BlockSpec((tm, ne), lambda i, e: (i, 0)),
                  pl.BlockSpec((1, d, de), lambda i, e: (e, 0, 0)),
                  pl.BlockSpec((1, d, de), lambda i, e: (e, 0, 0)),
                  pl.BlockSpec((1, de, d), lambda i, e: (e, 0, 0)),
                  pl.BlockSpec((tm, d), lambda i, e: (i, 0)),
                  pl.BlockSpec((1, d), lambda i, e: (0, 0))],
        out_specs=pl.BlockSpec((tm, d), lambda i, e: (i, 0)),
        out_shape=jax.ShapeDtypeStruct((n, d), F32),
        scratch_shapes=[pltpu.VMEM((tm, d), F32)],
        compiler_params=_cparams(("parallel", "arbitrary")),
        name="moe_experts",
    )(h2, wt, wg, wu, wd, x1, lnf)


def _pad_rows(a, before, total):
    return jnp.pad(a, ((before, total - before - a.shape[0]), (0, 0)))


def _layer(xf, batch, seq_len, ln1_w, w_in, rw_mu, rw_w0_f, rw_w2_f, rw_w0_b, rw_w2_b, rw_a0_f,
           rw_a2_f, rw_a0_b, rw_a2_b, rw_g2, rw_k_k, rw_k_a, rw_r_k, rw_ln_w, rw_ln_b, gla_gw2_f,
           gla_gb_f, gla_gw2_b, gla_gb_b, gla_norm_w, w_out, ln2_w, moe_w_coarse, moe_b_coarse,
           moe_w_fine, moe_b_fine, moe_w_gate, moe_w_up, moe_w_down):
    d = xf.shape[1]
    rw_cols = 3 * RWKV_WIDTH + 4 * LORA + GATE_LORA
    w_rw, w_gla = w_in[:, :rw_cols], w_in[:, rw_cols:]
    o = 0
    w_gq = w_gla[:, o:o + GLA_K_WIDTH]; o += GLA_K_WIDTH
    w_gk = w_gla[:, o:o + GLA_K_WIDTH]; o += GLA_K_WIDTH
    w_gv = w_gla[:, o:o + GLA_V_WIDTH]; o += GLA_V_WIDTH
    w_gg = w_gla[:, o:o + 2 * GLA_GATE_LORA]; o += 2 * GLA_GATE_LORA
    w_og = w_gla[:, o:o + GLA_V_WIDTH]
    w_gg = jnp.pad(w_gg, ((0, 0), (0, LANES - 2 * GLA_GATE_LORA)))
    w_pad = jnp.concatenate([w_rw, w_gg, w_gq, w_gk, w_gv, w_og], axis=1).astype(BF16)
    mu_pad = jnp.pad(rw_mu, (0, NP - rw_cols))[None, :]

    p = _inproj(xf, ln1_w[None, :], w_pad, mu_pad, seq_len)

    head = lax.broadcasted_iota(jnp.int32, (RWKV_WIDTH, RWKV_WIDTH), 0) // RWKV_HEAD_DIM
    blk = (head == head.T).astype(BF16)
    w0 = jnp.stack([rw_w0_f, rw_w0_b])[:, None, :]
    a0 = jnp.stack([rw_a0_f, rw_a0_b])[:, None, :]
    w2 = jnp.stack([_pad_rows(rw_w2_f, 0, LANES), _pad_rows(rw_w2_b, LORA, LANES)]).astype(BF16)
    a2 = jnp.stack([_pad_rows(rw_a2_f, 0, LANES), _pad_rows(rw_a2_b, LORA, LANES)]).astype(BF16)
    yf, yb, bdf, bdb = _rwkv_scan(p, batch, seq_len, w0, w2, a0, a2, rw_k_k[None, :],
                                  rw_k_a[None, :], rw_r_k.reshape(1, -1), blk)

    gw = jnp.stack([_pad_rows(gla_gw2_f, 0, LANES),
                    _pad_rows(gla_gw2_b, GLA_GATE_LORA, LANES)]).astype(BF16)
    gbias = jnp.stack([gla_gb_f, gla_gb_b])[:, None, :]
    of, ob = _gla_scan(p, batch, seq_len, gw, gbias)

    x1 = _mix_out(xf, p, yf, yb, bdf, bdb, of, ob, rw_g2.astype(BF16), rw_ln_w[None, :],
                  rw_ln_b[None, :], blk, gla_norm_w[None, :], w_out.astype(BF16))

    wr = jnp.concatenate([moe_w_coarse.T, jnp.zeros((8 - N_GROUPS, d), F32), moe_w_fine.T], axis=0)
    br = jnp.concatenate([moe_b_coarse, jnp.zeros((8 - N_GROUPS,), F32), moe_b_fine])[:, None]
    h2, wt = _router(x1, ln2_w[None, :], wr, br)
    return h2, wt, x1


def kernel(x, ln1_w, w_in, rw_mu, rw_w0_f, rw_w2_f, rw_w0_b, rw_w2_b, rw_a0_f, rw_a2_f, rw_a0_b, rw_a2_b, rw_g2, rw_k_k, rw_k_a, rw_r_k, rw_ln_w, rw_ln_b, gla_gw2_f, gla_gb_f, gla_gw2_b, gla_gb_b, gla_norm_w, w_out, ln2_w, moe_w_coarse, moe_b_coarse, moe_w_fine, moe_b_fine, moe_w_gate, moe_w_up, moe_w_down, ln_f_w):
    batch, seq_len, d = x.shape
    xf = x.reshape(batch * seq_len, d)
    depth = w_in.shape[0]
    for l in range(depth):
        h2, wt, x1 = _layer(
            xf, batch, seq_len, ln1_w[l], w_in[l], rw_mu[l], rw_w0_f[l], rw_w2_f[l], rw_w0_b[l],
            rw_w2_b[l], rw_a0_f[l], rw_a2_f[l], rw_a0_b[l], rw_a2_b[l], rw_g2[l], rw_k_k[l],
            rw_k_a[l], rw_r_k[l], rw_ln_w[l], rw_ln_b[l], gla_gw2_f[l], gla_gb_f[l], gla_gw2_b[l],
            gla_gb_b[l], gla_norm_w[l], w_out[l], ln2_w[l], moe_w_coarse[l], moe_b_coarse[l],
            moe_w_fine[l], moe_b_fine[l], moe_w_gate[l], moe_w_up[l], moe_w_down[l])
        xf = _moe(h2, wt, moe_w_gate[l].astype(BF16), moe_w_up[l].astype(BF16),
                  moe_w_down[l].astype(BF16), x1, ln_f_w[None, :], l == depth - 1)
    return xf.reshape(batch, seq_len, d)
```

```python
import functools

import jax
import jax.numpy as jnp
from jax import lax
from jax.experimental import pallas as pl
from jax.experimental.pallas import tpu as pltpu

F32 = jnp.float32
BF16 = jnp.bfloat16

RMS_EPS = 1e-6
RWKV_GN_EPS = 64e-5
RWKV_WIDTH = 512
RWKV_HEAD_DIM = 64
LORA = 64
GATE_LORA = 128
GLA_HEADS = 4
GLA_DK = 64
GLA_DV = 128
GLA_K_WIDTH = GLA_HEADS * GLA_DK
GLA_V_WIDTH = GLA_HEADS * GLA_DV
GLA_GATE_LORA = 16
GLA_TAU = 16.0
N_GROUPS = 4
EXPERTS_PER_GROUP = 8
N_EXPERTS = N_GROUPS * EXPERTS_PER_GROUP

CHUNK = 64
LANES = 128

COL_R, COL_K, COL_V = 0, 512, 1024
COL_LORA = 1536
COL_GL = 1792
COL_GG = 1920
COL_GQ, COL_GK, COL_GV, COL_OG = 2048, 2304, 2560, 3072
NP = 3584

VMEM_LIMIT = 56 * 1024 * 1024


def _cparams(sem):
    return pltpu.CompilerParams(dimension_semantics=sem, vmem_limit_bytes=VMEM_LIMIT)


MM_DTYPE = BF16


def _mm(a, b, ca=1, cb=0):
    dt = MM_DTYPE
    return lax.dot_general(a.astype(dt), b.astype(dt), (((ca,), (cb,)), ((), ())),
                           preferred_element_type=F32)


def _split2(x):
    hi = x.astype(BF16)
    lo = (x - hi.astype(F32)).astype(BF16)
    return hi, lo


def _split3(x):
    hi = x.astype(BF16)
    r1 = x - hi.astype(F32)
    mid = r1.astype(BF16)
    lo = (r1 - mid.astype(F32)).astype(BF16)
    return hi, mid, lo


def _mm_ones(x, ones_bf16):
    hi, mid, lo = _split3(x)
    return (_mm(lo, ones_bf16) + _mm(mid, ones_bf16)) + _mm(hi, ones_bf16)


def _mm_x3(a, b, ca=1, cb=0):
    ah, al = _split2(a)
    bh, bl = _split2(b)
    return (_mm(al, bh, ca, cb) + _mm(ah, bl, ca, cb)) + _mm(ah, bh, ca, cb)


def _rmsnorm_rows(x, w):
    ms = jnp.mean(x * x, axis=-1, keepdims=True)
    return x * lax.rsqrt(ms + RMS_EPS) * w


def _softplus(x):
    return jnp.maximum(x, 0.0) + jnp.log(1.0 + jnp.exp(-jnp.abs(x)))


def _sigmoid(x):
    return 1.0 / (1.0 + jnp.exp(-x))


def _seg_cumsum(x, seg, rev):
    n = x.shape[0]
    pos = lax.broadcasted_iota(jnp.int32, x.shape, 0) & (seg - 1)
    sh = 1
    while sh < seg:
        if not rev:
            x = x + jnp.where(pos >= sh, pltpu.roll(x, sh, 0), 0.0)
        else:
            x = x + jnp.where(pos < seg - sh, pltpu.roll(x, n - sh, 0), 0.0)
        sh *= 2
    return x


def _blockdiag2(y, m0):
    return jnp.concatenate([jnp.where(m0, y, 0.0), jnp.where(m0, 0.0, y)], axis=0)


def _inproj_kernel(x_ref, xp_ref, xn_ref, lnw_ref, w_ref, mu_ref, o_ref, h_scr, hh_scr, *,
                   tiles_per_seq):
    i = pl.program_id(0)
    j = pl.program_id(1)

    @pl.when(j == 0)
    def _():
        lnw = lnw_ref[...]
        h_scr[...] = _rmsnorm_rows(x_ref[...], lnw).astype(BF16)
        t = i % tiles_per_seq
        hp = jnp.where(t == 0, 0.0, _rmsnorm_rows(xp_ref[...], lnw))
        hn = jnp.where(t == tiles_per_seq - 1, 0.0, _rmsnorm_rows(xn_ref[...], lnw))
        hh_scr[...] = jnp.concatenate([hp, hn], axis=0).astype(BF16)

    w = w_ref[...]
    p = jnp.dot(h_scr[...], w, preferred_element_type=F32)
    ph = jnp.dot(hh_scr[...], w, preferred_element_type=F32)
    tm = p.shape[0]
    row = lax.broadcasted_iota(jnp.int32, p.shape, 0)
    prev = jnp.where(row == 0, ph[7:8, :], pltpu.roll(p, 1, 0))
    nxt = jnp.where(row == tm - 1, ph[8:9, :], pltpu.roll(p, tm - 1, 0))
    o_ref[...] = p + mu_ref[...] * (0.5 * (prev + nxt) - p)


def _inproj(xf, lnw, w_pad, mu_pad, seq_len, tm=512, tn=512):
    n, d = xf.shape
    npad = w_pad.shape[1]
    assert seq_len % tm == 0 and npad % tn == 0
    tps = seq_len // tm
    nb8 = n // 8
    return pl.pallas_call(
        functools.partial(_inproj_kernel, tiles_per_seq=tps),
        grid=(n // tm, npad // tn),
        in_specs=[
            pl.BlockSpec((tm, d), lambda i, j: (i, 0)),
            pl.BlockSpec((8, d), lambda i, j: (jnp.maximum(i * (tm // 8) - 1, 0), 0)),
            pl.BlockSpec((8, d), lambda i, j: (jnp.minimum((i + 1) * (tm // 8), nb8 - 1), 0)),
            pl.BlockSpec((1, d), lambda i, j: (0, 0)),
            pl.BlockSpec((d, tn), lambda i, j: (0, j)),
            pl.BlockSpec((1, tn), lambda i, j: (0, j)),
        ],
        out_specs=pl.BlockSpec((tm, tn), lambda i, j: (i, j)),
        out_shape=jax.ShapeDtypeStruct((n, npad), F32),
        scratch_shapes=[pltpu.VMEM((tm, d), BF16), pltpu.VMEM((16, d), BF16)],
        compiler_params=_cparams(("parallel", "arbitrary")),
        name="inproj_shift",
    )(xf, xf, xf, lnw, w_pad, mu_pad)


def _rwkv_prep(d, r_ref, k_ref, v_ref, l_ref, w0_ref, w2_ref, a0_ref, a2_ref,
               kk_ref, ka_ref, rk_ref, blk_ref, bd_ref):
    rev = d == 1
    c = CHUNK
    r = r_ref[...]
    k = k_ref[...]
    lo = l_ref[...]
    blk = blk_ref[...]

    zw = w0_ref[d] + _mm(jnp.tanh(lo[:, 0:LANES]), w2_ref[d])
    lw = -jnp.exp(-_softplus(-zw) - 0.5)
    a = _sigmoid(a0_ref[d] + _mm(lo[:, LANES:2 * LANES], a2_ref[d]))
    kkr = k * kk_ref[...]
    ss = _mm_ones(kkr * kkr, blk)
    kk = kkr / jnp.maximum(jnp.sqrt(ss), 1e-12)
    kd = k * (1.0 + (a - 1.0) * ka_ref[...])
    b = kk * a
    bd_ref[...] = _mm_ones(r * kd * rk_ref[...], blk)

    cum = _seg_cumsum(lw, c, rev)
    tot = cum[0:1, :] if rev else cum[c - 1:c, :]
    ginv = jnp.exp(-cum)
    gend = jnp.exp(tot - cum)
    return dict(at=-kk * jnp.exp(cum - lw), rt=r * jnp.exp(cum), bt=b * ginv, kt=kd * ginv,
                bh=b * gend, kh=kd * gend, gc=jnp.exp(tot), v=v_ref[...])


def _rwkv_kernel(rf, kf, vf, lf, rb, kb, vb, lb, w0, w2, a0, a2, kk, ka, rk, blk,
                 yf, yb, bdf, bdb, m_scr):
    @pl.when(pl.program_id(1) == 0)
    def _():
        m_scr[...] = jnp.zeros(m_scr.shape, F32)

    c = CHUNK
    prep = [_rwkv_prep(0, rf, kf, vf, lf, w0, w2, a0, a2, kk, ka, rk, blk, bdf),
            _rwkv_prep(1, rb, kb, vb, lb, w0, w2, a0, a2, kk, ka, rk, blk, bdb)]
    y_refs = [yf, yb]

    row = lax.broadcasted_iota(jnp.int32, (c, LANES), 0)
    lane = lax.broadcasted_iota(jnp.int32, (c, LANES), 1)
    s_idx = lane & (RWKV_HEAD_DIM - 1)
    m0 = lane < RWKV_HEAD_DIM
    strict = [s_idx < row, s_idx > row]
    incl = [s_idx <= row, s_idx >= row]
    diag = s_idx == row

    def bd2(y):
        return _blockdiag2(y, m0)

    def fold(z):
        return jnp.where(m0, z[:c], 0.0) + jnp.where(m0, 0.0, z[c:])

    chains = [(d, p) for p in range(RWKV_WIDTH // LANES) for d in (0, 1)]
    ds = [d for d, _ in chains]

    def get(name):
        return [prep[d][name][:, p * LANES:(p + 1) * LANES] for d, p in chains]

    at, rt, bt, kt, bh, kh, vp, gc = (get(n) for n in ("at", "rt", "bt", "kt", "bh", "kh", "v", "gc"))
    x2 = [jnp.concatenate([a_, r_], axis=0) for a_, r_ in zip(at, rt)]
    sb = [_mm(x, bd2(y), 1, 1) for x, y in zip(x2, bt)]
    sk = [_mm(x, bd2(y), 1, 1) for x, y in zip(x2, kt)]
    lab = [jnp.where(strict[d], z[:c], 0.0) for d, z in zip(ds, sb)]
    prb = [jnp.where(incl[d], z[c:], 0.0) for d, z in zip(ds, sb)]
    lak = [jnp.where(strict[d], z[:c], 0.0) for d, z in zip(ds, sk)]
    prk = [jnp.where(incl[d], z[c:], 0.0) for d, z in zip(ds, sk)]
    bdv = [bd2(y) for y in vp]
    xa = at
    xv = [_mm(x, y) for x, y in zip(lak, bdv)]
    lp = lab
    n_it = 6
    for it in range(n_it):
        upd = [_mm(l_, jnp.concatenate([bd2(a_), bd2(v_)], axis=1)) for l_, a_, v_ in zip(lp, xa, xv)]
        xa = [a_ + u[:, :LANES] for a_, u in zip(xa, upd)]
        xv = [v_ + u[:, LANES:] for v_, u in zip(xv, upd)]
        if it < n_it - 1:
            lp = [_mm(l_, bd2(l_)) for l_ in lp]
    px = [_mm(x, jnp.concatenate([bd2(a_), bd2(v_)], axis=1)) for x, a_, v_ in zip(prb, xa, xv)]
    pv = [_mm(x, y) for x, y in zip(prk, bdv)]
    zb = [_mm(x, jnp.concatenate([a_, v_], axis=1), 0, 0) for x, a_, v_ in zip(bh, xa, xv)]
    zk = [_mm(x, y, 0, 0) for x, y in zip(kh, vp)]
    for i, (d, p) in enumerate(chains):
        qp = rt[i] + px[i][:, :LANES]
        yl = px[i][:, LANES:] + pv[i]
        g = jnp.where(diag, gc[i], 0.0) + fold(zb[i][:, :LANES])
        h = fold(zb[i][:, LANES:]) + fold(zk[i])
        bm = bd2(m_scr[d, p])
        yg = _mm(jnp.concatenate([qp, g], axis=0), bm)
        y_refs[d][:, p * LANES:(p + 1) * LANES] = yg[:c] + yl
        m_scr[d, p] = yg[c:] + h


def _rwkv_scan(p, batch, seq_len, w0, w2, a0, a2, kk, ka, rk, blk):
    n = p.shape[0]
    nc = seq_len // CHUNK
    w = RWKV_WIDTH

    def fwd(col, width):
        return pl.BlockSpec((CHUNK, width), lambda b, c: (b * nc + c, col // width))

    def bwd(col, width):
        return pl.BlockSpec((CHUNK, width), lambda b, c: (b * nc + nc - 1 - c, col // width))

    def full(a):
        return pl.BlockSpec(a.shape, lambda b, c: (0,) * a.ndim)

    lora_w = 2 * LANES
    in_specs = [fwd(COL_R, w), fwd(COL_K, w), fwd(COL_V, w), fwd(COL_LORA, lora_w),
                bwd(COL_R, w), bwd(COL_K, w), bwd(COL_V, w), bwd(COL_LORA, lora_w),
                full(w0), full(w2), full(a0), full(a2), full(kk), full(ka), full(rk), full(blk)]
    ospec_f = pl.BlockSpec((CHUNK, w), lambda b, c: (b * nc + c, 0))
    ospec_b = pl.BlockSpec((CHUNK, w), lambda b, c: (b * nc + nc - 1 - c, 0))
    osh = jax.ShapeDtypeStruct((n, w), F32)
    return pl.pallas_call(
        _rwkv_kernel,
        grid=(batch, nc),
        in_specs=in_specs,
        out_specs=[ospec_f, ospec_b, ospec_f, ospec_b],
        out_shape=[osh, osh, osh, osh],
        scratch_shapes=[pltpu.VMEM((2, w // LANES, CHUNK, LANES), F32)],
        compiler_params=_cparams(("parallel", "arbitrary")),
        name="rwkv7_scan",
    )(p, p, p, p, p, p, p, p, w0, w2, a0, a2, kk, ka, rk, blk)


def _gla_chunk(d, g_ref, q_ref, k_ref, v_ref, gw_ref, gb_ref, o_ref, st_scr):
    rev = d == 1
    c = CHUNK
    q = q_ref[...] * (GLA_DK ** -0.5)
    k = k_ref[...]
    v = v_ref[...]
    xg = _mm(g_ref[...], gw_ref[d]) + gb_ref[d]
    g = -_softplus(-xg) / GLA_TAU

    row = lax.broadcasted_iota(jnp.int32, (c, LANES), 0)
    lane = lax.broadcasted_iota(jnp.int32, (c, LANES), 1)
    s_idx = lane & (GLA_DK - 1)
    m0 = lane < GLA_DK
    prog = (c - 1 - row) if rev else row
    wide_row = lax.broadcasted_iota(jnp.int32, g.shape, 0)
    wide_prog = (c - 1 - wide_row) if rev else wide_row

    levels = []
    s = 1
    while s < c:
        if s == 1:
            cs_p = g
            cs_o = g
        else:
            cs_p = _seg_cumsum(g, s, rev)
            cs_o = _seg_cumsum(g, s, not rev)
        upper = (wide_prog & (2 * s - 1)) >= s
        fq = jnp.where(upper, jnp.exp(cs_p), 0.0)
        fk = jnp.where(upper, 0.0, jnp.exp(cs_o - g))
        same = (row // (2 * s)) == (s_idx // (2 * s))
        levels.append((q * fq, k * fk, same))
        s *= 2
    bc = _seg_cumsum(g, c, rev)
    bo = _seg_cumsum(g, c, not rev)
    tot = bc[0:1, :] if rev else bc[c - 1:c, :]
    qt_all = q * jnp.exp(bc)
    kh_all = k * jnp.exp(bo - g)
    gc_all = jnp.exp(tot)
    diag = row == s_idx
    del prog

    for p in range(GLA_K_WIDTH // LANES):
        sl = slice(p * LANES, (p + 1) * LANES)
        a = jnp.where(diag, _mm(q[:, sl], _blockdiag2(k[:, sl], m0), 1, 1), 0.0)
        for ql, kl, same in levels:
            a = a + jnp.where(same, _mm(ql[:, sl], _blockdiag2(kl[:, sl], m0), 1, 1), 0.0)
        st = st_scr[d, p]
        v0 = v[:, (2 * p) * GLA_DV:(2 * p + 1) * GLA_DV]
        v1 = v[:, (2 * p + 1) * GLA_DV:(2 * p + 2) * GLA_DV]
        o = (_mm(_blockdiag2(qt_all[:, sl], m0), st, 1, 1)
             + _mm(_blockdiag2(a, m0), jnp.concatenate([v0, v1], axis=0)))
        o_ref[:, (2 * p) * GLA_DV:(2 * p + 1) * GLA_DV] = o[:c]
        o_ref[:, (2 * p + 1) * GLA_DV:(2 * p + 2) * GLA_DV] = o[c:]
        z = _mm(jnp.concatenate([v0, v1], axis=1), kh_all[:, sl], 0, 0)
        lane_v = lax.broadcasted_iota(jnp.int32, (GLA_DV, LANES), 1)
        mv = lane_v < GLA_DK
        st_scr[d, p] = (st * gc_all[:, sl] + jnp.where(mv, z[:GLA_DV], 0.0)
                        + jnp.where(mv, 0.0, z[GLA_DV:]))


def _gla_kernel(gf, qf, kf, vf, gb_, qb, kb, vb, gw, gbias, of, ob, st_scr):
    @pl.when(pl.program_id(1) == 0)
    def _():
        st_scr[...] = jnp.zeros(st_scr.shape, F32)

    _gla_chunk(0, gf, qf, kf, vf, gw, gbias, of, st_scr)
    _gla_chunk(1, gb_, qb, kb, vb, gw, gbias, ob, st_scr)


def _gla_scan(p, batch, seq_len, gw, gbias):
    n = p.shape[0]
    nc = seq_len // CHUNK

    def fwd(col, width):
        return pl.BlockSpec((CHUNK, width), lambda b, c: (b * nc + c, col // width))

    def bwd(col, width):
        return pl.BlockSpec((CHUNK, width), lambda b, c: (b * nc + nc - 1 - c, col // width))

    def full(a):
        return pl.BlockSpec(a.shape, lambda b, c: (0,) * a.ndim)

    kw, vw = GLA_K_WIDTH, GLA_V_WIDTH
    in_specs = [fwd(COL_GG, LANES), fwd(COL_GQ, kw), fwd(COL_GK, kw), fwd(COL_GV, vw),
                bwd(COL_GG, LANES), bwd(COL_GQ, kw), bwd(COL_GK, kw), bwd(COL_GV, vw),
                full(gw), full(gbias)]
    ospec_f = pl.BlockSpec((CHUNK, vw), lambda b, c: (b * nc + c, 0))
    ospec_b = pl.BlockSpec((CHUNK, vw), lambda b, c: (b * nc + nc - 1 - c, 0))
    osh = jax.ShapeDtypeStruct((n, vw), F32)
    return pl.pallas_call(
        _gla_kernel,
        grid=(batch, nc),
        in_specs=in_specs,
        out_specs=[ospec_f, ospec_b],
        out_shape=[osh, osh],
        scratch_shapes=[pltpu.VMEM((2, kw // LANES, GLA_DV, LANES), F32)],
        compiler_params=_cparams(("parallel", "arbitrary")),
        name="gla_scan",
    )(p, p, p, p, p, p, p, p, gw, gbias)


def _mix_out_kernel(x_ref, yf, yb, bdf, bdb, v_ref, gl_ref, of, ob, og_ref,
                    g2_ref, lnw_ref, lnb_ref, blk_ref, nw_ref, wo_ref, o_ref):
    blk = blk_ref[...]
    inv = 1.0 / RWKV_HEAD_DIM
    y = yf[...] + yb[...]
    mean = _mm_ones(y, blk) * inv
    yc = y - mean
    var = _mm_ones(yc * yc, blk) * inv
    yn = yc * lax.rsqrt(var + RWKV_GN_EPS) * lnw_ref[...] + lnb_ref[...]
    bonus = (bdf[...] + bdb[...]) * v_ref[...]
    gate = _mm(_sigmoid(gl_ref[...]), g2_ref[...])
    y_rw = (yn + bonus) * gate

    o = of[...] + ob[...]
    og = og_ref[...]
    gsil = og * _sigmoid(og)
    nw = nw_ref[...]
    acc = x_ref[...] + _mm(y_rw, wo_ref[0:RWKV_WIDTH, :])
    for h in range(GLA_HEADS):
        sl = slice(h * GLA_DV, (h + 1) * GLA_DV)
        oh = o[:, sl]
        ms = jnp.mean(oh * oh, axis=-1, keepdims=True)
        yg = oh * lax.rsqrt(ms + RMS_EPS) * nw[:, sl] * gsil[:, sl]
        acc = acc + _mm(yg, wo_ref[RWKV_WIDTH + h * GLA_DV:RWKV_WIDTH + (h + 1) * GLA_DV, :])
    o_ref[...] = acc


def _mix_out(xf, p, yf, yb, bdf, bdb, of, ob, g2, lnw, lnb, blk, nw, wo, tm=256):
    n, d = xf.shape
    w = RWKV_WIDTH

    def rows(width, col=0):
        return pl.BlockSpec((tm, width), lambda i: (i, col // width))

    def full(a):
        return pl.BlockSpec(a.shape, lambda i: (0,) * a.ndim)

    in_specs = [rows(d), rows(w), rows(w), rows(w), rows(w), rows(w, COL_V), rows(LANES, COL_GL),
                rows(w), rows(w), rows(w, COL_OG),
                full(g2), full(lnw), full(lnb), full(blk), full(nw), full(wo)]
    return pl.pallas_call(
        _mix_out_kernel,
        grid=(n // tm,),
        in_specs=in_specs,
        out_specs=rows(d),
        out_shape=jax.ShapeDtypeStruct((n, d), F32),
        compiler_params=_cparams(("parallel",)),
        name="mix_outproj",
    )(xf, yf, yb, bdf, bdb, p, p, of, ob, p, g2, lnw, lnb, blk, nw, wo)


def _router_kernel(x_ref, lnw_ref, wr_ref, br_ref, h_ref, wt_ref):
    h = _rmsnorm_rows(x_ref[...], lnw_ref[...])
    h_ref[...] = h.astype(BF16)
    logits = _mm_x3(wr_ref[...], h, 1, 1) + br_ref[...]
    tm = logits.shape[1]
    coarse = logits[0:N_GROUPS, :]
    fine = logits[8:8 + N_EXPERTS, :]
    rowg = lax.broadcasted_iota(jnp.int32, (N_GROUPS, tm), 0)
    cmax = jnp.max(coarse, axis=0, keepdims=True)
    gsel = jnp.min(jnp.where(coarse == cmax, rowg, N_GROUPS), axis=0, keepdims=True)
    pg = 1.0 / jnp.sum(jnp.exp(coarse - cmax), axis=0, keepdims=True)
    sel = jnp.zeros((EXPERTS_PER_GROUP, tm), F32)
    for g in range(N_GROUPS):
        sel = sel + jnp.where(gsel == g, fine[g * EXPERTS_PER_GROUP:(g + 1) * EXPERTS_PER_GROUP, :], 0.0)
    rowe = lax.broadcasted_iota(jnp.int32, (EXPERTS_PER_GROUP, tm), 0)
    l1 = jnp.max(sel, axis=0, keepdims=True)
    i1 = jnp.min(jnp.where(sel == l1, rowe, EXPERTS_PER_GROUP), axis=0, keepdims=True)
    sel2 = jnp.where(rowe == i1, -jnp.inf, sel)
    l2 = jnp.max(sel2, axis=0, keepdims=True)
    i2 = jnp.min(jnp.where(sel2 == l2, rowe, EXPERTS_PER_GROUP), axis=0, keepdims=True)
    t = jnp.exp(l2 - l1)
    w1 = pg / (1.0 + t)
    w2 = pg * t / (1.0 + t)
    e1 = gsel * EXPERTS_PER_GROUP + i1
    e2 = gsel * EXPERTS_PER_GROUP + i2
    rowx = lax.broadcasted_iota(jnp.int32, (N_EXPERTS, tm), 0)
    wt = jnp.where(rowx == e1, w1, 0.0) + jnp.where(rowx == e2, w2, 0.0)
    wt_ref[...] = jnp.transpose(wt)


def _router(x1, lnw, wr, br, tm=256):
    n, d = x1.shape

    def full(a):
        return pl.BlockSpec(a.shape, lambda i: (0,) * a.ndim)

    return pl.pallas_call(
        _router_kernel,
        grid=(n // tm,),
        in_specs=[pl.BlockSpec((tm, d), lambda i: (i, 0)), full(lnw), full(wr), full(br)],
        out_specs=[pl.BlockSpec((tm, d), lambda i: (i, 0)),
                   pl.BlockSpec((tm, N_EXPERTS), lambda i: (i, 0))],
        out_shape=[jax.ShapeDtypeStruct((n, d), BF16),
                   jax.ShapeDtypeStruct((n, N_EXPERTS), F32)],
        compiler_params=_cparams(("parallel",)),
        name="router",
    )(x1, lnw, wr, br)


def _moe_kernel(h_ref, wt_ref, wg_ref, wu_ref, wd_ref, x_ref, lnf_ref, o_ref, acc, *, final_norm):
    e = pl.program_id(1)

    @pl.when(e == 0)
    def _():
        acc[...] = jnp.zeros(acc.shape, F32)

    h = h_ref[...]
    a = jnp.dot(h, wg_ref[0], preferred_element_type=F32)
    u = jnp.dot(h, wu_ref[0], preferred_element_type=F32)
    wt = wt_ref[...]
    lane = lax.broadcasted_iota(jnp.int32, wt.shape, 1)
    wcol = jnp.sum(jnp.where(lane == e, wt, 0.0), axis=1, keepdims=True)
    hh = (a * _sigmoid(a)) * u * wcol
    acc[...] += jnp.dot(hh.astype(BF16), wd_ref[0], preferred_element_type=F32)

    @pl.when(e == pl.num_programs(1) - 1)
    def _():
        y = x_ref[...] + acc[...]
        o_ref[...] = _rmsnorm_rows(y, lnf_ref[...]) if final_norm else y


def _moe(h2, wt, wg, wu, wd, x1, lnf, final_norm, tm=512):
    n, d = x1.shape
    ne, _, de = wg.shape
    return pl.pallas_call(
        functools.partial(_moe_kernel, final_norm=final_norm),
        grid=(n // tm, ne),
        in_specs=[pl.BlockSpec((tm, d), lambda i, e: (i, 0)),
                  pl.BlockSpec((tm, ne), lambda i, e: (i, 0)),
                  pl.BlockSpec((1, d, de), lambda i, e: (e, 0, 0)),
                  pl.BlockSpec((1, d, de), lambda i, e: (e, 0, 0)),
                  pl.BlockSpec((1, de, d), lambda i, e: (e, 0, 0)),
                  pl.BlockSpec((tm, d), lambda i, e: (i, 0)),
                  pl.BlockSpec((1, d), lambda i, e: (0, 0))],
        out_specs=pl.BlockSpec((tm, d), lambda i, e: (i, 0)),
        out_shape=jax.ShapeDtypeStruct((n, d), F32),
        scratch_shapes=[pltpu.VMEM((tm, d), F32)],
        compiler_params=_cparams(("parallel", "arbitrary")),
        name="moe_experts",
    )(h2, wt, wg, wu, wd, x1, lnf)


def _pad_rows(a, before, total):
    return jnp.pad(a, ((before, total - before - a.shape[0]), (0, 0)))


def _layer(xf, batch, seq_len, ln1_w, w_in, rw_mu, rw_w0_f, rw_w2_f, rw_w0_b, rw_w2_b, rw_a0_f,
           rw_a2_f, rw_a0_b, rw_a2_b, rw_g2, rw_k_k, rw_k_a, rw_r_k, rw_ln_w, rw_ln_b, gla_gw2_f,
           gla_gb_f, gla_gw2_b, gla_gb_b, gla_norm_w, w_out, ln2_w, moe_w_coarse, moe_b_coarse,
           moe_w_fine, moe_b_fine, moe_w_gate, moe_w_up, moe_w_down):
    d = xf.shape[1]
    rw_cols = 3 * RWKV_WIDTH + 4 * LORA + GATE_LORA
    w_rw, w_gla = w_in[:, :rw_cols], w_in[:, rw_cols:]
    o = 0
    w_gq = w_gla[:, o:o + GLA_K_WIDTH]; o += GLA_K_WIDTH
    w_gk = w_gla[:, o:o + GLA_K_WIDTH]; o += GLA_K_WIDTH
    w_gv = w_gla[:, o:o + GLA_V_WIDTH]; o += GLA_V_WIDTH
    w_gg = w_gla[:, o:o + 2 * GLA_GATE_LORA]; o += 2 * GLA_GATE_LORA
    w_og = w_gla[:, o:o + GLA_V_WIDTH]
    w_gg = jnp.pad(w_gg, ((0, 0), (0, LANES - 2 * GLA_GATE_LORA)))
    w_pad = jnp.concatenate([w_rw, w_gg, w_gq, w_gk, w_gv, w_og], axis=1).astype(BF16)
    mu_pad = jnp.pad(rw_mu, (0, NP - rw_cols))[None, :]

    p = _inproj(xf, ln1_w[None, :], w_pad, mu_pad, seq_len)

    head = lax.broadcasted_iota(jnp.int32, (RWKV_WIDTH, RWKV_WIDTH), 0) // RWKV_HEAD_DIM
    blk = (head == head.T).astype(BF16)
    w0 = jnp.stack([rw_w0_f, rw_w0_b])[:, None, :]
    a0 = jnp.stack([rw_a0_f, rw_a0_b])[:, None, :]
    w2 = jnp.stack([_pad_rows(rw_w2_f, 0, LANES), _pad_rows(rw_w2_b, LORA, LANES)]).astype(BF16)
    a2 = jnp.stack([_pad_rows(rw_a2_f, 0, LANES), _pad_rows(rw_a2_b, LORA, LANES)]).astype(BF16)
    yf, yb, bdf, bdb = _rwkv_scan(p, batch, seq_len, w0, w2, a0, a2, rw_k_k[None, :],
                                  rw_k_a[None, :], rw_r_k.reshape(1, -1), blk)

    gw = jnp.stack([_pad_rows(gla_gw2_f, 0, LANES),
                    _pad_rows(gla_gw2_b, GLA_GATE_LORA, LANES)]).astype(BF16)
    gbias = jnp.stack([gla_gb_f, gla_gb_b])[:, None, :]
    of, ob = _gla_scan(p, batch, seq_len, gw, gbias)

    x1 = _mix_out(xf, p, yf, yb, bdf, bdb, of, ob, rw_g2.astype(BF16), rw_ln_w[None, :],
                  rw_ln_b[None, :], blk, gla_norm_w[None, :], w_out.astype(BF16))

    wr = jnp.concatenate([moe_w_coarse.T, jnp.zeros((8 - N_GROUPS, d), F32), moe_w_fine.T], axis=0)
    br = jnp.concatenate([moe_b_coarse, jnp.zeros((8 - N_GROUPS,), F32), moe_b_fine])[:, None]
    h2, wt = _router(x1, ln2_w[None, :], wr, br)
    return h2, wt, x1


def kernel(x, ln1_w, w_in, rw_mu, rw_w0_f, rw_w2_f, rw_w0_b, rw_w2_b, rw_a0_f, rw_a2_f, rw_a0_b, rw_a2_b, rw_g2, rw_k_k, rw_k_a, rw_r_k, rw_ln_w, rw_ln_b, gla_gw2_f, gla_gb_f, gla_gw2_b, gla_gb_b, gla_norm_w, w_out, ln2_w, moe_w_coarse, moe_b_coarse, moe_w_fine, moe_b_fine, moe_w_gate, moe_w_up, moe_w_down, ln_f_w):
    batch, seq_len, d = x.shape
    xf = x.reshape(batch * seq_len, d)
    depth = w_in.shape[0]
    for l in range(depth):
        h2, wt, x1 = _layer(
            xf, batch, seq_len, ln1_w[l], w_in[l], rw_mu[l], rw_w0_f[l], rw_w2_f[l], rw_w0_b[l],
            rw_w2_b[l], rw_a0_f[l], rw_a2_f[l], rw_a0_b[l], rw_a2_b[l], rw_g2[l], rw_k_k[l],
            rw_k_a[l], rw_r_k[l], rw_ln_w[l], rw_ln_b[l], gla_gw2_f[l], gla_gb_f[l], gla_gw2_b[l],
            gla_gb_b[l], gla_norm_w[l], w_out[l], ln2_w[l], moe_w_coarse[l], moe_b_coarse[l],
            moe_w_fine[l], moe_b_fine[l], moe_w_gate[l], moe_w_up[l], moe_w_down[l])
        xf = _moe(h2, wt, moe_w_gate[l].astype(BF16), moe_w_up[l].astype(BF16),
                  moe_w_down[l].astype(BF16), x1, ln_f_w[None, :], l == depth - 1)
    return xf.reshape(batch, seq_len, d)
```

```python
import functools

import jax
import jax.numpy as jnp
from jax import lax
from jax.experimental import pallas as pl
from jax.experimental.pallas import tpu as pltpu

F32 = jnp.float32
BF16 = jnp.bfloat16

RMS_EPS = 1e-6
RWKV_GN_EPS = 64e-5
RWKV_WIDTH = 512
RWKV_HEAD_DIM = 64
LORA = 64
GATE_LORA = 128
GLA_HEADS = 4
GLA_DK = 64
GLA_DV = 128
GLA_K_WIDTH = GLA_HEADS * GLA_DK
GLA_V_WIDTH = GLA_HEADS * GLA_DV
GLA_GATE_LORA = 16
GLA_TAU = 16.0
N_GROUPS = 4
EXPERTS_PER_GROUP = 8
N_EXPERTS = N_GROUPS * EXPERTS_PER_GROUP

CHUNK = 64
LANES = 128

COL_R, COL_K, COL_V = 0, 512, 1024
COL_LORA = 1536
COL_GL = 1792
COL_GG = 1920
COL_GQ, COL_GK, COL_GV, COL_OG = 2048, 2304, 2560, 3072
NP = 3584

VMEM_LIMIT = 56 * 1024 * 1024


def _cparams(sem):
    return pltpu.CompilerParams(dimension_semantics=sem, vmem_limit_bytes=VMEM_LIMIT)


MM_DTYPE = BF16


def _mm(a, b, ca=1, cb=0):
    dt = MM_DTYPE
    return lax.dot_general(a.astype(dt), b.astype(dt), (((ca,), (cb,)), ((), ())),
                           preferred_element_type=F32)


def _split2(x):
    hi = x.astype(BF16)
    lo = (x - hi.astype(F32)).astype(BF16)
    return hi, lo


def _split3(x):
    hi = x.astype(BF16)
    r1 = x - hi.astype(F32)
    mid = r1.astype(BF16)
    lo = (r1 - mid.astype(F32)).astype(BF16)
    return hi, mid, lo


def _mm_ones(x, ones_bf16):
    hi, mid, lo = _split3(x)
    return (_mm(lo, ones_bf16) + _mm(mid, ones_bf16)) + _mm(hi, ones_bf16)


def _mm_x3(a, b, ca=1, cb=0):
    ah, al = _split2(a)
    bh, bl = _split2(b)
    return (_mm(al, bh, ca, cb) + _mm(ah, bl, ca, cb)) + _mm(ah, bh, ca, cb)


def _rmsnorm_rows(x, w):
    ms = jnp.mean(x * x, axis=-1, keepdims=True)
    return x * lax.rsqrt(ms + RMS_EPS) * w


def _softplus(x):
    return jnp.maximum(x, 0.0) + jnp.log(1.0 + jnp.exp(-jnp.abs(x)))


def _sigmoid(x):
    return 1.0 / (1.0 + jnp.exp(-x))


def _seg_cumsum(x, seg, rev):
    n = x.shape[0]
    pos = lax.broadcasted_iota(jnp.int32, x.shape, 0) & (seg - 1)
    sh = 1
    while sh < seg:
        if not rev:
            x = x + jnp.where(pos >= sh, pltpu.roll(x, sh, 0), 0.0)
        else:
            x = x + jnp.where(pos < seg - sh, pltpu.roll(x, n - sh, 0), 0.0)
        sh *= 2
    return x


def _blockdiag2(y, m0):
    return jnp.concatenate([jnp.where(m0, y, 0.0), jnp.where(m0, 0.0, y)], axis=0)


def _inproj_kernel(x_ref, xp_ref, xn_ref, lnw_ref, w_ref, mu_ref, o_ref, *, tiles_per_seq,
                   shift_cols, tn):
    i = pl.program_id(0)
    lnw = lnw_ref[...]
    h = _rmsnorm_rows(x_ref[...], lnw).astype(BF16)
    t = i % tiles_per_seq
    hp = jnp.where(t == 0, 0.0, _rmsnorm_rows(xp_ref[...], lnw))
    hn = jnp.where(t == tiles_per_seq - 1, 0.0, _rmsnorm_rows(xn_ref[...], lnw))
    hh = jnp.concatenate([hp, hn], axis=0).astype(BF16)
    tm = h.shape[0]
    npad = w_ref.shape[1]
    bounds = [(c0, min(c0 + tn, shift_cols)) for c0 in range(0, shift_cols, tn)]
    bounds += [(c0, min(c0 + tn, npad)) for c0 in range(shift_cols, npad, tn)]
    for c0, c1 in bounds:
        w = w_ref[:, c0:c1]
        p = jnp.dot(h, w, preferred_element_type=F32)
        if c0 < shift_cols:
            ph = jnp.dot(hh, w, preferred_element_type=F32)
            row = lax.broadcasted_iota(jnp.int32, p.shape, 0)
            prev = jnp.where(row == 0, ph[7:8, :], pltpu.roll(p, 1, 0))
            nxt = jnp.where(row == tm - 1, ph[8:9, :], pltpu.roll(p, tm - 1, 0))
            p = p + mu_ref[:, c0:c1] * (0.5 * (prev + nxt) - p)
        o_ref[:, c0:c1] = p


def _inproj(xf, lnw, w_pad, mu_pad, seq_len, shift_cols, tm=512, tn=640):
    n, d = xf.shape
    npad = w_pad.shape[1]
    assert seq_len % tm == 0 and shift_cols % LANES == 0 and npad % LANES == 0 and tn % LANES == 0
    tps = seq_len // tm
    nb8 = n // 8
    return pl.pallas_call(
        functools.partial(_inproj_kernel, tiles_per_seq=tps, shift_cols=shift_cols, tn=tn),
        grid=(n // tm,),
        in_specs=[
            pl.BlockSpec((tm, d), lambda i: (i, 0)),
            pl.BlockSpec((8, d), lambda i: (jnp.maximum(i * (tm // 8) - 1, 0), 0)),
            pl.BlockSpec((8, d), lambda i: (jnp.minimum((i + 1) * (tm // 8), nb8 - 1), 0)),
            pl.BlockSpec((1, d), lambda i: (0, 0)),
            pl.BlockSpec((d, npad), lambda i: (0, 0)),
            pl.BlockSpec((1, npad), lambda i: (0, 0)),
        ],
        out_specs=pl.BlockSpec((tm, npad), lambda i: (i, 0)),
        out_shape=jax.ShapeDtypeStruct((n, npad), F32),
        compiler_params=_cparams(("parallel",)),
        name="inproj_shift",
    )(xf, xf, xf, lnw, w_pad, mu_pad)


def _rwkv_prep(d, r_ref, k_ref, v_ref, l_ref, w0_ref, w2_ref, a0_ref, a2_ref,
               kk_ref, ka_ref, rk_ref, blk_ref, bd_ref):
    rev = d == 1
    c = CHUNK
    r = r_ref[...]
    k = k_ref[...]
    lo = l_ref[...]
    blk = blk_ref[...]

    zw = w0_ref[d] + _mm(jnp.tanh(lo[:, 0:LANES]), w2_ref[d])
    lw = -jnp.exp(-_softplus(-zw) - 0.5)
    a = _sigmoid(a0_ref[d] + _mm(lo[:, LANES:2 * LANES], a2_ref[d]))
    kkr = k * kk_ref[...]
    ss = _mm_ones(kkr * kkr, blk)
    kk = kkr / jnp.maximum(jnp.sqrt(ss), 1e-12)
    kd = k * (1.0 + (a - 1.0) * ka_ref[...])
    b = kk * a
    bd_ref[...] = _mm_ones(r * kd * rk_ref[...], blk)

    cum = _seg_cumsum(lw, c, rev)
    tot = cum[0:1, :] if rev else cum[c - 1:c, :]
    ginv = jnp.exp(-cum)
    gend = jnp.exp(tot - cum)
    return dict(at=-kk * jnp.exp(cum - lw), rt=r * jnp.exp(cum), bt=b * ginv, kt=kd * ginv,
                bh=b * gend, kh=kd * gend, gc=jnp.exp(tot), v=v_ref[...])


def _rwkv_kernel(rf, kf, vf, lf, rb, kb, vb, lb, w0, w2, a0, a2, kk, ka, rk, blk,
                 yf, yb, bdf, bdb, m_scr):
    @pl.when(pl.program_id(1) == 0)
    def _():
        m_scr[...] = jnp.zeros(m_scr.shape, F32)

    c = CHUNK
    prep = [_rwkv_prep(0, rf, kf, vf, lf, w0, w2, a0, a2, kk, ka, rk, blk, bdf),
            _rwkv_prep(1, rb, kb, vb, lb, w0, w2, a0, a2, kk, ka, rk, blk, bdb)]
    y_refs = [yf, yb]

    row = lax.broadcasted_iota(jnp.int32, (c, LANES), 0)
    lane = lax.broadcasted_iota(jnp.int32, (c, LANES), 1)
    s_idx = lane & (RWKV_HEAD_DIM - 1)
    m0 = lane < RWKV_HEAD_DIM
    strict = [s_idx < row, s_idx > row]
    incl = [s_idx <= row, s_idx >= row]
    diag = s_idx == row

    def bd2(y):
        return _blockdiag2(y, m0)

    def fold(z):
        return jnp.where(m0, z[:c], 0.0) + jnp.where(m0, 0.0, z[c:])

    chains = [(d, p) for p in range(RWKV_WIDTH // LANES) for d in (0, 1)]
    ds = [d for d, _ in chains]

    def get(name):
        return [prep[d][name][:, p * LANES:(p + 1) * LANES] for d, p in chains]

    at, rt, bt, kt, bh, kh, vp, gc = (get(n) for n in ("at", "rt", "bt", "kt", "bh", "kh", "v", "gc"))
    x2 = [jnp.concatenate([a_, r_], axis=0) for a_, r_ in zip(at, rt)]
    sb = [_mm(x, bd2(y), 1, 1) for x, y in zip(x2, bt)]
    sk = [_mm(x, bd2(y), 1, 1) for x, y in zip(x2, kt)]
    lab = [jnp.where(strict[d], z[:c], 0.0) for d, z in zip(ds, sb)]
    prb = [jnp.where(incl[d], z[c:], 0.0) for d, z in zip(ds, sb)]
    lak = [jnp.where(strict[d], z[:c], 0.0) for d, z in zip(ds, sk)]
    prk = [jnp.where(incl[d], z[c:], 0.0) for d, z in zip(ds, sk)]
    bdv = [bd2(y) for y in vp]
    xa = at
    xv = [_mm(x, y) for x, y in zip(lak, bdv)]
    lp = lab
    n_it = 6
    for it in range(n_it):
        upd = [_mm(l_, jnp.concatenate([bd2(a_), bd2(v_)], axis=1)) for l_, a_, v_ in zip(lp, xa, xv)]
        xa = [a_ + u[:, :LANES] for a_, u in zip(xa, upd)]
        xv = [v_ + u[:, LANES:] for v_, u in zip(xv, upd)]
        if it < n_it - 1:
            lp = [_mm(l_, bd2(l_)) for l_ in lp]
    px = [_mm(x, jnp.concatenate([bd2(a_), bd2(v_)], axis=1)) for x, a_, v_ in zip(prb, xa, xv)]
    pv = [_mm(x, y) for x, y in zip(prk, bdv)]
    zb = [_mm(x, jnp.concatenate([a_, v_], axis=1), 0, 0) for x, a_, v_ in zip(bh, xa, xv)]
    zk = [_mm(x, y, 0, 0) for x, y in zip(kh, vp)]
    for i, (d, p) in enumerate(chains):
        qp = rt[i] + px[i][:, :LANES]
        yl = px[i][:, LANES:] + pv[i]
        g = jnp.where(diag, gc[i], 0.0) + fold(zb[i][:, :LANES])
        h = fold(zb[i][:, LANES:]) + fold(zk[i])
        bm = bd2(m_scr[d, p])
        yg = _mm(jnp.concatenate([qp, g], axis=0), bm)
        y_refs[d][:, p * LANES:(p + 1) * LANES] = yg[:c] + yl
        m_scr[d, p] = yg[c:] + h


def _rwkv_scan(p, batch, seq_len, w0, w2, a0, a2, kk, ka, rk, blk):
    n = p.shape[0]
    nc = seq_len // CHUNK
    w = RWKV_WIDTH

    def fwd(col, width):
        return pl.BlockSpec((CHUNK, width), lambda b, c: (b * nc + c, col // width))

    def bwd(col, width):
        return pl.BlockSpec((CHUNK, width), lambda b, c: (b * nc + nc - 1 - c, col // width))

    def full(a):
        return pl.BlockSpec(a.shape, lambda b, c: (0,) * a.ndim)

    lora_w = 2 * LANES
    in_specs = [fwd(COL_R, w), fwd(COL_K, w), fwd(COL_V, w), fwd(COL_LORA, lora_w),
                bwd(COL_R, w), bwd(COL_K, w), bwd(COL_V, w), bwd(COL_LORA, lora_w),
                full(w0), full(w2), full(a0), full(a2), full(kk), full(ka), full(rk), full(blk)]
    ospec_f = pl.BlockSpec((CHUNK, w), lambda b, c: (b * nc + c, 0))
    ospec_b = pl.BlockSpec((CHUNK, w), lambda b, c: (b * nc + nc - 1 - c, 0))
    osh = jax.ShapeDtypeStruct((n, w), F32)
    return pl.pallas_call(
        _rwkv_kernel,
        grid=(batch, nc),
        in_specs=in_specs,
        out_specs=[ospec_f, ospec_b, ospec_f, ospec_b],
        out_shape=[osh, osh, osh, osh],
        scratch_shapes=[pltpu.VMEM((2, w // LANES, CHUNK, LANES), F32)],
        compiler_params=_cparams(("parallel", "arbitrary")),
        name="rwkv7_scan",
    )(p, p, p, p, p, p, p, p, w0, w2, a0, a2, kk, ka, rk, blk)


def _gla_chunk(d, g_ref, q_ref, k_ref, v_ref, gw_ref, gb_ref, o_ref, st_scr):
    rev = d == 1
    c = CHUNK
    q = q_ref[...] * (GLA_DK ** -0.5)
    k = k_ref[...]
    v = v_ref[...]
    xg = _mm(g_ref[...], gw_ref[d]) + gb_ref[d]
    g = -_softplus(-xg) / GLA_TAU

    row = lax.broadcasted_iota(jnp.int32, (c, LANES), 0)
    lane = lax.broadcasted_iota(jnp.int32, (c, LANES), 1)
    s_idx = lane & (GLA_DK - 1)
    m0 = lane < GLA_DK
    prog = (c - 1 - row) if rev else row
    wide_row = lax.broadcasted_iota(jnp.int32, g.shape, 0)
    wide_prog = (c - 1 - wide_row) if rev else wide_row

    levels = []
    s = 1
    while s < c:
        if s == 1:
            cs_p = g
            cs_o = g
        else:
            cs_p = _seg_cumsum(g, s, rev)
            cs_o = _seg_cumsum(g, s, not rev)
        upper = (wide_prog & (2 * s - 1)) >= s
        fq = jnp.where(upper, jnp.exp(cs_p), 0.0)
        fk = jnp.where(upper, 0.0, jnp.exp(cs_o - g))
        same = (row // (2 * s)) == (s_idx // (2 * s))
        levels.append((q * fq, k * fk, same))
        s *= 2
    bc = _seg_cumsum(g, c, rev)
    bo = _seg_cumsum(g, c, not rev)
    tot = bc[0:1, :] if rev else bc[c - 1:c, :]
    qt_all = q * jnp.exp(bc)
    kh_all = k * jnp.exp(bo - g)
    gc_all = jnp.exp(tot)
    diag = row == s_idx
    del prog

    for p in range(GLA_K_WIDTH // LANES):
        sl = slice(p * LANES, (p + 1) * LANES)
        a = jnp.where(diag, _mm(q[:, sl], _blockdiag2(k[:, sl], m0), 1, 1), 0.0)
        for ql, kl, same in levels:
            a = a + jnp.where(same, _mm(ql[:, sl], _blockdiag2(kl[:, sl], m0), 1, 1), 0.0)
        st = st_scr[d, p]
        v0 = v[:, (2 * p) * GLA_DV:(2 * p + 1) * GLA_DV]
        v1 = v[:, (2 * p + 1) * GLA_DV:(2 * p + 2) * GLA_DV]
        o = (_mm(_blockdiag2(qt_all[:, sl], m0), st, 1, 1)
             + _mm(_blockdiag2(a, m0), jnp.concatenate([v0, v1], axis=0)))
        o_ref[:, (2 * p) * GLA_DV:(2 * p + 1) * GLA_DV] = o[:c]
        o_ref[:, (2 * p + 1) * GLA_DV:(2 * p + 2) * GLA_DV] = o[c:]
        z = _mm(jnp.concatenate([v0, v1], axis=1), kh_all[:, sl], 0, 0)
        lane_v = lax.broadcasted_iota(jnp.int32, (GLA_DV, LANES), 1)
        mv = lane_v < GLA_DK
        st_scr[d, p] = (st * gc_all[:, sl] + jnp.where(mv, z[:GLA_DV], 0.0)
                        + jnp.where(mv, 0.0, z[GLA_DV:]))


def _gla_kernel(gf, qf, kf, vf, gb_, qb, kb, vb, gw, gbias, of, ob, st_scr):
    @pl.when(pl.program_id(1) == 0)
    def _():
        st_scr[...] = jnp.zeros(st_scr.shape, F32)

    _gla_chunk(0, gf, qf, kf, vf, gw, gbias, of, st_scr)
    _gla_chunk(1, gb_, qb, kb, vb, gw, gbias, ob, st_scr)


def _gla_scan(p, batch, seq_len, gw, gbias):
    n = p.shape[0]
    nc = seq_len // CHUNK

    def fwd(col, width):
        return pl.BlockSpec((CHUNK, width), lambda b, c: (b * nc + c, col // width))

    def bwd(col, width):
        return pl.BlockSpec((CHUNK, width), lambda b, c: (b * nc + nc - 1 - c, col // width))

    def full(a):
        return pl.BlockSpec(a.shape, lambda b, c: (0,) * a.ndim)

    kw, vw = GLA_K_WIDTH, GLA_V_WIDTH
    in_specs = [fwd(COL_GG, LANES), fwd(COL_GQ, kw), fwd(COL_GK, kw), fwd(COL_GV, vw),
                bwd(COL_GG, LANES), bwd(COL_GQ, kw), bwd(COL_GK, kw), bwd(COL_GV, vw),
                full(gw), full(gbias)]
    ospec_f = pl.BlockSpec((CHUNK, vw), lambda b, c: (b * nc + c, 0))
    ospec_b = pl.BlockSpec((CHUNK, vw), lambda b, c: (b * nc + nc - 1 - c, 0))
    osh = jax.ShapeDtypeStruct((n, vw), F32)
    return pl.pallas_call(
        _gla_kernel,
        grid=(batch, nc),
        in_specs=in_specs,
        out_specs=[ospec_f, ospec_b],
        out_shape=[osh, osh],
        scratch_shapes=[pltpu.VMEM((2, kw // LANES, GLA_DV, LANES), F32)],
        compiler_params=_cparams(("parallel", "arbitrary")),
        name="gla_scan",
    )(p, p, p, p, p, p, p, p, gw, gbias)


def _mix_out_kernel(x_ref, yf, yb, bdf, bdb, v_ref, gl_ref, of, ob, og_ref,
                    g2_ref, lnw_ref, lnb_ref, blk_ref, nw_ref, wo_ref, o_ref):
    blk = blk_ref[...]
    inv = 1.0 / RWKV_HEAD_DIM
    y = yf[...] + yb[...]
    mean = _mm_ones(y, blk) * inv
    yc = y - mean
    var = _mm_ones(yc * yc, blk) * inv
    yn = yc * lax.rsqrt(var + RWKV_GN_EPS) * lnw_ref[...] + lnb_ref[...]
    bonus = (bdf[...] + bdb[...]) * v_ref[...]
    gate = _mm(_sigmoid(gl_ref[...]), g2_ref[...])
    y_rw = (yn + bonus) * gate

    o = of[...] + ob[...]
    og = og_ref[...]
    gsil = og * _sigmoid(og)
    nw = nw_ref[...]
    acc = x_ref[...] + _mm(y_rw, wo_ref[0:RWKV_WIDTH, :])
    for h in range(GLA_HEADS):
        sl = slice(h * GLA_DV, (h + 1) * GLA_DV)
        oh = o[:, sl]
        ms = jnp.mean(oh * oh, axis=-1, keepdims=True)
        yg = oh * lax.rsqrt(ms + RMS_EPS) * nw[:, sl] * gsil[:, sl]
        acc = acc + _mm(yg, wo_ref[RWKV_WIDTH + h * GLA_DV:RWKV_WIDTH + (h + 1) * GLA_DV, :])
    o_ref[...] = acc


def _mix_out(xf, p, yf, yb, bdf, bdb, of, ob, g2, lnw, lnb, blk, nw, wo, tm=256):
    n, d = xf.shape
    w = RWKV_WIDTH

    def rows(width, col=0):
        return pl.BlockSpec((tm, width), lambda i: (i, col // width))

    def full(a):
        return pl.BlockSpec(a.shape, lambda i: (0,) * a.ndim)

    in_specs = [rows(d), rows(w), rows(w), rows(w), rows(w), rows(w, COL_V), rows(LANES, COL_GL),
                rows(w), rows(w), rows(w, COL_OG),
                full(g2), full(lnw), full(lnb), full(blk), full(nw), full(wo)]
    return pl.pallas_call(
        _mix_out_kernel,
        grid=(n // tm,),
        in_specs=in_specs,
        out_specs=rows(d),
        out_shape=jax.ShapeDtypeStruct((n, d), F32),
        compiler_params=_cparams(("parallel",)),
        name="mix_outproj",
    )(xf, yf, yb, bdf, bdb, p, p, of, ob, p, g2, lnw, lnb, blk, nw, wo)


def _router_kernel(x_ref, lnw_ref, wr_ref, br_ref, h_ref, idx_ref, wc_ref, cnt_ref, base):
    @pl.when(pl.program_id(0) == 0)
    def _():
        base[...] = jnp.zeros(base.shape, F32)

    h = _rmsnorm_rows(x_ref[...], lnw_ref[...])
    h_ref[...] = h
    logits = _mm_x3(wr_ref[...], h, 1, 1) + br_ref[...]
    tm = logits.shape[1]
    coarse = logits[0:N_GROUPS, :]
    fine = logits[8:8 + N_EXPERTS, :]
    rowg = lax.broadcasted_iota(jnp.int32, (N_GROUPS, tm), 0)
    cmax = jnp.max(coarse, axis=0, keepdims=True)
    gsel = jnp.min(jnp.where(coarse == cmax, rowg, N_GROUPS), axis=0, keepdims=True)
    pg = 1.0 / jnp.sum(jnp.exp(coarse - cmax), axis=0, keepdims=True)
    sel = jnp.zeros((EXPERTS_PER_GROUP, tm), F32)
    for g in range(N_GROUPS):
        sel = sel + jnp.where(gsel == g, fine[g * EXPERTS_PER_GROUP:(g + 1) * EXPERTS_PER_GROUP, :], 0.0)
    rowe = lax.broadcasted_iota(jnp.int32, (EXPERTS_PER_GROUP, tm), 0)
    l1 = jnp.max(sel, axis=0, keepdims=True)
    i1 = jnp.min(jnp.where(sel == l1, rowe, EXPERTS_PER_GROUP), axis=0, keepdims=True)
    sel2 = jnp.where(rowe == i1, -jnp.inf, sel)
    l2 = jnp.max(sel2, axis=0, keepdims=True)
    i2 = jnp.min(jnp.where(sel2 == l2, rowe, EXPERTS_PER_GROUP), axis=0, keepdims=True)
    t = jnp.exp(l2 - l1)
    w1 = pg / (1.0 + t)
    w2 = pg * t / (1.0 + t)
    e1 = gsel * EXPERTS_PER_GROUP + i1
    e2 = gsel * EXPERTS_PER_GROUP + i2
    rowx = lax.broadcasted_iota(jnp.int32, (N_EXPERTS, tm), 0)
    hit1 = rowx == e1
    hit2 = rowx == e2
    oh = jnp.where(hit1 | hit2, 1.0, 0.0)
    tr = lax.broadcasted_iota(jnp.int32, (tm, tm), 0)
    tc = lax.broadcasted_iota(jnp.int32, (tm, tm), 1)
    before = jnp.where(tr < tc, 1.0, 0.0).astype(BF16)
    pos = base[:, 0:1] + _mm(oh, before)
    r1 = jnp.sum(jnp.where(hit1, pos, 0.0), axis=0, keepdims=True)
    r2 = jnp.sum(jnp.where(hit2, pos, 0.0), axis=0, keepdims=True)
    new_base = base[...] + jnp.sum(oh, axis=1, keepdims=True)
    base[...] = new_base
    cnt_ref[...] = new_base
    zi = jnp.zeros((4, tm), jnp.int32)
    idx_ref[...] = jnp.concatenate([e1, e2, r1.astype(jnp.int32), r2.astype(jnp.int32), zi], axis=0)
    zf = jnp.zeros((6, tm), F32)
    wc_ref[...] = jnp.transpose(jnp.concatenate([w1, w2, zf], axis=0))


def _router(x1, lnw, wr, br, tm=256):
    n, d = x1.shape

    def full(a):
        return pl.BlockSpec(a.shape, lambda i: (0,) * a.ndim)

    return pl.pallas_call(
        _router_kernel,
        grid=(n // tm,),
        in_specs=[pl.BlockSpec((tm, d), lambda i: (i, 0)), full(lnw), full(wr), full(br)],
        out_specs=[pl.BlockSpec((tm, d), lambda i: (i, 0)),
                   pl.BlockSpec((8, tm), lambda i: (0, i)),
                   pl.BlockSpec((tm, 8), lambda i: (i, 0)),
                   pl.BlockSpec((N_EXPERTS, LANES), lambda i: (0, 0))],
        out_shape=[jax.ShapeDtypeStruct((n, d), F32),
                   jax.ShapeDtypeStruct((8, n), jnp.int32),
                   jax.ShapeDtypeStruct((n, 8), F32),
                   jax.ShapeDtypeStruct((N_EXPERTS, LANES), F32)],
        scratch_shapes=[pltpu.VMEM((N_EXPERTS, LANES), F32)],
        compiler_params=_cparams(("arbitrary",)),
        name="router",
    )(x1, lnw, wr, br)


MOE_ROWS = 256


def _slot(pstart_ref, idx_ref, k, t):
    return pstart_ref[idx_ref[k, t]] + idx_ref[2 + k, t]


def _dispatch_kernel(pstart_ref, idx_ref, h_hbm, xs_in, xs_out, sem):
    del xs_in
    tm = idx_ref.shape[1]
    base = pl.program_id(0) * tm

    def row_copy(t, k):
        return pltpu.make_async_copy(h_hbm.at[pl.ds(base + t, 1)],
                                     xs_out.at[pl.ds(_slot(pstart_ref, idx_ref, k, t), 1)], sem)

    def start(t, carry):
        row_copy(t, 0).start()
        row_copy(t, 1).start()
        return carry

    def wait(t, carry):
        row_copy(t, 0).wait()
        row_copy(t, 1).wait()
        return carry

    lax.fori_loop(0, tm, start, 0)
    lax.fori_loop(0, tm, wait, 0)


def _dispatch(pstart, idx, h2, n_rows, tm=256):
    n, d = h2.shape
    xs0 = jnp.zeros((n_rows, d), F32)
    return pl.pallas_call(
        _dispatch_kernel,
        grid_spec=pltpu.PrefetchScalarGridSpec(
            num_scalar_prefetch=1,
            grid=(n // tm,),
            in_specs=[pl.BlockSpec((8, tm), lambda i, ps: (0, i), memory_space=pltpu.SMEM),
                      pl.BlockSpec(memory_space=pl.ANY),
                      pl.BlockSpec(memory_space=pl.ANY)],
            out_specs=pl.BlockSpec(memory_space=pl.ANY),
            scratch_shapes=[pltpu.SemaphoreType.DMA(())]),
        out_shape=jax.ShapeDtypeStruct((n_rows, d), F32),
        input_output_aliases={3: 0},
        compiler_params=_cparams(("arbitrary",)),
        name="moe_dispatch",
    )(pstart, idx, h2, xs0)


def _expert_kernel(be_ref, nu_ref, x_ref, wg_ref, wu_ref, wd_ref, o_ref):
    i = pl.program_id(0)

    @pl.when(i < nu_ref[0])
    def _():
        x = x_ref[...].astype(BF16)
        a = jnp.dot(x, wg_ref[0], preferred_element_type=F32)
        u = jnp.dot(x, wu_ref[0], preferred_element_type=F32)
        hh = (a * _sigmoid(a)) * u
        o_ref[...] = jnp.dot(hh.astype(BF16), wd_ref[0], preferred_element_type=F32)

    @pl.when(i >= nu_ref[0])
    def _():
        o_ref[...] = jnp.zeros(o_ref.shape, F32)


def _experts(block_e, n_used, xs, wg, wu, wd):
    n_rows, d = xs.shape
    _, _, de = wg.shape
    return pl.pallas_call(
        _expert_kernel,
        grid_spec=pltpu.PrefetchScalarGridSpec(
            num_scalar_prefetch=2,
            grid=(n_rows // MOE_ROWS,),
            in_specs=[pl.BlockSpec((MOE_ROWS, d), lambda i, be, nu: (i, 0)),
                      pl.BlockSpec((1, d, de), lambda i, be, nu: (be[i], 0, 0)),
                      pl.BlockSpec((1, d, de), lambda i, be, nu: (be[i], 0, 0)),
                      pl.BlockSpec((1, de, d), lambda i, be, nu: (be[i], 0, 0))],
            out_specs=pl.BlockSpec((MOE_ROWS, d), lambda i, be, nu: (i, 0))),
        out_shape=jax.ShapeDtypeStruct((n_rows, d), F32),
        compiler_params=_cparams(("arbitrary",)),
        name="moe_experts",
    )(block_e, n_used, xs, wg, wu, wd)


def _combine_kernel(pstart_ref, idx_ref, ys_hbm, x_ref, w_ref, lnf_ref, o_ref, y1, y2, sem, *,
                    final_norm):
    tm = idx_ref.shape[1]

    def row_copy(t, k):
        buf = y1 if k == 0 else y2
        return pltpu.make_async_copy(ys_hbm.at[pl.ds(_slot(pstart_ref, idx_ref, k, t), 1)],
                                     buf.at[pl.ds(t, 1)], sem)

    def start(t, carry):
        row_copy(t, 0).start()
        row_copy(t, 1).start()
        return carry

    def wait(t, carry):
        row_copy(t, 0).wait()
        row_copy(t, 1).wait()
        return carry

    lax.fori_loop(0, tm, start, 0)
    lax.fori_loop(0, tm, wait, 0)
    w = w_ref[...]
    y = x_ref[...] + w[:, 0:1] * y1[...] + w[:, 1:2] * y2[...]
    o_ref[...] = _rmsnorm_rows(y, lnf_ref[...]) if final_norm else y


def _combine(pstart, idx, ys, x1, wc, lnf, final_norm, tm=256):
    n, d = x1.shape
    return pl.pallas_call(
        functools.partial(_combine_kernel, final_norm=final_norm),
        grid_spec=pltpu.PrefetchScalarGridSpec(
            num_scalar_prefetch=1,
            grid=(n // tm,),
            in_specs=[pl.BlockSpec((8, tm), lambda i, ps: (0, i), memory_space=pltpu.SMEM),
                      pl.BlockSpec(memory_space=pl.ANY),
                      pl.BlockSpec((tm, d), lambda i, ps: (i, 0)),
                      pl.BlockSpec((tm, 8), lambda i, ps: (i, 0)),
                      pl.BlockSpec((1, d), lambda i, ps: (0, 0))],
            out_specs=pl.BlockSpec((tm, d), lambda i, ps: (i, 0)),
            scratch_shapes=[pltpu.VMEM((tm, d), F32), pltpu.VMEM((tm, d), F32),
                            pltpu.SemaphoreType.DMA(())]),
        out_shape=jax.ShapeDtypeStruct((n, d), F32),
        compiler_params=_cparams(("arbitrary",)),
        name="moe_combine",
    )(pstart, idx, ys, x1, wc, lnf)


def _moe(h2, idx, wc, counts, wg, wu, wd, x1, lnf, final_norm):
    n = h2.shape[0]
    n_rows = 2 * n + N_EXPERTS * MOE_ROWS
    cnt = counts[:, 0].astype(jnp.int32)
    padded = ((cnt + MOE_ROWS - 1) // MOE_ROWS) * MOE_ROWS
    pend = jnp.cumsum(padded)
    pstart = pend - padded
    blk_first = jnp.arange(n_rows // MOE_ROWS, dtype=jnp.int32) * MOE_ROWS
    block_e = jnp.minimum(jnp.searchsorted(pend, blk_first, side="right"), N_EXPERTS - 1)
    n_used = (pend[-1:] // MOE_ROWS).astype(jnp.int32)
    xs = _dispatch(pstart, idx, h2, n_rows)
    ys = _experts(block_e.astype(jnp.int32), n_used, xs, wg, wu, wd)
    return _combine(pstart, idx, ys, x1, wc, lnf, final_norm)


def _pad_rows(a, before, total):
    return jnp.pad(a, ((before, total - before - a.shape[0]), (0, 0)))


def _layer(xf, batch, seq_len, ln1_w, w_in, rw_mu, rw_w0_f, rw_w2_f, rw_w0_b, rw_w2_b, rw_a0_f,
           rw_a2_f, rw_a0_b, rw_a2_b, rw_g2, rw_k_k, rw_k_a, rw_r_k, rw_ln_w, rw_ln_b, gla_gw2_f,
           gla_gb_f, gla_gw2_b, gla_gb_b, gla_norm_w, w_out, ln2_w, moe_w_coarse, moe_b_coarse,
           moe_w_fine, moe_b_fine, moe_w_gate, moe_w_up, moe_w_down):
    d = xf.shape[1]
    rw_cols = 3 * RWKV_WIDTH + 4 * LORA + GATE_LORA
    w_rw, w_gla = w_in[:, :rw_cols], w_in[:, rw_cols:]
    o = 0
    w_gq = w_gla[:, o:o + GLA_K_WIDTH]; o += GLA_K_WIDTH
    w_gk = w_gla[:, o:o + GLA_K_WIDTH]; o += GLA_K_WIDTH
    w_gv = w_gla[:, o:o + GLA_V_WIDTH]; o += GLA_V_WIDTH
    w_gg = w_gla[:, o:o + 2 * GLA_GATE_LORA]; o += 2 * GLA_GATE_LORA
    w_og = w_gla[:, o:o + GLA_V_WIDTH]
    w_gg = jnp.pad(w_gg, ((0, 0), (0, LANES - 2 * GLA_GATE_LORA)))
    w_pad = jnp.concatenate([w_rw, w_gg, w_gq, w_gk, w_gv, w_og], axis=1).astype(BF16)
    mu_pad = jnp.pad(rw_mu, (0, NP - rw_cols))[None, :]

    p = _inproj(xf, ln1_w[None, :], w_pad, mu_pad, seq_len, rw_cols)

    head = lax.broadcasted_iota(jnp.int32, (RWKV_WIDTH, RWKV_WIDTH), 0) // RWKV_HEAD_DIM
    blk = (head == head.T).astype(BF16)
    w0 = jnp.stack([rw_w0_f, rw_w0_b])[:, None, :]
    a0 = jnp.stack([rw_a0_f, rw_a0_b])[:, None, :]
    w2 = jnp.stack([_pad_rows(rw_w2_f, 0, LANES), _pad_rows(rw_w2_b, LORA, LANES)]).astype(BF16)
    a2 = jnp.stack([_pad_rows(rw_a2_f, 0, LANES), _pad_rows(rw_a2_b, LORA, LANES)]).astype(BF16)
    yf, yb, bdf, bdb = _rwkv_scan(p, batch, seq_len, w0, w2, a0, a2, rw_k_k[None, :],
                                  rw_k_a[None, :], rw_r_k.reshape(1, -1), blk)

    gw = jnp.stack([_pad_rows(gla_gw2_f, 0, LANES),
                    _pad_rows(gla_gw2_b, GLA_GATE_LORA, LANES)]).astype(BF16)
    gbias = jnp.stack([gla_gb_f, gla_gb_b])[:, None, :]
    of, ob = _gla_scan(p, batch, seq_len, gw, gbias)

    x1 = _mix_out(xf, p, yf, yb, bdf, bdb, of, ob, rw_g2.astype(BF16), rw_ln_w[None, :],
                  rw_ln_b[None, :], blk, gla_norm_w[None, :], w_out.astype(BF16))

    wr = jnp.concatenate([moe_w_coarse.T, jnp.zeros((8 - N_GROUPS, d), F32), moe_w_fine.T], axis=0)
    br = jnp.concatenate([moe_b_coarse, jnp.zeros((8 - N_GROUPS,), F32), moe_b_fine])[:, None]
    h2, idx, wc, counts = _router(x1, ln2_w[None, :], wr, br)
    return h2, idx, wc, counts, x1


def kernel(x, ln1_w, w_in, rw_mu, rw_w0_f, rw_w2_f, rw_w0_b, rw_w2_b, rw_a0_f, rw_a2_f, rw_a0_b, rw_a2_b, rw_g2, rw_k_k, rw_k_a, rw_r_k, rw_ln_w, rw_ln_b, gla_gw2_f, gla_gb_f, gla_gw2_b, gla_gb_b, gla_norm_w, w_out, ln2_w, moe_w_coarse, moe_b_coarse, moe_w_fine, moe_b_fine, moe_w_gate, moe_w_up, moe_w_down, ln_f_w):
    batch, seq_len, d = x.shape
    xf = x.reshape(batch * seq_len, d)
    depth = w_in.shape[0]
    for l in range(depth):
        h2, idx, wc, counts, x1 = _layer(
            xf, batch, seq_len, ln1_w[l], w_in[l], rw_mu[l], rw_w0_f[l], rw_w2_f[l], rw_w0_b[l],
            rw_w2_b[l], rw_a0_f[l], rw_a2_f[l], rw_a0_b[l], rw_a2_b[l], rw_g2[l], rw_k_k[l],
            rw_k_a[l], rw_r_k[l], rw_ln_w[l], rw_ln_b[l], gla_gw2_f[l], gla_gb_f[l], gla_gw2_b[l],
            gla_gb_b[l], gla_norm_w[l], w_out[l], ln2_w[l], moe_w_coarse[l], moe_b_coarse[l],
            moe_w_fine[l], moe_b_fine[l], moe_w_gate[l], moe_w_up[l], moe_w_down[l])
        xf = _moe(h2, idx, wc, counts, moe_w_gate[l].astype(BF16), moe_w_up[l].astype(BF16),
                  moe_w_down[l].astype(BF16), x1, ln_f_w[None, :], l == depth - 1)
    return xf.reshape(batch, seq_len, d)
```

```python
import functools

import numpy as np
import jax
import jax.numpy as jnp
from jax import lax
from jax.experimental import pallas as pl
from jax.experimental.pallas import tpu as pltpu

F32 = jnp.float32
BF16 = jnp.bfloat16

RMS_EPS = 1e-6
RWKV_GN_EPS = 64e-5
RWKV_WIDTH = 512
RWKV_HEAD_DIM = 64
LORA = 64
GATE_LORA = 128
GLA_HEADS = 4
GLA_DK = 64
GLA_DV = 128
GLA_K_WIDTH = GLA_HEADS * GLA_DK
GLA_V_WIDTH = GLA_HEADS * GLA_DV
GLA_GATE_LORA = 16
GLA_TAU = 16.0
N_GROUPS = 4
EXPERTS_PER_GROUP = 8
N_EXPERTS = N_GROUPS * EXPERTS_PER_GROUP

CHUNK = 64
LANES = 128

COL_R, COL_K, COL_V = 0, 512, 1024
COL_LORA = 1536
COL_GL = 1792
COL_GG = 1920
COL_GQ, COL_GK, COL_GV, COL_OG = 2048, 2304, 2560, 3072
NP = 3584

VMEM_LIMIT = 56 * 1024 * 1024


def _cparams(sem):
    return pltpu.CompilerParams(dimension_semantics=sem, vmem_limit_bytes=VMEM_LIMIT)


MM_DTYPE = BF16


def _mm(a, b, ca=1, cb=0):
    dt = MM_DTYPE
    return lax.dot_general(a.astype(dt), b.astype(dt), (((ca,), (cb,)), ((), ())),
                           preferred_element_type=F32)


def _split2(x):
    hi = x.astype(BF16)
    lo = (x - hi.astype(F32)).astype(BF16)
    return hi, lo


def _split3(x):
    hi = x.astype(BF16)
    r1 = x - hi.astype(F32)
    mid = r1.astype(BF16)
    lo = (r1 - mid.astype(F32)).astype(BF16)
    return hi, mid, lo


def _mm_ones(x, ones_bf16, left=False):
    hi, mid, lo = _split3(x)
    if left:
        return (_mm(ones_bf16, lo) + _mm(ones_bf16, mid)) + _mm(ones_bf16, hi)
    return (_mm(lo, ones_bf16) + _mm(mid, ones_bf16)) + _mm(hi, ones_bf16)


def _mm_x3(a, b, ca=1, cb=0):
    ah, al = _split2(a)
    bh, bl = _split2(b)
    return (_mm(al, bh, ca, cb) + _mm(ah, bl, ca, cb)) + _mm(ah, bh, ca, cb)


def _rmsnorm_rows(x, w):
    ms = jnp.mean(x * x, axis=-1, keepdims=True)
    return x * lax.rsqrt(ms + RMS_EPS) * w


def _softplus(x):
    return jnp.maximum(x, 0.0) + jnp.log(1.0 + jnp.exp(-jnp.abs(x)))


def _sigmoid(x):
    return 1.0 / (1.0 + jnp.exp(-x))


def _seg_cumsum(x, seg, rev):
    n = x.shape[0]
    pos = lax.broadcasted_iota(jnp.int32, x.shape, 0) & (seg - 1)
    sh = 1
    while sh < seg:
        if not rev:
            x = x + jnp.where(pos >= sh, pltpu.roll(x, sh, 0), 0.0)
        else:
            x = x + jnp.where(pos < seg - sh, pltpu.roll(x, n - sh, 0), 0.0)
        sh *= 2
    return x


def _blockdiag2(y, m0):
    return jnp.concatenate([jnp.where(m0, y, 0.0), jnp.where(m0, 0.0, y)], axis=0)


def _inproj_kernel(x_ref, xp_ref, xn_ref, lnw_ref, w_ref, mu_ref, o_ref, *, tiles_per_seq,
                   shift_cols, tn):
    i = pl.program_id(0)
    lnw = lnw_ref[...]
    h = _rmsnorm_rows(x_ref[...], lnw).astype(BF16)
    t = i % tiles_per_seq
    hp = jnp.where(t == 0, 0.0, _rmsnorm_rows(xp_ref[...], lnw))
    hn = jnp.where(t == tiles_per_seq - 1, 0.0, _rmsnorm_rows(xn_ref[...], lnw))
    hh = jnp.concatenate([hp, hn], axis=0).astype(BF16)
    tm = h.shape[0]
    npad = w_ref.shape[1]
    bounds = [(c0, min(c0 + tn, shift_cols)) for c0 in range(0, shift_cols, tn)]
    bounds += [(c0, min(c0 + tn, npad)) for c0 in range(shift_cols, npad, tn)]
    for c0, c1 in bounds:
        w = w_ref[:, c0:c1]
        p = jnp.dot(h, w, preferred_element_type=F32)
        if c0 < shift_cols:
            ph = jnp.dot(hh, w, preferred_element_type=F32)
            row = lax.broadcasted_iota(jnp.int32, p.shape, 0)
            prev = jnp.where(row == 0, ph[7:8, :], pltpu.roll(p, 1, 0))
            nxt = jnp.where(row == tm - 1, ph[8:9, :], pltpu.roll(p, tm - 1, 0))
            p = p + mu_ref[:, c0:c1] * (0.5 * (prev + nxt) - p)
        o_ref[:, c0:c1] = p


def _inproj(xf, lnw, w_pad, mu_pad, seq_len, shift_cols, tm=512, tn=640):
    n, d = xf.shape
    npad = w_pad.shape[1]
    assert seq_len % tm == 0 and shift_cols % LANES == 0 and npad % LANES == 0 and tn % LANES == 0
    tps = seq_len // tm
    nb8 = n // 8
    return pl.pallas_call(
        functools.partial(_inproj_kernel, tiles_per_seq=tps, shift_cols=shift_cols, tn=tn),
        grid=(n // tm,),
        in_specs=[
            pl.BlockSpec((tm, d), lambda i: (i, 0)),
            pl.BlockSpec((8, d), lambda i: (jnp.maximum(i * (tm // 8) - 1, 0), 0)),
            pl.BlockSpec((8, d), lambda i: (jnp.minimum((i + 1) * (tm // 8), nb8 - 1), 0)),
            pl.BlockSpec((1, d), lambda i: (0, 0)),
            pl.BlockSpec((d, npad), lambda i: (0, 0)),
            pl.BlockSpec((1, npad), lambda i: (0, 0)),
        ],
        out_specs=pl.BlockSpec((tm, npad), lambda i: (i, 0)),
        out_shape=jax.ShapeDtypeStruct((n, npad), F32),
        compiler_params=_cparams(("parallel",)),
        name="inproj_shift",
    )(xf, xf, xf, lnw, w_pad, mu_pad)


def _rwkv_prep(d, r_ref, k_ref, v_ref, l_ref, w0_ref, w2_ref, a0_ref, a2_ref,
               kk_ref, ka_ref, rk_ref, blk_ref, bd_ref):
    rev = d == 1
    c = CHUNK
    r = r_ref[...]
    k = k_ref[...]
    lo = l_ref[...]
    blk = blk_ref[...]

    zw = w0_ref[d] + _mm(jnp.tanh(lo[:, 0:LANES]), w2_ref[d])
    lw = -jnp.exp(-_softplus(-zw) - 0.5)
    a = _sigmoid(a0_ref[d] + _mm(lo[:, LANES:2 * LANES], a2_ref[d]))
    kkr = k * kk_ref[...]
    ss = _mm_ones(kkr * kkr, blk)
    kk = kkr / jnp.maximum(jnp.sqrt(ss), 1e-12)
    kd = k * (1.0 + (a - 1.0) * ka_ref[...])
    b = kk * a
    bd_ref[...] = _mm_ones(r * kd * rk_ref[...], blk)

    cum = _seg_cumsum(lw, c, rev)
    tot = cum[0:1, :] if rev else cum[c - 1:c, :]
    ginv = jnp.exp(-cum)
    gend = jnp.exp(tot - cum)
    return dict(at=-kk * jnp.exp(cum - lw), rt=r * jnp.exp(cum), bt=b * ginv, kt=kd * ginv,
                bh=b * gend, kh=kd * gend, gc=jnp.exp(tot), v=v_ref[...])


def _rwkv_kernel(rf, kf, vf, lf, rb, kb, vb, lb, w0, w2, a0, a2, kk, ka, rk, blk,
                 yf, yb, bdf, bdb, m_scr):
    @pl.when(pl.program_id(1) == 0)
    def _():
        m_scr[...] = jnp.zeros(m_scr.shape, F32)

    c = CHUNK
    prep = [_rwkv_prep(0, rf, kf, vf, lf, w0, w2, a0, a2, kk, ka, rk, blk, bdf),
            _rwkv_prep(1, rb, kb, vb, lb, w0, w2, a0, a2, kk, ka, rk, blk, bdb)]
    y_refs = [yf, yb]

    row = lax.broadcasted_iota(jnp.int32, (c, LANES), 0)
    lane = lax.broadcasted_iota(jnp.int32, (c, LANES), 1)
    s_idx = lane & (RWKV_HEAD_DIM - 1)
    m0 = lane < RWKV_HEAD_DIM
    strict = [s_idx < row, s_idx > row]
    incl = [s_idx <= row, s_idx >= row]
    diag = s_idx == row

    def bd2(y):
        return _blockdiag2(y, m0)

    def fold(z):
        return jnp.where(m0, z[:c], 0.0) + jnp.where(m0, 0.0, z[c:])

    chains = [(d, p) for p in range(RWKV_WIDTH // LANES) for d in (0, 1)]
    ds = [d for d, _ in chains]

    def get(name):
        return [prep[d][name][:, p * LANES:(p + 1) * LANES] for d, p in chains]

    at, rt, bt, kt, bh, kh, vp, gc = (get(n) for n in ("at", "rt", "bt", "kt", "bh", "kh", "v", "gc"))
    x2 = [jnp.concatenate([a_, r_], axis=0) for a_, r_ in zip(at, rt)]
    sb = [_mm(x, bd2(y), 1, 1) for x, y in zip(x2, bt)]
    sk = [_mm(x, bd2(y), 1, 1) for x, y in zip(x2, kt)]
    lab = [jnp.where(strict[d], z[:c], 0.0) for d, z in zip(ds, sb)]
    prb = [jnp.where(incl[d], z[c:], 0.0) for d, z in zip(ds, sb)]
    lak = [jnp.where(strict[d], z[:c], 0.0) for d, z in zip(ds, sk)]
    prk = [jnp.where(incl[d], z[c:], 0.0) for d, z in zip(ds, sk)]
    bdv = [bd2(y) for y in vp]
    xa = at
    xv = [_mm(x, y) for x, y in zip(lak, bdv)]
    lp = lab
    n_it = 6
    for it in range(n_it):
        upd = [_mm(l_, jnp.concatenate([bd2(a_), bd2(v_)], axis=1)) for l_, a_, v_ in zip(lp, xa, xv)]
        xa = [a_ + u[:, :LANES] for a_, u in zip(xa, upd)]
        xv = [v_ + u[:, LANES:] for v_, u in zip(xv, upd)]
        if it < n_it - 1:
            lp = [_mm(l_, bd2(l_)) for l_ in lp]
    px = [_mm(x, jnp.concatenate([bd2(a_), bd2(v_)], axis=1)) for x, a_, v_ in zip(prb, xa, xv)]
    pv = [_mm(x, y) for x, y in zip(prk, bdv)]
    zb = [_mm(x, jnp.concatenate([a_, v_], axis=1), 0, 0) for x, a_, v_ in zip(bh, xa, xv)]
    zk = [_mm(x, y, 0, 0) for x, y in zip(kh, vp)]
    for i, (d, p) in enumerate(chains):
        qp = rt[i] + px[i][:, :LANES]
        yl = px[i][:, LANES:] + pv[i]
        g = jnp.where(diag, gc[i], 0.0) + fold(zb[i][:, :LANES])
        h = fold(zb[i][:, LANES:]) + fold(zk[i])
        bm = bd2(m_scr[d, p])
        yg = _mm(jnp.concatenate([qp, g], axis=0), bm)
        y_refs[d][:, p * LANES:(p + 1) * LANES] = yg[:c] + yl
        m_scr[d, p] = yg[c:] + h


def _rwkv_scan(p, batch, seq_len, w0, w2, a0, a2, kk, ka, rk, blk):
    n = p.shape[0]
    nc = seq_len // CHUNK
    w = RWKV_WIDTH

    def fwd(col, width):
        return pl.BlockSpec((CHUNK, width), lambda b, c: (b * nc + c, col // width))

    def bwd(col, width):
        return pl.BlockSpec((CHUNK, width), lambda b, c: (b * nc + nc - 1 - c, col // width))

    def full(a):
        return pl.BlockSpec(a.shape, lambda b, c: (0,) * a.ndim)

    lora_w = 2 * LANES
    in_specs = [fwd(COL_R, w), fwd(COL_K, w), fwd(COL_V, w), fwd(COL_LORA, lora_w),
                bwd(COL_R, w), bwd(COL_K, w), bwd(COL_V, w), bwd(COL_LORA, lora_w),
                full(w0), full(w2), full(a0), full(a2), full(kk), full(ka), full(rk), full(blk)]
    ospec_f = pl.BlockSpec((CHUNK, w), lambda b, c: (b * nc + c, 0))
    ospec_b = pl.BlockSpec((CHUNK, w), lambda b, c: (b * nc + nc - 1 - c, 0))
    osh = jax.ShapeDtypeStruct((n, w), F32)
    return pl.pallas_call(
        _rwkv_kernel,
        grid=(batch, nc),
        in_specs=in_specs,
        out_specs=[ospec_f, ospec_b, ospec_f, ospec_b],
        out_shape=[osh, osh, osh, osh],
        scratch_shapes=[pltpu.VMEM((2, w // LANES, CHUNK, LANES), F32)],
        compiler_params=_cparams(("parallel", "arbitrary")),
        name="rwkv7_scan",
    )(p, p, p, p, p, p, p, p, w0, w2, a0, a2, kk, ka, rk, blk)


def _gla_gates(d, g_ref, gw_ref, gb_ref, cum_ref):
    xg = _mm(g_ref[...], gw_ref[d]) + gb_ref[d]
    g = -_softplus(-xg) / GLA_TAU
    return _mm_ones(g, cum_ref[d], left=True)


def _gla_kernel(gf, qf, kf, vf, gb_, qb, kb, vb, gw, gbias, cum, of, ob, st_scr):
    @pl.when(pl.program_id(1) == 0)
    def _():
        st_scr[...] = jnp.zeros(st_scr.shape, F32)

    c = CHUNK
    o_refs = [of, ob]
    cds = [_gla_gates(0, gf, gw, gbias, cum), _gla_gates(1, gb_, gw, gbias, cum)]
    qs = [qf[...] * (GLA_DK ** -0.5), qb[...] * (GLA_DK ** -0.5)]
    ks = [kf[...], kb[...]]
    vs = [vf[...], vb[...]]

    row = lax.broadcasted_iota(jnp.int32, (c, LANES), 0)
    lane = lax.broadcasted_iota(jnp.int32, (c, LANES), 1)
    s_idx = lane & (GLA_DK - 1)
    m0 = lane < GLA_DK
    diag = row == s_idx
    wide_row = lax.broadcasted_iota(jnp.int32, (c, GLA_K_WIDTH), 0)
    wide_prog = [wide_row, c - 1 - wide_row]

    chains = [(d, p) for p in range(GLA_K_WIDTH // LANES) for d in (0, 1)]

    def pair(x, p):
        return x[:, p * LANES:(p + 1) * LANES]

    def nt(x, y):
        return _mm(x, _blockdiag2(y, m0), 1, 1)

    a = [jnp.where(diag, nt(pair(qs[d], p), pair(ks[d], p)), 0.0) for d, p in chains]
    s = 1
    lvl = 1
    while s < c:
        same = (row // (2 * s)) == (s_idx // (2 * s))
        ql, kl = [], []
        for d in (0, 1):
            upper = (wide_prog[d] & (2 * s - 1)) >= s
            e = jnp.exp(-jnp.abs(cds[d][lvl * c:(lvl + 1) * c]))
            ql.append(qs[d] * jnp.where(upper, e, 0.0))
            kl.append(ks[d] * jnp.where(upper, 0.0, e))
        upd = [jnp.where(same, nt(pair(ql[d], p), pair(kl[d], p)), 0.0) for d, p in chains]
        a = [x + y for x, y in zip(a, upd)]
        s *= 2
        lvl += 1

    bc = [cds[0][0:c], cds[1][0:c]]
    tot = [bc[0][c - 1:c, :], bc[1][0:1, :]]
    qt = [qs[d] * jnp.exp(bc[d]) for d in (0, 1)]
    kh = [ks[d] * jnp.exp(tot[d] - bc[d]) for d in (0, 1)]
    gc = [jnp.exp(tot[d]) for d in (0, 1)]

    def vhead(d, h):
        return vs[d][:, h * GLA_DV:(h + 1) * GLA_DV]

    st = [st_scr[d, p] for d, p in chains]
    oq = [_mm(_blockdiag2(pair(qt[d], p), m0), st[i], 1, 1) for i, (d, p) in enumerate(chains)]
    oa = [_mm(_blockdiag2(a[i], m0),
              jnp.concatenate([vhead(d, 2 * p), vhead(d, 2 * p + 1)], axis=0))
          for i, (d, p) in enumerate(chains)]
    z = [_mm(jnp.concatenate([vhead(d, 2 * p), vhead(d, 2 * p + 1)], axis=1), pair(kh[d], p), 0, 0)
         for d, p in chains]
    mv = lax.broadcasted_iota(jnp.int32, (GLA_DV, LANES), 1) < GLA_DK
    for i, (d, p) in enumerate(chains):
        o = oq[i] + oa[i]
        o_refs[d][:, (2 * p) * GLA_DV:(2 * p + 1) * GLA_DV] = o[:c]
        o_refs[d][:, (2 * p + 1) * GLA_DV:(2 * p + 2) * GLA_DV] = o[c:]
        st_scr[d, p] = (st[i] * pair(gc[d], p) + jnp.where(mv, z[i][:GLA_DV], 0.0)
                        + jnp.where(mv, 0.0, z[i][GLA_DV:]))


def _gla_cumsum_matrices():
    c = CHUNK
    i = np.arange(c)
    out = []
    for rev in (False, True):
        tri = (i[None, :] >= i[:, None]) if rev else (i[None, :] <= i[:, None])
        tri = tri.astype(np.float32)
        mats = [tri]
        s = 1
        while s < c:
            ref = (i // (2 * s)) * (2 * s) + (s if rev else s - 1)
            mats.append(tri - tri[ref])
            s *= 2
        out.append(np.concatenate(mats, axis=0))
    return jnp.asarray(np.stack(out), dtype=BF16)


def _gla_scan(p, batch, seq_len, gw, gbias):
    n = p.shape[0]
    nc = seq_len // CHUNK

    def fwd(col, width):
        return pl.BlockSpec((CHUNK, width), lambda b, c: (b * nc + c, col // width))

    def bwd(col, width):
        return pl.BlockSpec((CHUNK, width), lambda b, c: (b * nc + nc - 1 - c, col // width))

    def full(a):
        return pl.BlockSpec(a.shape, lambda b, c: (0,) * a.ndim)

    kw, vw = GLA_K_WIDTH, GLA_V_WIDTH
    cum = _gla_cumsum_matrices()
    in_specs = [fwd(COL_GG, LANES), fwd(COL_GQ, kw), fwd(COL_GK, kw), fwd(COL_GV, vw),
                bwd(COL_GG, LANES), bwd(COL_GQ, kw), bwd(COL_GK, kw), bwd(COL_GV, vw),
                full(gw), full(gbias), full(cum)]
    ospec_f = pl.BlockSpec((CHUNK, vw), lambda b, c: (b * nc + c, 0))
    ospec_b = pl.BlockSpec((CHUNK, vw), lambda b, c: (b * nc + nc - 1 - c, 0))
    osh = jax.ShapeDtypeStruct((n, vw), F32)
    return pl.pallas_call(
        _gla_kernel,
        grid=(batch, nc),
        in_specs=in_specs,
        out_specs=[ospec_f, ospec_b],
        out_shape=[osh, osh],
        scratch_shapes=[pltpu.VMEM((2, kw // LANES, GLA_DV, LANES), F32)],
        compiler_params=_cparams(("parallel", "arbitrary")),
        name="gla_scan",
    )(p, p, p, p, p, p, p, p, gw, gbias, cum)


def _mix_out_kernel(x_ref, yf, yb, bdf, bdb, v_ref, gl_ref, of, ob, og_ref,
                    g2_ref, lnw_ref, lnb_ref, blk_ref, nw_ref, wo_ref, o_ref):
    blk = blk_ref[...]
    inv = 1.0 / RWKV_HEAD_DIM
    y = yf[...] + yb[...]
    mean = _mm_ones(y, blk) * inv
    yc = y - mean
    var = _mm_ones(yc * yc, blk) * inv
    yn = yc * lax.rsqrt(var + RWKV_GN_EPS) * lnw_ref[...] + lnb_ref[...]
    bonus = (bdf[...] + bdb[...]) * v_ref[...]
    gate = _mm(_sigmoid(gl_ref[...]), g2_ref[...])
    y_rw = (yn + bonus) * gate

    o = of[...] + ob[...]
    og = og_ref[...]
    gsil = og * _sigmoid(og)
    nw = nw_ref[...]
    acc = x_ref[...] + _mm(y_rw, wo_ref[0:RWKV_WIDTH, :])
    for h in range(GLA_HEADS):
        sl = slice(h * GLA_DV, (h + 1) * GLA_DV)
        oh = o[:, sl]
        ms = jnp.mean(oh * oh, axis=-1, keepdims=True)
        yg = oh * lax.rsqrt(ms + RMS_EPS) * nw[:, sl] * gsil[:, sl]
        acc = acc + _mm(yg, wo_ref[RWKV_WIDTH + h * GLA_DV:RWKV_WIDTH + (h + 1) * GLA_DV, :])
    o_ref[...] = acc


def _mix_out(xf, p, yf, yb, bdf, bdb, of, ob, g2, lnw, lnb, blk, nw, wo, tm=256):
    n, d = xf.shape
    w = RWKV_WIDTH

    def rows(width, col=0):
        return pl.BlockSpec((tm, width), lambda i: (i, col // width))

    def full(a):
        return pl.BlockSpec(a.shape, lambda i: (0,) * a.ndim)

    in_specs = [rows(d), rows(w), rows(w), rows(w), rows(w), rows(w, COL_V), rows(LANES, COL_GL),
                rows(w), rows(w), rows(w, COL_OG),
                full(g2), full(lnw), full(lnb), full(blk), full(nw), full(wo)]
    return pl.pallas_call(
        _mix_out_kernel,
        grid=(n // tm,),
        in_specs=in_specs,
        out_specs=rows(d),
        out_shape=jax.ShapeDtypeStruct((n, d), F32),
        compiler_params=_cparams(("parallel",)),
        name="mix_outproj",
    )(xf, yf, yb, bdf, bdb, p, p, of, ob, p, g2, lnw, lnb, blk, nw, wo)


def _router_kernel(x_ref, lnw_ref, wr_ref, br_ref, h_ref, idx_ref, wc_ref, cnt_ref, base):
    @pl.when(pl.program_id(0) == 0)
    def _():
        base[...] = jnp.zeros(base.shape, F32)

    h = _rmsnorm_rows(x_ref[...], lnw_ref[...])
    h_ref[...] = h
    logits = _mm_x3(wr_ref[...], h, 1, 1) + br_ref[...]
    tm = logits.shape[1]
    coarse = logits[0:N_GROUPS, :]
    fine = logits[8:8 + N_EXPERTS, :]
    rowg = lax.broadcasted_iota(jnp.int32, (N_GROUPS, tm), 0)
    cmax = jnp.max(coarse, axis=0, keepdims=True)
    gsel = jnp.min(jnp.where(coarse == cmax, rowg, N_GROUPS), axis=0, keepdims=True)
    pg = 1.0 / jnp.sum(jnp.exp(coarse - cmax), axis=0, keepdims=True)
    sel = jnp.zeros((EXPERTS_PER_GROUP, tm), F32)
    for g in range(N_GROUPS):
        sel = sel + jnp.where(gsel == g, fine[g * EXPERTS_PER_GROUP:(g + 1) * EXPERTS_PER_GROUP, :], 0.0)
    rowe = lax.broadcasted_iota(jnp.int32, (EXPERTS_PER_GROUP, tm), 0)
    l1 = jnp.max(sel, axis=0, keepdims=True)
    i1 = jnp.min(jnp.where(sel == l1, rowe, EXPERTS_PER_GROUP), axis=0, keepdims=True)
    sel2 = jnp.where(rowe == i1, -jnp.inf, sel)
    l2 = jnp.max(sel2, axis=0, keepdims=True)
    i2 = jnp.min(jnp.where(sel2 == l2, rowe, EXPERTS_PER_GROUP), axis=0, keepdims=True)
    t = jnp.exp(l2 - l1)
    w1 = pg / (1.0 + t)
    w2 = pg * t / (1.0 + t)
    e1 = gsel * EXPERTS_PER_GROUP + i1
    e2 = gsel * EXPERTS_PER_GROUP + i2
    rowx = lax.broadcasted_iota(jnp.int32, (N_EXPERTS, tm), 0)
    hit1 = rowx == e1
    hit2 = rowx == e2
    oh = jnp.where(hit1 | hit2, 1.0, 0.0)
    tr = lax.broadcasted_iota(jnp.int32, (tm, tm), 0)
    tc = lax.broadcasted_iota(jnp.int32, (tm, tm), 1)
    before = jnp.where(tr < tc, 1.0, 0.0).astype(BF16)
    pos = base[:, 0:1] + _mm(oh, before)
    r1 = jnp.sum(jnp.where(hit1, pos, 0.0), axis=0, keepdims=True)
    r2 = jnp.sum(jnp.where(hit2, pos, 0.0), axis=0, keepdims=True)
    new_base = base[...] + jnp.sum(oh, axis=1, keepdims=True)
    base[...] = new_base
    cnt_ref[...] = new_base
    zi = jnp.zeros((4, tm), jnp.int32)
    idx_ref[...] = jnp.concatenate([e1, e2, r1.astype(jnp.int32), r2.astype(jnp.int32), zi], axis=0)
    zf = jnp.zeros((6, tm), F32)
    wc_ref[...] = jnp.transpose(jnp.concatenate([w1, w2, zf], axis=0))


def _router(x1, lnw, wr, br, tm=256):
    n, d = x1.shape

    def full(a):
        return pl.BlockSpec(a.shape, lambda i: (0,) * a.ndim)

    return pl.pallas_call(
        _router_kernel,
        grid=(n // tm,),
        in_specs=[pl.BlockSpec((tm, d), lambda i: (i, 0)), full(lnw), full(wr), full(br)],
        out_specs=[pl.BlockSpec((tm, d), lambda i: (i, 0)),
                   pl.BlockSpec((8, tm), lambda i: (0, i)),
                   pl.BlockSpec((tm, 8), lambda i: (i, 0)),
                   pl.BlockSpec((N_EXPERTS, LANES), lambda i: (0, 0))],
        out_shape=[jax.ShapeDtypeStruct((n, d), F32),
                   jax.ShapeDtypeStruct((8, n), jnp.int32),
                   jax.ShapeDtypeStruct((n, 8), F32),
                   jax.ShapeDtypeStruct((N_EXPERTS, LANES), F32)],
        scratch_shapes=[pltpu.VMEM((N_EXPERTS, LANES), F32)],
        compiler_params=_cparams(("arbitrary",)),
        name="router",
    )(x1, lnw, wr, br)


MOE_ROWS = 256
DMA_UNROLL = 8


def _slot(pstart_ref, idx_ref, k, t):
    return pstart_ref[idx_ref[k, t]] + idx_ref[2 + k, t]


def _dispatch_kernel(pstart_ref, pend_ref, idx_ref, h_ref, xs_out, zbuf, sem, zsem):
    tm = idx_ref.shape[1]

    @pl.when(pl.program_id(0) == 0)
    def _():
        zbuf[...] = jnp.zeros(zbuf.shape, F32)

        def zero_copy(e):
            first = pl.multiple_of(pend_ref[e] - MOE_ROWS, MOE_ROWS)
            return pltpu.make_async_copy(zbuf, xs_out.at[pl.ds(first, MOE_ROWS)], zsem)

        def zstart(e, carry):
            @pl.when(pend_ref[e] > pstart_ref[e])
            def _():
                zero_copy(e).start()
            return carry

        def zwait(e, carry):
            @pl.when(pend_ref[e] > pstart_ref[e])
            def _():
                zero_copy(e).wait()
            return carry

        lax.fori_loop(0, N_EXPERTS, zstart, 0)
        lax.fori_loop(0, N_EXPERTS, zwait, 0)

        def tail_copy(b):
            return pltpu.make_async_copy(
                zbuf, xs_out.at[pl.ds(pl.multiple_of(b * MOE_ROWS, MOE_ROWS), MOE_ROWS)], zsem)

        def tstart(b, carry):
            tail_copy(b).start()
            return carry

        def twait(b, carry):
            tail_copy(b).wait()
            return carry

        first_free = pend_ref[N_EXPERTS - 1] // MOE_ROWS
        n_blocks = xs_out.shape[0] // MOE_ROWS
        lax.fori_loop(first_free, n_blocks, tstart, 0)
        lax.fori_loop(first_free, n_blocks, twait, 0)

    def row_copy(t, k):
        return pltpu.make_async_copy(h_ref.at[pl.ds(t, 1)],
                                     xs_out.at[pl.ds(_slot(pstart_ref, idx_ref, k, t), 1)], sem)

    def start(t, carry):
        row_copy(t, 0).start()
        row_copy(t, 1).start()
        return carry

    def wait(t, carry):
        row_copy(t, 0).wait()
        row_copy(t, 1).wait()
        return carry

    lax.fori_loop(0, tm, start, 0, unroll=DMA_UNROLL)
    lax.fori_loop(0, tm, wait, 0, unroll=DMA_UNROLL)


def _dispatch(pstart, pend, idx, h2, n_rows, tm=256):
    n, d = h2.shape
    return pl.pallas_call(
        _dispatch_kernel,
        grid_spec=pltpu.PrefetchScalarGridSpec(
            num_scalar_prefetch=2,
            grid=(n // tm,),
            in_specs=[pl.BlockSpec((8, tm), lambda i, ps, pe: (0, i), memory_space=pltpu.SMEM),
                      pl.BlockSpec((tm, d), lambda i, ps, pe: (i, 0))],
            out_specs=pl.BlockSpec(memory_space=pl.ANY),
            scratch_shapes=[pltpu.VMEM((MOE_ROWS, d), F32), pltpu.SemaphoreType.DMA(()),
                            pltpu.SemaphoreType.DMA(())]),
        out_shape=jax.ShapeDtypeStruct((n_rows, d), F32),
        compiler_params=_cparams(("arbitrary",)),
        name="moe_dispatch",
    )(pstart, pend, idx, h2)


def _expert_kernel(be_ref, nu_ref, x_ref, wg_ref, wu_ref, wd_ref, o_ref):
    i = pl.program_id(0)

    @pl.when(i < nu_ref[0])
    def _():
        x = x_ref[...].astype(BF16)
        a = jnp.dot(x, wg_ref[0], preferred_element_type=F32)
        u = jnp.dot(x, wu_ref[0], preferred_element_type=F32)
        hh = (a * _sigmoid(a)) * u
        o_ref[...] = jnp.dot(hh.astype(BF16), wd_ref[0], preferred_element_type=F32)

    @pl.when(i >= nu_ref[0])
    def _():
        o_ref[...] = jnp.zeros(o_ref.shape, F32)


def _experts(block_e, n_used, xs, wg, wu, wd):
    n_rows, d = xs.shape
    _, _, de = wg.shape
    return pl.pallas_call(
        _expert_kernel,
        grid_spec=pltpu.PrefetchScalarGridSpec(
            num_scalar_prefetch=2,
            grid=(n_rows // MOE_ROWS,),
            in_specs=[pl.BlockSpec((MOE_ROWS, d), lambda i, be, nu: (jnp.minimum(i, nu[0] - 1), 0)),
                      pl.BlockSpec((1, d, de), lambda i, be, nu: (be[i], 0, 0)),
                      pl.BlockSpec((1, d, de), lambda i, be, nu: (be[i], 0, 0)),
                      pl.BlockSpec((1, de, d), lambda i, be, nu: (be[i], 0, 0))],
            out_specs=pl.BlockSpec((MOE_ROWS, d), lambda i, be, nu: (i, 0))),
        out_shape=jax.ShapeDtypeStruct((n_rows, d), F32),
        compiler_params=_cparams(("arbitrary",)),
        name="moe_experts",
    )(block_e, n_used, xs, wg, wu, wd)


def _combine_kernel(pstart_ref, idx_ref, ys_hbm, x_ref, w_ref, lnf_ref, o_ref, y1, y2, sem, *,
                    final_norm):
    tm = idx_ref.shape[1]

    def row_copy(t, k):
        buf = y1 if k == 0 else y2
        return pltpu.make_async_copy(ys_hbm.at[pl.ds(_slot(pstart_ref, idx_ref, k, t), 1)],
                                     buf.at[pl.ds(t, 1)], sem)

    def start(t, carry):
        row_copy(t, 0).start()
        row_copy(t, 1).start()
        return carry

    def wait(t, carry):
        row_copy(t, 0).wait()
        row_copy(t, 1).wait()
        return carry

    lax.fori_loop(0, tm, start, 0, unroll=DMA_UNROLL)
    lax.fori_loop(0, tm, wait, 0, unroll=DMA_UNROLL)
    w = w_ref[...]
    y = x_ref[...] + w[:, 0:1] * y1[...] + w[:, 1:2] * y2[...]
    o_ref[...] = _rmsnorm_rows(y, lnf_ref[...]) if final_norm else y


def _combine(pstart, idx, ys, x1, wc, lnf, final_norm, tm=256):
    n, d = x1.shape
    return pl.pallas_call(
        functools.partial(_combine_kernel, final_norm=final_norm),
        grid_spec=pltpu.PrefetchScalarGridSpec(
            num_scalar_prefetch=1,
            grid=(n // tm,),
            in_specs=[pl.BlockSpec((8, tm), lambda i, ps: (0, i), memory_space=pltpu.SMEM),
                      pl.BlockSpec(memory_space=pl.ANY),
                      pl.BlockSpec((tm, d), lambda i, ps: (i, 0)),
                      pl.BlockSpec((tm, 8), lambda i, ps: (i, 0)),
                      pl.BlockSpec((1, d), lambda i, ps: (0, 0))],
            out_specs=pl.BlockSpec((tm, d), lambda i, ps: (i, 0)),
            scratch_shapes=[pltpu.VMEM((tm, d), F32), pltpu.VMEM((tm, d), F32),
                            pltpu.SemaphoreType.DMA(())]),
        out_shape=jax.ShapeDtypeStruct((n, d), F32),
        compiler_params=_cparams(("arbitrary",)),
        name="moe_combine",
    )(pstart, idx, ys, x1, wc, lnf)


def _moe(h2, idx, wc, counts, wg, wu, wd, x1, lnf, final_norm):
    n = h2.shape[0]
    n_rows = 2 * n + N_EXPERTS * MOE_ROWS
    cnt = counts[:, 0].astype(jnp.int32)
    padded = ((cnt + MOE_ROWS - 1) // MOE_ROWS) * MOE_ROWS
    pend = jnp.cumsum(padded)
    pstart = pend - padded
    blk_first = jnp.arange(n_rows // MOE_ROWS, dtype=jnp.int32) * MOE_ROWS
    block_e = jnp.minimum(jnp.sum(pend[None, :] <= blk_first[:, None], axis=1), N_EXPERTS - 1)
    n_used = (pend[-1:] // MOE_ROWS).astype(jnp.int32)
    xs = _dispatch(pstart, pend, idx, h2, n_rows)
    ys = _experts(block_e.astype(jnp.int32), n_used, xs, wg, wu, wd)
    return _combine(pstart, idx, ys, x1, wc, lnf, final_norm)


def _pad_rows(a, before, total):
    return jnp.pad(a, ((before, total - before - a.shape[0]), (0, 0)))


def _layer(xf, batch, seq_len, ln1_w, w_in, rw_mu, rw_w0_f, rw_w2_f, rw_w0_b, rw_w2_b, rw_a0_f,
           rw_a2_f, rw_a0_b, rw_a2_b, rw_g2, rw_k_k, rw_k_a, rw_r_k, rw_ln_w, rw_ln_b, gla_gw2_f,
           gla_gb_f, gla_gw2_b, gla_gb_b, gla_norm_w, w_out, ln2_w, moe_w_coarse, moe_b_coarse,
           moe_w_fine, moe_b_fine, moe_w_gate, moe_w_up, moe_w_down):
    d = xf.shape[1]
    rw_cols = 3 * RWKV_WIDTH + 4 * LORA + GATE_LORA
    w_rw, w_gla = w_in[:, :rw_cols], w_in[:, rw_cols:]
    o = 0
    w_gq = w_gla[:, o:o + GLA_K_WIDTH]; o += GLA_K_WIDTH
    w_gk = w_gla[:, o:o + GLA_K_WIDTH]; o += GLA_K_WIDTH
    w_gv = w_gla[:, o:o + GLA_V_WIDTH]; o += GLA_V_WIDTH
    w_gg = w_gla[:, o:o + 2 * GLA_GATE_LORA]; o += 2 * GLA_GATE_LORA
    w_og = w_gla[:, o:o + GLA_V_WIDTH]
    w_gg = jnp.pad(w_gg, ((0, 0), (0, LANES - 2 * GLA_GATE_LORA)))
    w_pad = jnp.concatenate([w_rw, w_gg, w_gq, w_gk, w_gv, w_og], axis=1).astype(BF16)
    mu_pad = jnp.pad(rw_mu, (0, NP - rw_cols))[None, :]

    p = _inproj(xf, ln1_w[None, :], w_pad, mu_pad, seq_len, rw_cols)

    head = lax.broadcasted_iota(jnp.int32, (RWKV_WIDTH, RWKV_WIDTH), 0) // RWKV_HEAD_DIM
    blk = (head == head.T).astype(BF16)
    w0 = jnp.stack([rw_w0_f, rw_w0_b])[:, None, :]
    a0 = jnp.stack([rw_a0_f, rw_a0_b])[:, None, :]
    w2 = jnp.stack([_pad_rows(rw_w2_f, 0, LANES), _pad_rows(rw_w2_b, LORA, LANES)]).astype(BF16)
    a2 = jnp.stack([_pad_rows(rw_a2_f, 0, LANES), _pad_rows(rw_a2_b, LORA, LANES)]).astype(BF16)
    yf, yb, bdf, bdb = _rwkv_scan(p, batch, seq_len, w0, w2, a0, a2, rw_k_k[None, :],
                                  rw_k_a[None, :], rw_r_k.reshape(1, -1), blk)

    gw = jnp.stack([_pad_rows(gla_gw2_f, 0, LANES),
                    _pad_rows(gla_gw2_b, GLA_GATE_LORA, LANES)]).astype(BF16)
    gbias = jnp.stack([gla_gb_f, gla_gb_b])[:, None, :]
    of, ob = _gla_scan(p, batch, seq_len, gw, gbias)

    x1 = _mix_out(xf, p, yf, yb, bdf, bdb, of, ob, rw_g2.astype(BF16), rw_ln_w[None, :],
                  rw_ln_b[None, :], blk, gla_norm_w[None, :], w_out.astype(BF16))

    wr = jnp.concatenate([moe_w_coarse.T, jnp.zeros((8 - N_GROUPS, d), F32), moe_w_fine.T], axis=0)
    br = jnp.concatenate([moe_b_coarse, jnp.zeros((8 - N_GROUPS,), F32), moe_b_fine])[:, None]
    h2, idx, wc, counts = _router(x1, ln2_w[None, :], wr, br)
    return h2, idx, wc, counts, x1


def kernel(x, ln1_w, w_in, rw_mu, rw_w0_f, rw_w2_f, rw_w0_b, rw_w2_b, rw_a0_f, rw_a2_f, rw_a0_b, rw_a2_b, rw_g2, rw_k_k, rw_k_a, rw_r_k, rw_ln_w, rw_ln_b, gla_gw2_f, gla_gb_f, gla_gw2_b, gla_gb_b, gla_norm_w, w_out, ln2_w, moe_w_coarse, moe_b_coarse, moe_w_fine, moe_b_fine, moe_w_gate, moe_w_up, moe_w_down, ln_f_w):
    batch, seq_len, d = x.shape
    xf = x.reshape(batch * seq_len, d)
    depth = w_in.shape[0]
    for l in range(depth):
        h2, idx, wc, counts, x1 = _layer(
            xf, batch, seq_len, ln1_w[l], w_in[l], rw_mu[l], rw_w0_f[l], rw_w2_f[l], rw_w0_b[l],
            rw_w2_b[l], rw_a0_f[l], rw_a2_f[l], rw_a0_b[l], rw_a2_b[l], rw_g2[l], rw_k_k[l],
            rw_k_a[l], rw_r_k[l], rw_ln_w[l], rw_ln_b[l], gla_gw2_f[l], gla_gb_f[l], gla_gw2_b[l],
            gla_gb_b[l], gla_norm_w[l], w_out[l], ln2_w[l], moe_w_coarse[l], moe_b_coarse[l],
            moe_w_fine[l], moe_b_fine[l], moe_w_gate[l], moe_w_up[l], moe_w_down[l])
        xf = _moe(h2, idx, wc, counts, moe_w_gate[l].astype(BF16), moe_w_up[l].astype(BF16),
                  moe_w_down[l].astype(BF16), x1, ln_f_w[None, :], l == depth - 1)
    return xf.reshape(batch, seq_len, d)
```

```python
import functools

import numpy as np
import jax
import jax.numpy as jnp
from jax import lax
from jax.experimental import pallas as pl
from jax.experimental.pallas import tpu as pltpu

F32 = jnp.float32
BF16 = jnp.bfloat16

RMS_EPS = 1e-6
RWKV_GN_EPS = 64e-5
RWKV_WIDTH = 512
RWKV_HEAD_DIM = 64
LORA = 64
GATE_LORA = 128
GLA_HEADS = 4
GLA_DK = 64
GLA_DV = 128
GLA_K_WIDTH = GLA_HEADS * GLA_DK
GLA_V_WIDTH = GLA_HEADS * GLA_DV
GLA_GATE_LORA = 16
GLA_TAU = 16.0
N_GROUPS = 4
EXPERTS_PER_GROUP = 8
N_EXPERTS = N_GROUPS * EXPERTS_PER_GROUP

CHUNK = 64
LANES = 128

COL_R, COL_K, COL_V = 0, 512, 1024
COL_LORA = 1536
COL_GL = 1792
COL_GG = 1920
COL_GQ, COL_GK, COL_GV, COL_OG = 2048, 2304, 2560, 3072
NP = 3584

VMEM_LIMIT = 56 * 1024 * 1024


def _cparams(sem):
    return pltpu.CompilerParams(dimension_semantics=sem, vmem_limit_bytes=VMEM_LIMIT)


MM_DTYPE = BF16


def _mm(a, b, ca=1, cb=0):
    dt = MM_DTYPE
    return lax.dot_general(a.astype(dt), b.astype(dt), (((ca,), (cb,)), ((), ())),
                           preferred_element_type=F32)


def _split2(x):
    hi = x.astype(BF16)
    lo = (x - hi.astype(F32)).astype(BF16)
    return hi, lo


def _split3(x):
    hi = x.astype(BF16)
    r1 = x - hi.astype(F32)
    mid = r1.astype(BF16)
    lo = (r1 - mid.astype(F32)).astype(BF16)
    return hi, mid, lo


def _mm_ones(x, ones_bf16, left=False):
    hi, mid, lo = _split3(x)
    if left:
        return (_mm(ones_bf16, lo) + _mm(ones_bf16, mid)) + _mm(ones_bf16, hi)
    return (_mm(lo, ones_bf16) + _mm(mid, ones_bf16)) + _mm(hi, ones_bf16)


def _mm_x3(a, b, ca=1, cb=0):
    ah, al = _split2(a)
    bh, bl = _split2(b)
    return (_mm(al, bh, ca, cb) + _mm(ah, bl, ca, cb)) + _mm(ah, bh, ca, cb)


def _rmsnorm_rows(x, w):
    ms = jnp.mean(x * x, axis=-1, keepdims=True)
    return x * lax.rsqrt(ms + RMS_EPS) * w


def _softplus(x):
    return jnp.maximum(x, 0.0) + jnp.log(1.0 + jnp.exp(-jnp.abs(x)))


def _sigmoid(x):
    return 1.0 / (1.0 + jnp.exp(-x))


def _seg_cumsum(x, seg, rev):
    n = x.shape[0]
    pos = lax.broadcasted_iota(jnp.int32, x.shape, 0) & (seg - 1)
    sh = 1
    while sh < seg:
        if not rev:
            x = x + jnp.where(pos >= sh, pltpu.roll(x, sh, 0), 0.0)
        else:
            x = x + jnp.where(pos < seg - sh, pltpu.roll(x, n - sh, 0), 0.0)
        sh *= 2
    return x


def _blockdiag2(y, m0):
    return jnp.concatenate([jnp.where(m0, y, 0.0), jnp.where(m0, 0.0, y)], axis=0)


def _inproj_kernel(x_ref, xp_ref, xn_ref, lnw_ref, w_ref, mu_ref, o_ref, *, tiles_per_seq,
                   shift_cols, tn):
    i = pl.program_id(0)
    lnw = lnw_ref[...]
    h = _rmsnorm_rows(x_ref[...], lnw).astype(BF16)
    t = i % tiles_per_seq
    hp = jnp.where(t == 0, 0.0, _rmsnorm_rows(xp_ref[...], lnw))
    hn = jnp.where(t == tiles_per_seq - 1, 0.0, _rmsnorm_rows(xn_ref[...], lnw))
    hh = jnp.concatenate([hp, hn], axis=0).astype(BF16)
    tm = h.shape[0]
    npad = w_ref.shape[1]
    bounds = [(c0, min(c0 + tn, shift_cols)) for c0 in range(0, shift_cols, tn)]
    bounds += [(c0, min(c0 + tn, npad)) for c0 in range(shift_cols, npad, tn)]
    for c0, c1 in bounds:
        w = w_ref[:, c0:c1]
        p = jnp.dot(h, w, preferred_element_type=F32)
        if c0 < shift_cols:
            ph = jnp.dot(hh, w, preferred_element_type=F32)
            row = lax.broadcasted_iota(jnp.int32, p.shape, 0)
            prev = jnp.where(row == 0, ph[7:8, :], pltpu.roll(p, 1, 0))
            nxt = jnp.where(row == tm - 1, ph[8:9, :], pltpu.roll(p, tm - 1, 0))
            p = p + mu_ref[:, c0:c1] * (0.5 * (prev + nxt) - p)
        o_ref[:, c0:c1] = p


def _inproj(xf, lnw, w_pad, mu_pad, seq_len, shift_cols, tm=512, tn=640):
    n, d = xf.shape
    npad = w_pad.shape[1]
    assert seq_len % tm == 0 and shift_cols % LANES == 0 and npad % LANES == 0 and tn % LANES == 0
    tps = seq_len // tm
    nb8 = n // 8
    return pl.pallas_call(
        functools.partial(_inproj_kernel, tiles_per_seq=tps, shift_cols=shift_cols, tn=tn),
        grid=(n // tm,),
        in_specs=[
            pl.BlockSpec((tm, d), lambda i: (i, 0)),
            pl.BlockSpec((8, d), lambda i: (jnp.maximum(i * (tm // 8) - 1, 0), 0)),
            pl.BlockSpec((8, d), lambda i: (jnp.minimum((i + 1) * (tm // 8), nb8 - 1), 0)),
            pl.BlockSpec((1, d), lambda i: (0, 0)),
            pl.BlockSpec((d, npad), lambda i: (0, 0)),
            pl.BlockSpec((1, npad), lambda i: (0, 0)),
        ],
        out_specs=pl.BlockSpec((tm, npad), lambda i: (i, 0)),
        out_shape=jax.ShapeDtypeStruct((n, npad), F32),
        compiler_params=_cparams(("parallel",)),
        name="inproj_shift",
    )(xf, xf, xf, lnw, w_pad, mu_pad)


RWKV_CPS = 2


def _rwkv_prep(d, r_ref, k_ref, v_ref, l_ref, w0_ref, w2_ref, a0_ref, a2_ref,
               kk_ref, ka_ref, rk_ref, blk_ref, bd_ref):
    rev = d == 1
    c = CHUNK
    r = r_ref[...]
    k = k_ref[...]
    lo = l_ref[...]
    blk = blk_ref[...]

    zw = w0_ref[d] + _mm(jnp.tanh(lo[:, 0:LANES]), w2_ref[d])
    lw = -jnp.exp(-_softplus(-zw) - 0.5)
    a = _sigmoid(a0_ref[d] + _mm(lo[:, LANES:2 * LANES], a2_ref[d]))
    kkr = k * kk_ref[...]
    ss = _mm_ones(kkr * kkr, blk)
    kk = kkr / jnp.maximum(jnp.sqrt(ss), 1e-12)
    kd = k * (1.0 + (a - 1.0) * ka_ref[...])
    b = kk * a
    bd_ref[...] = _mm_ones(r * kd * rk_ref[...], blk)

    cum = _seg_cumsum(lw, c, rev)
    n_blk = r.shape[0] // c
    tot = [cum[j * c:j * c + 1, :] if rev else cum[j * c + c - 1:j * c + c, :] for j in range(n_blk)]
    ginv = jnp.exp(-cum)
    gend = jnp.concatenate([jnp.exp(tot[j] - cum[j * c:(j + 1) * c]) for j in range(n_blk)], axis=0)
    return dict(at=-kk * jnp.exp(cum - lw), rt=r * jnp.exp(cum), bt=b * ginv, kt=kd * ginv,
                bh=b * gend, kh=kd * gend, gc=[jnp.exp(t) for t in tot], v=v_ref[...])


def _rwkv_kernel(rf, kf, vf, lf, rb, kb, vb, lb, w0, w2, a0, a2, kk, ka, rk, blk,
                 yf, yb, bdf, bdb, m_scr):
    @pl.when(pl.program_id(1) == 0)
    def _():
        m_scr[...] = jnp.zeros(m_scr.shape, F32)

    c = CHUNK
    n_blk = rf.shape[0] // c
    n_pair = RWKV_WIDTH // LANES
    prep = [_rwkv_prep(0, rf, kf, vf, lf, w0, w2, a0, a2, kk, ka, rk, blk, bdf),
            _rwkv_prep(1, rb, kb, vb, lb, w0, w2, a0, a2, kk, ka, rk, blk, bdb)]
    y_refs = [yf, yb]

    row = lax.broadcasted_iota(jnp.int32, (c, LANES), 0)
    lane = lax.broadcasted_iota(jnp.int32, (c, LANES), 1)
    s_idx = lane & (RWKV_HEAD_DIM - 1)
    m0 = lane < RWKV_HEAD_DIM
    strict = [s_idx < row, s_idx > row]
    incl = [s_idx <= row, s_idx >= row]
    diag = s_idx == row

    def bd2(y):
        return _blockdiag2(y, m0)

    def fold(z):
        return jnp.where(m0, z[:c], 0.0) + jnp.where(m0, 0.0, z[c:])

    chains = [(d, p, j) for j in range(n_blk) for p in range(n_pair) for d in (0, 1)]
    ds = [d for d, _, _ in chains]

    def get(name):
        return [prep[d][name][j * c:(j + 1) * c, p * LANES:(p + 1) * LANES] for d, p, j in chains]

    at, rt, bt, kt, bh, kh, vp = (get(n) for n in ("at", "rt", "bt", "kt", "bh", "kh", "v"))
    gc = [prep[d]["gc"][j][:, p * LANES:(p + 1) * LANES] for d, p, j in chains]
    x2 = [jnp.concatenate([a_, r_], axis=0) for a_, r_ in zip(at, rt)]
    sb = [_mm(x, bd2(y), 1, 1) for x, y in zip(x2, bt)]
    sk = [_mm(x, bd2(y), 1, 1) for x, y in zip(x2, kt)]
    lab = [jnp.where(strict[d], z[:c], 0.0) for d, z in zip(ds, sb)]
    prb = [jnp.where(incl[d], z[c:], 0.0) for d, z in zip(ds, sb)]
    lak = [jnp.where(strict[d], z[:c], 0.0) for d, z in zip(ds, sk)]
    prk = [jnp.where(incl[d], z[c:], 0.0) for d, z in zip(ds, sk)]
    bdv = [bd2(y) for y in vp]
    xa = at
    xv = [_mm(x, y) for x, y in zip(lak, bdv)]
    lp = lab
    n_it = 6
    for it in range(n_it):
        upd = [_mm(l_, jnp.concatenate([bd2(a_), bd2(v_)], axis=1)) for l_, a_, v_ in zip(lp, xa, xv)]
        xa = [a_ + u[:, :LANES] for a_, u in zip(xa, upd)]
        xv = [v_ + u[:, LANES:] for v_, u in zip(xv, upd)]
        if it < n_it - 1:
            lp = [_mm(l_, bd2(l_)) for l_ in lp]
    px = [_mm(x, jnp.concatenate([bd2(a_), bd2(v_)], axis=1)) for x, a_, v_ in zip(prb, xa, xv)]
    pv = [_mm(x, y) for x, y in zip(prk, bdv)]
    zb = [_mm(x, jnp.concatenate([a_, v_], axis=1), 0, 0) for x, a_, v_ in zip(bh, xa, xv)]
    zk = [_mm(x, y, 0, 0) for x, y in zip(kh, vp)]
    qg, yl, hh = {}, {}, {}
    for i, key in enumerate(chains):
        qp = rt[i] + px[i][:, :LANES]
        g = jnp.where(diag, gc[i], 0.0) + fold(zb[i][:, :LANES])
        qg[key] = jnp.concatenate([qp, g], axis=0)
        yl[key] = px[i][:, LANES:] + pv[i]
        hh[key] = fold(zb[i][:, LANES:]) + fold(zk[i])
    state = {(d, p): m_scr[d, p] for p in range(n_pair) for d in (0, 1)}
    for step in range(n_blk):
        for p in range(n_pair):
            for d in (0, 1):
                j = n_blk - 1 - step if d == 1 else step
                yg = _mm(qg[(d, p, j)], bd2(state[(d, p)]))
                y_refs[d][j * c:(j + 1) * c, p * LANES:(p + 1) * LANES] = yg[:c] + yl[(d, p, j)]
                state[(d, p)] = yg[c:] + hh[(d, p, j)]
    for (d, p), m in state.items():
        m_scr[d, p] = m


def _rwkv_scan(p, batch, seq_len, w0, w2, a0, a2, kk, ka, rk, blk):
    n = p.shape[0]
    rows = RWKV_CPS * CHUNK
    assert seq_len % rows == 0
    ns = seq_len // rows
    w = RWKV_WIDTH

    def fwd(col, width):
        return pl.BlockSpec((rows, width), lambda b, c: (b * ns + c, col // width))

    def bwd(col, width):
        return pl.BlockSpec((rows, width), lambda b, c: (b * ns + ns - 1 - c, col // width))

    def full(a):
        return pl.BlockSpec(a.shape, lambda b, c: (0,) * a.ndim)

    lora_w = 2 * LANES
    in_specs = [fwd(COL_R, w), fwd(COL_K, w), fwd(COL_V, w), fwd(COL_LORA, lora_w),
                bwd(COL_R, w), bwd(COL_K, w), bwd(COL_V, w), bwd(COL_LORA, lora_w),
                full(w0), full(w2), full(a0), full(a2), full(kk), full(ka), full(rk), full(blk)]
    ospec_f = pl.BlockSpec((rows, w), lambda b, c: (b * ns + c, 0))
    ospec_b = pl.BlockSpec((rows, w), lambda b, c: (b * ns + ns - 1 - c, 0))
    osh = jax.ShapeDtypeStruct((n, w), F32)
    return pl.pallas_call(
        _rwkv_kernel,
        grid=(batch, ns),
        in_specs=in_specs,
        out_specs=[ospec_f, ospec_b, ospec_f, ospec_b],
        out_shape=[osh, osh, osh, osh],
        scratch_shapes=[pltpu.VMEM((2, w // LANES, CHUNK, LANES), F32)],
        compiler_params=_cparams(("parallel", "arbitrary")),
        name="rwkv7_scan",
    )(p, p, p, p, p, p, p, p, w0, w2, a0, a2, kk, ka, rk, blk)


GLA_CPS = 4


def _gla_gates(d, g_ref, gw_ref, gb_ref, cum_ref):
    c = CHUNK
    xg = _mm(g_ref[...], gw_ref[d]) + gb_ref[d]
    g = -_softplus(-xg) / GLA_TAU
    return [_mm_ones(g[j * c:(j + 1) * c], cum_ref[d], left=True)
            for j in range(g.shape[0] // c)]


def _gla_kernel(gf, qf, kf, vf, gb_, qb, kb, vb, gw, gbias, cum, of, ob, st_scr):
    @pl.when(pl.program_id(1) == 0)
    def _():
        st_scr[...] = jnp.zeros(st_scr.shape, F32)

    c = CHUNK
    n_blk = qf.shape[0] // c
    n_pair = GLA_K_WIDTH // LANES
    o_refs = [of, ob]
    cds = [_gla_gates(0, gf, gw, gbias, cum), _gla_gates(1, gb_, gw, gbias, cum)]
    qs = [qf[...] * (GLA_DK ** -0.5), qb[...] * (GLA_DK ** -0.5)]
    ks = [kf[...], kb[...]]
    vs = [vf[...], vb[...]]

    row = lax.broadcasted_iota(jnp.int32, (c, LANES), 0)
    lane = lax.broadcasted_iota(jnp.int32, (c, LANES), 1)
    s_idx = lane & (GLA_DK - 1)
    m0 = lane < GLA_DK
    diag = row == s_idx
    wide_row = lax.broadcasted_iota(jnp.int32, (c, GLA_K_WIDTH), 0)
    wide_prog = [wide_row, c - 1 - wide_row]

    chains = [(d, p, j) for j in range(n_blk) for p in range(n_pair) for d in (0, 1)]

    def blk_rows(x, j):
        return x[j * c:(j + 1) * c]

    def pair(x, p):
        return x[:, p * LANES:(p + 1) * LANES]

    def nt(x, y):
        return _mm(x, _blockdiag2(y, m0), 1, 1)

    qb_ = {(d, j): blk_rows(qs[d], j) for d in (0, 1) for j in range(n_blk)}
    kb_ = {(d, j): blk_rows(ks[d], j) for d in (0, 1) for j in range(n_blk)}
    a = [jnp.where(diag, nt(pair(qb_[(d, j)], p), pair(kb_[(d, j)], p)), 0.0) for d, p, j in chains]
    s = 1
    lvl = 1
    while s < c:
        same = (row // (2 * s)) == (s_idx // (2 * s))
        ql, kl = {}, {}
        for d in (0, 1):
            upper = (wide_prog[d] & (2 * s - 1)) >= s
            for j in range(n_blk):
                e = jnp.exp(-jnp.abs(cds[d][j][lvl * c:(lvl + 1) * c]))
                ql[(d, j)] = qb_[(d, j)] * jnp.where(upper, e, 0.0)
                kl[(d, j)] = kb_[(d, j)] * jnp.where(upper, 0.0, e)
        upd = [jnp.where(same, nt(pair(ql[(d, j)], p), pair(kl[(d, j)], p)), 0.0)
               for d, p, j in chains]
        a = [x + y for x, y in zip(a, upd)]
        s *= 2
        lvl += 1

    qt, kh, gc = {}, {}, {}
    for d in (0, 1):
        for j in range(n_blk):
            bc = cds[d][j][0:c]
            tot = bc[0:1, :] if d == 1 else bc[c - 1:c, :]
            qt[(d, j)] = qb_[(d, j)] * jnp.exp(bc)
            kh[(d, j)] = kb_[(d, j)] * jnp.exp(tot - bc)
            gc[(d, j)] = jnp.exp(tot)

    def vhead(d, j, h):
        return vs[d][j * c:(j + 1) * c, h * GLA_DV:(h + 1) * GLA_DV]

    oa = {key: _mm(_blockdiag2(a[i], m0),
                   jnp.concatenate([vhead(key[0], key[2], 2 * key[1]),
                                    vhead(key[0], key[2], 2 * key[1] + 1)], axis=0))
          for i, key in enumerate(chains)}
    z = {(d, p, j): _mm(jnp.concatenate([vhead(d, j, 2 * p), vhead(d, j, 2 * p + 1)], axis=1),
                        pair(kh[(d, j)], p), 0, 0)
         for d, p, j in chains}
    mv = lax.broadcasted_iota(jnp.int32, (GLA_DV, LANES), 1) < GLA_DK
    state = {(d, p): st_scr[d, p] for p in range(n_pair) for d in (0, 1)}
    for step in range(n_blk):
        for p in range(n_pair):
            for d in (0, 1):
                j = n_blk - 1 - step if d == 1 else step
                st = state[(d, p)]
                o = _mm(_blockdiag2(pair(qt[(d, j)], p), m0), st, 1, 1) + oa[(d, p, j)]
                o_refs[d][j * c:(j + 1) * c, (2 * p) * GLA_DV:(2 * p + 1) * GLA_DV] = o[:c]
                o_refs[d][j * c:(j + 1) * c, (2 * p + 1) * GLA_DV:(2 * p + 2) * GLA_DV] = o[c:]
                zz = z[(d, p, j)]
                state[(d, p)] = (st * pair(gc[(d, j)], p) + jnp.where(mv, zz[:GLA_DV], 0.0)
                                 + jnp.where(mv, 0.0, zz[GLA_DV:]))
    for (d, p), st in state.items():
        st_scr[d, p] = st


def _gla_cumsum_matrices():
    c = CHUNK
    i = np.arange(c)
    out = []
    for rev in (False, True):
        tri = (i[None, :] >= i[:, None]) if rev else (i[None, :] <= i[:, None])
        tri = tri.astype(np.float32)
        mats = [tri]
        s = 1
        while s < c:
            ref = (i // (2 * s)) * (2 * s) + (s if rev else s - 1)
            mats.append(tri - tri[ref])
            s *= 2
        out.append(np.concatenate(mats, axis=0))
    return jnp.asarray(np.stack(out), dtype=BF16)


def _gla_scan(p, batch, seq_len, gw, gbias):
    n = p.shape[0]
    rows = GLA_CPS * CHUNK
    assert seq_len % rows == 0
    nc = seq_len // rows

    def fwd(col, width):
        return pl.BlockSpec((rows, width), lambda b, c: (b * nc + c, col // width))

    def bwd(col, width):
        return pl.BlockSpec((rows, width), lambda b, c: (b * nc + nc - 1 - c, col // width))

    def full(a):
        return pl.BlockSpec(a.shape, lambda b, c: (0,) * a.ndim)

    kw, vw = GLA_K_WIDTH, GLA_V_WIDTH
    cum = _gla_cumsum_matrices()
    in_specs = [fwd(COL_GG, LANES), fwd(COL_GQ, kw), fwd(COL_GK, kw), fwd(COL_GV, vw),
                bwd(COL_GG, LANES), bwd(COL_GQ, kw), bwd(COL_GK, kw), bwd(COL_GV, vw),
                full(gw), full(gbias), full(cum)]
    ospec_f = pl.BlockSpec((rows, vw), lambda b, c: (b * nc + c, 0))
    ospec_b = pl.BlockSpec((rows, vw), lambda b, c: (b * nc + nc - 1 - c, 0))
    osh = jax.ShapeDtypeStruct((n, vw), F32)
    return pl.pallas_call(
        _gla_kernel,
        grid=(batch, nc),
        in_specs=in_specs,
        out_specs=[ospec_f, ospec_b],
        out_shape=[osh, osh],
        scratch_shapes=[pltpu.VMEM((2, kw // LANES, GLA_DV, LANES), F32)],
        compiler_params=_cparams(("parallel", "arbitrary")),
        name="gla_scan",
    )(p, p, p, p, p, p, p, p, gw, gbias, cum)


def _mix_out_kernel(x_ref, yf, yb, bdf, bdb, v_ref, gl_ref, of, ob, og_ref,
                    g2_ref, lnw_ref, lnb_ref, blk_ref, nw_ref, wo_ref, o_ref):
    blk = blk_ref[...]
    inv = 1.0 / RWKV_HEAD_DIM
    y = yf[...] + yb[...]
    mean = _mm_ones(y, blk) * inv
    yc = y - mean
    var = _mm_ones(yc * yc, blk) * inv
    yn = yc * lax.rsqrt(var + RWKV_GN_EPS) * lnw_ref[...] + lnb_ref[...]
    bonus = (bdf[...] + bdb[...]) * v_ref[...]
    gate = _mm(_sigmoid(gl_ref[...]), g2_ref[...])
    y_rw = (yn + bonus) * gate

    o = of[...] + ob[...]
    og = og_ref[...]
    gsil = og * _sigmoid(og)
    nw = nw_ref[...]
    acc = x_ref[...] + _mm(y_rw, wo_ref[0:RWKV_WIDTH, :])
    for h in range(GLA_HEADS):
        sl = slice(h * GLA_DV, (h + 1) * GLA_DV)
        oh = o[:, sl]
        ms = jnp.mean(oh * oh, axis=-1, keepdims=True)
        yg = oh * lax.rsqrt(ms + RMS_EPS) * nw[:, sl] * gsil[:, sl]
        acc = acc + _mm(yg, wo_ref[RWKV_WIDTH + h * GLA_DV:RWKV_WIDTH + (h + 1) * GLA_DV, :])
    o_ref[...] = acc


def _mix_out(xf, p, yf, yb, bdf, bdb, of, ob, g2, lnw, lnb, blk, nw, wo, tm=256):
    n, d = xf.shape
    w = RWKV_WIDTH

    def rows(width, col=0):
        return pl.BlockSpec((tm, width), lambda i: (i, col // width))

    def full(a):
        return pl.BlockSpec(a.shape, lambda i: (0,) * a.ndim)

    in_specs = [rows(d), rows(w), rows(w), rows(w), rows(w), rows(w, COL_V), rows(LANES, COL_GL),
                rows(w), rows(w), rows(w, COL_OG),
                full(g2), full(lnw), full(lnb), full(blk), full(nw), full(wo)]
    return pl.pallas_call(
        _mix_out_kernel,
        grid=(n // tm,),
        in_specs=in_specs,
        out_specs=rows(d),
        out_shape=jax.ShapeDtypeStruct((n, d), F32),
        compiler_params=_cparams(("parallel",)),
        name="mix_outproj",
    )(xf, yf, yb, bdf, bdb, p, p, of, ob, p, g2, lnw, lnb, blk, nw, wo)


def _router_kernel(x_ref, lnw_ref, wr_ref, br_ref, h_ref, idx_ref, wc_ref, cnt_ref, base):
    @pl.when(pl.program_id(0) == 0)
    def _():
        base[...] = jnp.zeros(base.shape, F32)

    h = _rmsnorm_rows(x_ref[...], lnw_ref[...])
    h_ref[...] = h
    logits = _mm_x3(wr_ref[...], h, 1, 1) + br_ref[...]
    tm = logits.shape[1]
    coarse = logits[0:N_GROUPS, :]
    fine = logits[8:8 + N_EXPERTS, :]
    rowg = lax.broadcasted_iota(jnp.int32, (N_GROUPS, tm), 0)
    cmax = jnp.max(coarse, axis=0, keepdims=True)
    gsel = jnp.min(jnp.where(coarse == cmax, rowg, N_GROUPS), axis=0, keepdims=True)
    pg = 1.0 / jnp.sum(jnp.exp(coarse - cmax), axis=0, keepdims=True)
    sel = jnp.zeros((EXPERTS_PER_GROUP, tm), F32)
    for g in range(N_GROUPS):
        sel = sel + jnp.where(gsel == g, fine[g * EXPERTS_PER_GROUP:(g + 1) * EXPERTS_PER_GROUP, :], 0.0)
    rowe = lax.broadcasted_iota(jnp.int32, (EXPERTS_PER_GROUP, tm), 0)
    l1 = jnp.max(sel, axis=0, keepdims=True)
    i1 = jnp.min(jnp.where(sel == l1, rowe, EXPERTS_PER_GROUP), axis=0, keepdims=True)
    sel2 = jnp.where(rowe == i1, -jnp.inf, sel)
    l2 = jnp.max(sel2, axis=0, keepdims=True)
    i2 = jnp.min(jnp.where(sel2 == l2, rowe, EXPERTS_PER_GROUP), axis=0, keepdims=True)
    t = jnp.exp(l2 - l1)
    w1 = pg / (1.0 + t)
    w2 = pg * t / (1.0 + t)
    e1 = gsel * EXPERTS_PER_GROUP + i1
    e2 = gsel * EXPERTS_PER_GROUP + i2
    rowx = lax.broadcasted_iota(jnp.int32, (N_EXPERTS, tm), 0)
    hit1 = rowx == e1
    hit2 = rowx == e2
    oh = jnp.where(hit1 | hit2, 1.0, 0.0)
    tr = lax.broadcasted_iota(jnp.int32, (tm, tm), 0)
    tc = lax.broadcasted_iota(jnp.int32, (tm, tm), 1)
    before = jnp.where(tr < tc, 1.0, 0.0).astype(BF16)
    pos = base[:, 0:1] + _mm(oh, before)
    r1 = jnp.sum(jnp.where(hit1, pos, 0.0), axis=0, keepdims=True)
    r2 = jnp.sum(jnp.where(hit2, pos, 0.0), axis=0, keepdims=True)
    new_base = base[...] + jnp.sum(oh, axis=1, keepdims=True)
    base[...] = new_base
    cnt_ref[...] = new_base
    zi = jnp.zeros((4, tm), jnp.int32)
    idx_ref[...] = jnp.concatenate([e1, e2, r1.astype(jnp.int32), r2.astype(jnp.int32), zi], axis=0)
    zf = jnp.zeros((6, tm), F32)
    wc_ref[...] = jnp.transpose(jnp.concatenate([w1, w2, zf], axis=0))


def _router(x1, lnw, wr, br, tm=256):
    n, d = x1.shape

    def full(a):
        return pl.BlockSpec(a.shape, lambda i: (0,) * a.ndim)

    return pl.pallas_call(
        _router_kernel,
        grid=(n // tm,),
        in_specs=[pl.BlockSpec((tm, d), lambda i: (i, 0)), full(lnw), full(wr), full(br)],
        out_specs=[pl.BlockSpec((tm, d), lambda i: (i, 0)),
                   pl.BlockSpec((8, tm), lambda i: (0, i)),
                   pl.BlockSpec((tm, 8), lambda i: (i, 0)),
                   pl.BlockSpec((N_EXPERTS, LANES), lambda i: (0, 0))],
        out_shape=[jax.ShapeDtypeStruct((n, d), F32),
                   jax.ShapeDtypeStruct((8, n), jnp.int32),
                   jax.ShapeDtypeStruct((n, 8), F32),
                   jax.ShapeDtypeStruct((N_EXPERTS, LANES), F32)],
        scratch_shapes=[pltpu.VMEM((N_EXPERTS, LANES), F32)],
        compiler_params=_cparams(("arbitrary",)),
        name="router",
    )(x1, lnw, wr, br)


MOE_ROWS = 256
DMA_UNROLL = 8


def _slots_kernel(idx_ref, ps_ref, o_ref):
    idx = idx_ref[...]
    tm = idx.shape[1]
    rowx = lax.broadcasted_iota(jnp.int32, (N_EXPERTS, tm), 0)
    ps = ps_ref[:, 0:1]
    out = []
    for k in range(2):
        start = jnp.sum(jnp.where(rowx == idx[k:k + 1], ps, 0), axis=0, keepdims=True)
        out.append(start + idx[2 + k:3 + k])
    o_ref[...] = jnp.concatenate(out + [jnp.zeros((6, tm), jnp.int32)], axis=0)


def _slots(idx, pstart, tm=2048):
    n = idx.shape[1]
    tm = min(tm, n)
    ps = jnp.broadcast_to(pstart[:, None], (N_EXPERTS, LANES))
    return pl.pallas_call(
        _slots_kernel,
        grid=(n // tm,),
        in_specs=[pl.BlockSpec((8, tm), lambda i: (0, i)),
                  pl.BlockSpec((N_EXPERTS, LANES), lambda i: (0, 0))],
        out_specs=pl.BlockSpec((8, tm), lambda i: (0, i)),
        out_shape=jax.ShapeDtypeStruct((8, n), jnp.int32),
        compiler_params=_cparams(("parallel",)),
        name="moe_slots",
    )(idx, ps)


def _dispatch_kernel(pstart_ref, pend_ref, idx_ref, h_ref, xs_out, zbuf, sem, zsem):
    tm = idx_ref.shape[1]

    @pl.when(pl.program_id(0) == 0)
    def _():
        zbuf[...] = jnp.zeros(zbuf.shape, F32)

        def zero_copy(e):
            first = pl.multiple_of(pend_ref[e] - MOE_ROWS, MOE_ROWS)
            return pltpu.make_async_copy(zbuf, xs_out.at[pl.ds(first, MOE_ROWS)], zsem)

        def zstart(e, carry):
            @pl.when(pend_ref[e] > pstart_ref[e])
            def _():
                zero_copy(e).start()
            return carry

        def zwait(e, carry):
            @pl.when(pend_ref[e] > pstart_ref[e])
            def _():
                zero_copy(e).wait()
            return carry

        lax.fori_loop(0, N_EXPERTS, zstart, 0)
        lax.fori_loop(0, N_EXPERTS, zwait, 0)

        def tail_copy(b):
            return pltpu.make_async_copy(
                zbuf, xs_out.at[pl.ds(pl.multiple_of(b * MOE_ROWS, MOE_ROWS), MOE_ROWS)], zsem)

        def tstart(b, carry):
            tail_copy(b).start()
            return carry

        def twait(b, carry):
            tail_copy(b).wait()
            return carry

        first_free = pend_ref[N_EXPERTS - 1] // MOE_ROWS
        n_blocks = xs_out.shape[0] // MOE_ROWS
        lax.fori_loop(first_free, n_blocks, tstart, 0)
        lax.fori_loop(first_free, n_blocks, twait, 0)

    def row_copy(t, k):
        return pltpu.make_async_copy(h_ref.at[pl.ds(t, 1)], xs_out.at[pl.ds(idx_ref[k, t], 1)], sem)

    def start(t, carry):
        row_copy(t, 0).start()
        row_copy(t, 1).start()
        return carry

    def wait(t, carry):
        row_copy(t, 0).wait()
        row_copy(t, 1).wait()
        return carry

    lax.fori_loop(0, tm, start, 0, unroll=DMA_UNROLL)
    lax.fori_loop(0, tm, wait, 0, unroll=DMA_UNROLL)


def _dispatch(pstart, pend, idx, h2, n_rows, tm=256):
    n, d = h2.shape
    return pl.pallas_call(
        _dispatch_kernel,
        grid_spec=pltpu.PrefetchScalarGridSpec(
            num_scalar_prefetch=2,
            grid=(n // tm,),
            in_specs=[pl.BlockSpec((8, tm), lambda i, ps, pe: (0, i), memory_space=pltpu.SMEM),
                      pl.BlockSpec((tm, d), lambda i, ps, pe: (i, 0))],
            out_specs=pl.BlockSpec(memory_space=pl.ANY),
            scratch_shapes=[pltpu.VMEM((MOE_ROWS, d), F32), pltpu.SemaphoreType.DMA(()),
                            pltpu.SemaphoreType.DMA(())]),
        out_shape=jax.ShapeDtypeStruct((n_rows, d), F32),
        compiler_params=_cparams(("arbitrary",)),
        name="moe_dispatch",
    )(pstart, pend, idx, h2)


def _expert_kernel(be_ref, nu_ref, x_ref, wg_ref, wu_ref, wd_ref, o_ref, wg_b, wu_b, wd_b):
    i = pl.program_id(0)
    used = i < nu_ref[0]
    new_expert = (i == 0) | (be_ref[i] != be_ref[jnp.maximum(i - 1, 0)])

    @pl.when(used & new_expert)
    def _():
        wg_b[...] = wg_ref[0].astype(BF16)
        wu_b[...] = wu_ref[0].astype(BF16)
        wd_b[...] = wd_ref[0].astype(BF16)

    @pl.when(used)
    def _():
        x = x_ref[...].astype(BF16)
        a = jnp.dot(x, wg_b[...], preferred_element_type=F32)
        u = jnp.dot(x, wu_b[...], preferred_element_type=F32)
        hh = (a * _sigmoid(a)) * u
        o_ref[...] = jnp.dot(hh.astype(BF16), wd_b[...], preferred_element_type=F32)

    @pl.when(i >= nu_ref[0])
    def _():
        o_ref[...] = jnp.zeros(o_ref.shape, F32)


def _experts(block_e, n_used, xs, wg, wu, wd):
    n_rows, d = xs.shape
    _, _, de = wg.shape
    return pl.pallas_call(
        _expert_kernel,
        grid_spec=pltpu.PrefetchScalarGridSpec(
            num_scalar_prefetch=2,
            grid=(n_rows // MOE_ROWS,),
            in_specs=[pl.BlockSpec((MOE_ROWS, d), lambda i, be, nu: (jnp.minimum(i, nu[0] - 1), 0)),
                      pl.BlockSpec((1, d, de), lambda i, be, nu: (be[i], 0, 0)),
                      pl.BlockSpec((1, d, de), lambda i, be, nu: (be[i], 0, 0)),
                      pl.BlockSpec((1, de, d), lambda i, be, nu: (be[i], 0, 0))],
            out_specs=pl.BlockSpec((MOE_ROWS, d), lambda i, be, nu: (i, 0)),
            scratch_shapes=[pltpu.VMEM((d, de), BF16), pltpu.VMEM((d, de), BF16),
                            pltpu.VMEM((de, d), BF16)]),
        out_shape=jax.ShapeDtypeStruct((n_rows, d), F32),
        compiler_params=_cparams(("arbitrary",)),
        name="moe_experts",
    )(block_e, n_used, xs, wg, wu, wd)


def _combine_kernel(idx_ref, ys_hbm, x_ref, w_ref, lnf_ref, o_ref, y1, y2, sem, *,
                    final_norm):
    tm = idx_ref.shape[1]

    def row_copy(t, k):
        buf = y1 if k == 0 else y2
        return pltpu.make_async_copy(ys_hbm.at[pl.ds(idx_ref[k, t], 1)], buf.at[pl.ds(t, 1)], sem)

    def start(t, carry):
        row_copy(t, 0).start()
        row_copy(t, 1).start()
        return carry

    def wait(t, carry):
        row_copy(t, 0).wait()
        row_copy(t, 1).wait()
        return carry

    lax.fori_loop(0, tm, start, 0, unroll=DMA_UNROLL)
    lax.fori_loop(0, tm, wait, 0, unroll=DMA_UNROLL)
    w = w_ref[...]
    y = x_ref[...] + w[:, 0:1] * y1[...] + w[:, 1:2] * y2[...]
    o_ref[...] = _rmsnorm_rows(y, lnf_ref[...]) if final_norm else y


def _combine(slots, ys, x1, wc, lnf, final_norm, tm=256):
    n, d = x1.shape
    return pl.pallas_call(
        functools.partial(_combine_kernel, final_norm=final_norm),
        grid=(n // tm,),
        in_specs=[pl.BlockSpec((8, tm), lambda i: (0, i), memory_space=pltpu.SMEM),
                  pl.BlockSpec(memory_space=pl.ANY),
                  pl.BlockSpec((tm, d), lambda i: (i, 0)),
                  pl.BlockSpec((tm, 8), lambda i: (i, 0)),
                  pl.BlockSpec((1, d), lambda i: (0, 0))],
        out_specs=pl.BlockSpec((tm, d), lambda i: (i, 0)),
        scratch_shapes=[pltpu.VMEM((tm, d), F32), pltpu.VMEM((tm, d), F32),
                        pltpu.SemaphoreType.DMA(())],
        out_shape=jax.ShapeDtypeStruct((n, d), F32),
        compiler_params=_cparams(("arbitrary",)),
        name="moe_combine",
    )(slots, ys, x1, wc, lnf)


def _moe(h2, idx, wc, counts, wg, wu, wd, x1, lnf, final_norm):
    n = h2.shape[0]
    n_rows = 2 * n + N_EXPERTS * MOE_ROWS
    cnt = counts[:, 0].astype(jnp.int32)
    padded = ((cnt + MOE_ROWS - 1) // MOE_ROWS) * MOE_ROWS
    pend = jnp.cumsum(padded)
    pstart = pend - padded
    blk_first = jnp.arange(n_rows // MOE_ROWS, dtype=jnp.int32) * MOE_ROWS
    block_e = jnp.minimum(jnp.sum(pend[None, :] <= blk_first[:, None], axis=1), N_EXPERTS - 1)
    n_used = (pend[-1:] // MOE_ROWS).astype(jnp.int32)
    slots = _slots(idx, pstart)
    xs = _dispatch(pstart, pend, slots, h2, n_rows)
    ys = _experts(block_e.astype(jnp.int32), n_used, xs, wg, wu, wd)
    return _combine(slots, ys, x1, wc, lnf, final_norm)


def _pad_rows(a, before, total):
    return jnp.pad(a, ((before, total - before - a.shape[0]), (0, 0)))


def _layer(xf, batch, seq_len, ln1_w, w_in, rw_mu, rw_w0_f, rw_w2_f, rw_w0_b, rw_w2_b, rw_a0_f,
           rw_a2_f, rw_a0_b, rw_a2_b, rw_g2, rw_k_k, rw_k_a, rw_r_k, rw_ln_w, rw_ln_b, gla_gw2_f,
           gla_gb_f, gla_gw2_b, gla_gb_b, gla_norm_w, w_out, ln2_w, moe_w_coarse, moe_b_coarse,
           moe_w_fine, moe_b_fine, moe_w_gate, moe_w_up, moe_w_down):
    d = xf.shape[1]
    rw_cols = 3 * RWKV_WIDTH + 4 * LORA + GATE_LORA
    w_rw, w_gla = w_in[:, :rw_cols], w_in[:, rw_cols:]
    o = 0
    w_gq = w_gla[:, o:o + GLA_K_WIDTH]; o += GLA_K_WIDTH
    w_gk = w_gla[:, o:o + GLA_K_WIDTH]; o += GLA_K_WIDTH
    w_gv = w_gla[:, o:o + GLA_V_WIDTH]; o += GLA_V_WIDTH
    w_gg = w_gla[:, o:o + 2 * GLA_GATE_LORA]; o += 2 * GLA_GATE_LORA
    w_og = w_gla[:, o:o + GLA_V_WIDTH]
    w_gg = jnp.pad(w_gg, ((0, 0), (0, LANES - 2 * GLA_GATE_LORA)))
    w_pad = jnp.concatenate([w_rw, w_gg, w_gq, w_gk, w_gv, w_og], axis=1).astype(BF16)
    mu_pad = jnp.pad(rw_mu, (0, NP - rw_cols))[None, :]

    p = _inproj(xf, ln1_w[None, :], w_pad, mu_pad, seq_len, rw_cols)

    head = lax.broadcasted_iota(jnp.int32, (RWKV_WIDTH, RWKV_WIDTH), 0) // RWKV_HEAD_DIM
    blk = (head == head.T).astype(BF16)
    w0 = jnp.stack([rw_w0_f, rw_w0_b])[:, None, :]
    a0 = jnp.stack([rw_a0_f, rw_a0_b])[:, None, :]
    w2 = jnp.stack([_pad_rows(rw_w2_f, 0, LANES), _pad_rows(rw_w2_b, LORA, LANES)]).astype(BF16)
    a2 = jnp.stack([_pad_rows(rw_a2_f, 0, LANES), _pad_rows(rw_a2_b, LORA, LANES)]).astype(BF16)
    yf, yb, bdf, bdb = _rwkv_scan(p, batch, seq_len, w0, w2, a0, a2, rw_k_k[None, :],
                                  rw_k_a[None, :], rw_r_k.reshape(1, -1), blk)

    gw = jnp.stack([_pad_rows(gla_gw2_f, 0, LANES),
                    _pad_rows(gla_gw2_b, GLA_GATE_LORA, LANES)]).astype(BF16)
    gbias = jnp.stack([gla_gb_f, gla_gb_b])[:, None, :]
    of, ob = _gla_scan(p, batch, seq_len, gw, gbias)

    x1 = _mix_out(xf, p, yf, yb, bdf, bdb, of, ob, rw_g2.astype(BF16), rw_ln_w[None, :],
                  rw_ln_b[None, :], blk, gla_norm_w[None, :], w_out.astype(BF16))

    wr = jnp.concatenate([moe_w_coarse.T, jnp.zeros((8 - N_GROUPS, d), F32), moe_w_fine.T], axis=0)
    br = jnp.concatenate([moe_b_coarse, jnp.zeros((8 - N_GROUPS,), F32), moe_b_fine])[:, None]
    h2, idx, wc, counts = _router(x1, ln2_w[None, :], wr, br)
    return h2, idx, wc, counts, x1


def kernel(x, ln1_w, w_in, rw_mu, rw_w0_f, rw_w2_f, rw_w0_b, rw_w2_b, rw_a0_f, rw_a2_f, rw_a0_b, rw_a2_b, rw_g2, rw_k_k, rw_k_a, rw_r_k, rw_ln_w, rw_ln_b, gla_gw2_f, gla_gb_f, gla_gw2_b, gla_gb_b, gla_norm_w, w_out, ln2_w, moe_w_coarse, moe_b_coarse, moe_w_fine, moe_b_fine, moe_w_gate, moe_w_up, moe_w_down, ln_f_w):
    batch, seq_len, d = x.shape
    xf = x.reshape(batch * seq_len, d)
    depth = w_in.shape[0]
    for l in range(depth):
        h2, idx, wc, counts, x1 = _layer(
            xf, batch, seq_len, ln1_w[l], w_in[l], rw_mu[l], rw_w0_f[l], rw_w2_f[l], rw_w0_b[l],
            rw_w2_b[l], rw_a0_f[l], rw_a2_f[l], rw_a0_b[l], rw_a2_b[l], rw_g2[l], rw_k_k[l],
            rw_k_a[l], rw_r_k[l], rw_ln_w[l], rw_ln_b[l], gla_gw2_f[l], gla_gb_f[l], gla_gw2_b[l],
            gla_gb_b[l], gla_norm_w[l], w_out[l], ln2_w[l], moe_w_coarse[l], moe_b_coarse[l],
            moe_w_fine[l], moe_b_fine[l], moe_w_gate[l], moe_w_up[l], moe_w_down[l])
        xf = _moe(h2, idx, wc, counts, moe_w_gate[l], moe_w_up[l], moe_w_down[l], x1,
                  ln_f_w[None, :], l == depth - 1)
    return xf.reshape(batch, seq_len, d)
```

```python
import functools

import numpy as np
import jax
import jax.numpy as jnp
from jax import lax
from jax.experimental import pallas as pl
from jax.experimental.pallas import tpu as pltpu

F32 = jnp.float32
BF16 = jnp.bfloat16

RMS_EPS = 1e-6
RWKV_GN_EPS = 64e-5
RWKV_WIDTH = 512
RWKV_HEAD_DIM = 64
LORA = 64
GATE_LORA = 128
GLA_HEADS = 4
GLA_DK = 64
GLA_DV = 128
GLA_K_WIDTH = GLA_HEADS * GLA_DK
GLA_V_WIDTH = GLA_HEADS * GLA_DV
GLA_GATE_LORA = 16
GLA_TAU = 16.0
N_GROUPS = 4
EXPERTS_PER_GROUP = 8
N_EXPERTS = N_GROUPS * EXPERTS_PER_GROUP

CHUNK = 64
LANES = 128

COL_R, COL_K, COL_V = 0, 512, 1024
COL_LORA = 1536
COL_GL = 1792
COL_GG = 1920
COL_GQ, COL_GK, COL_GV, COL_OG = 2048, 2304, 2560, 3072
NP = 3584

VMEM_LIMIT = 56 * 1024 * 1024


def _cparams(sem):
    return pltpu.CompilerParams(dimension_semantics=sem, vmem_limit_bytes=VMEM_LIMIT)


MM_DTYPE = BF16


def _mm(a, b, ca=1, cb=0):
    dt = MM_DTYPE
    return lax.dot_general(a.astype(dt), b.astype(dt), (((ca,), (cb,)), ((), ())),
                           preferred_element_type=F32)


def _split2(x):
    hi = x.astype(BF16)
    lo = (x - hi.astype(F32)).astype(BF16)
    return hi, lo


def _split3(x):
    hi = x.astype(BF16)
    r1 = x - hi.astype(F32)
    mid = r1.astype(BF16)
    lo = (r1 - mid.astype(F32)).astype(BF16)
    return hi, mid, lo


def _mm_ones(x, ones_bf16, left=False, terms=3):
    parts = _split3(x)[:terms]
    prods = [_mm(ones_bf16, t) if left else _mm(t, ones_bf16) for t in parts]
    out = prods[-1]
    for t in prods[-2::-1]:
        out = out + t
    return out


def _mm_x3(a, b, ca=1, cb=0):
    ah, al = _split2(a)
    bh, bl = _split2(b)
    return (_mm(al, bh, ca, cb) + _mm(ah, bl, ca, cb)) + _mm(ah, bh, ca, cb)


def _rmsnorm_rows(x, w):
    ms = jnp.mean(x * x, axis=-1, keepdims=True)
    return x * lax.rsqrt(ms + RMS_EPS) * w


def _softplus(x):
    return jnp.maximum(x, 0.0) + jnp.log(1.0 + jnp.exp(-jnp.abs(x)))


def _sigmoid(x):
    return 1.0 / (1.0 + jnp.exp(-x))


def _seg_cumsum(x, seg, rev):
    n = x.shape[0]
    pos = lax.broadcasted_iota(jnp.int32, x.shape, 0) & (seg - 1)
    sh = 1
    while sh < seg:
        if not rev:
            x = x + jnp.where(pos >= sh, pltpu.roll(x, sh, 0), 0.0)
        else:
            x = x + jnp.where(pos < seg - sh, pltpu.roll(x, n - sh, 0), 0.0)
        sh *= 2
    return x


def _blockdiag2(y, m0):
    return jnp.concatenate([jnp.where(m0, y, 0.0), jnp.where(m0, 0.0, y)], axis=0)


def _inproj_kernel(x_ref, xp_ref, xn_ref, lnw_ref, w_ref, mu_ref, o_ref, *, tiles_per_seq,
                   shift_cols, tn):
    i = pl.program_id(0)
    lnw = lnw_ref[...]
    h = _rmsnorm_rows(x_ref[...], lnw).astype(BF16)
    t = i % tiles_per_seq
    hp = jnp.where(t == 0, 0.0, _rmsnorm_rows(xp_ref[...], lnw))
    hn = jnp.where(t == tiles_per_seq - 1, 0.0, _rmsnorm_rows(xn_ref[...], lnw))
    hh = jnp.concatenate([hp, hn], axis=0).astype(BF16)
    tm = h.shape[0]
    npad = w_ref.shape[1]
    bounds = [(c0, min(c0 + tn, shift_cols)) for c0 in range(0, shift_cols, tn)]
    bounds += [(c0, min(c0 + tn, npad)) for c0 in range(shift_cols, npad, tn)]
    for c0, c1 in bounds:
        w = w_ref[:, c0:c1]
        p = jnp.dot(h, w, preferred_element_type=F32)
        if c0 < shift_cols:
            ph = jnp.dot(hh, w, preferred_element_type=F32)
            row = lax.broadcasted_iota(jnp.int32, p.shape, 0)
            prev = jnp.where(row == 0, ph[7:8, :], pltpu.roll(p, 1, 0))
            nxt = jnp.where(row == tm - 1, ph[8:9, :], pltpu.roll(p, tm - 1, 0))
            p = p + mu_ref[:, c0:c1] * (0.5 * (prev + nxt) - p)
        o_ref[:, c0:c1] = p


def _inproj(xf, lnw, w_pad, mu_pad, seq_len, shift_cols, tm=512, tn=640):
    n, d = xf.shape
    npad = w_pad.shape[1]
    assert seq_len % tm == 0 and shift_cols % LANES == 0 and npad % LANES == 0 and tn % LANES == 0
    tps = seq_len // tm
    nb8 = n // 8
    return pl.pallas_call(
        functools.partial(_inproj_kernel, tiles_per_seq=tps, shift_cols=shift_cols, tn=tn),
        grid=(n // tm,),
        in_specs=[
            pl.BlockSpec((tm, d), lambda i: (i, 0)),
            pl.BlockSpec((8, d), lambda i: (jnp.maximum(i * (tm // 8) - 1, 0), 0)),
            pl.BlockSpec((8, d), lambda i: (jnp.minimum((i + 1) * (tm // 8), nb8 - 1), 0)),
            pl.BlockSpec((1, d), lambda i: (0, 0)),
            pl.BlockSpec((d, npad), lambda i: (0, 0)),
            pl.BlockSpec((1, npad), lambda i: (0, 0)),
        ],
        out_specs=pl.BlockSpec((tm, npad), lambda i: (i, 0)),
        out_shape=jax.ShapeDtypeStruct((n, npad), F32),
        compiler_params=_cparams(("parallel",)),
        name="inproj_shift",
    )(xf, xf, xf, lnw, w_pad, mu_pad)


RWKV_CPS = 2


def _rwkv_prep(d, r_ref, k_ref, v_ref, l_ref, w0_ref, w2_ref, a0_ref, a2_ref,
               kk_ref, ka_ref, rk_ref, blk_ref, bd_ref):
    rev = d == 1
    c = CHUNK
    r = r_ref[...]
    k = k_ref[...]
    lo = l_ref[...]
    blk = blk_ref[...]

    zw = w0_ref[d] + _mm(jnp.tanh(lo[:, 0:LANES]), w2_ref[d])
    lw = -jnp.exp(-_softplus(-zw) - 0.5)
    a = _sigmoid(a0_ref[d] + _mm(lo[:, LANES:2 * LANES], a2_ref[d]))
    kkr = k * kk_ref[...]
    ss = _mm_ones(kkr * kkr, blk, terms=1)
    kk = kkr / jnp.maximum(jnp.sqrt(ss), 1e-12)
    kd = k * (1.0 + (a - 1.0) * ka_ref[...])
    b = kk * a
    bd_ref[...] = _mm_ones(r * kd * rk_ref[...], blk, terms=2)

    cum = _seg_cumsum(lw, c, rev)
    n_blk = r.shape[0] // c
    tot = [cum[j * c:j * c + 1, :] if rev else cum[j * c + c - 1:j * c + c, :] for j in range(n_blk)]
    ginv = jnp.exp(-cum)
    gend = jnp.concatenate([jnp.exp(tot[j] - cum[j * c:(j + 1) * c]) for j in range(n_blk)], axis=0)
    return dict(at=-kk * jnp.exp(cum - lw), rt=r * jnp.exp(cum), bt=b * ginv, kt=kd * ginv,
                bh=b * gend, kh=kd * gend, gc=[jnp.exp(t) for t in tot], v=v_ref[...])


def _rwkv_kernel(rf, kf, vf, lf, rb, kb, vb, lb, w0, w2, a0, a2, kk, ka, rk, blk,
                 yf, yb, bdf, bdb, m_scr):
    @pl.when(pl.program_id(1) == 0)
    def _():
        m_scr[...] = jnp.zeros(m_scr.shape, F32)

    c = CHUNK
    n_blk = rf.shape[0] // c
    n_pair = RWKV_WIDTH // LANES
    prep = [_rwkv_prep(0, rf, kf, vf, lf, w0, w2, a0, a2, kk, ka, rk, blk, bdf),
            _rwkv_prep(1, rb, kb, vb, lb, w0, w2, a0, a2, kk, ka, rk, blk, bdb)]
    y_refs = [yf, yb]

    row = lax.broadcasted_iota(jnp.int32, (c, LANES), 0)
    lane = lax.broadcasted_iota(jnp.int32, (c, LANES), 1)
    s_idx = lane & (RWKV_HEAD_DIM - 1)
    m0 = lane < RWKV_HEAD_DIM
    strict = [s_idx < row, s_idx > row]
    incl = [s_idx <= row, s_idx >= row]
    diag = s_idx == row

    def bd2(y):
        return _blockdiag2(y, m0)

    def fold(z):
        return jnp.where(m0, z[:c], 0.0) + jnp.where(m0, 0.0, z[c:])

    chains = [(d, p, j) for j in range(n_blk) for p in range(n_pair) for d in (0, 1)]
    ds = [d for d, _, _ in chains]

    def get(name):
        return [prep[d][name][j * c:(j + 1) * c, p * LANES:(p + 1) * LANES] for d, p, j in chains]

    at, rt, bt, kt, bh, kh, vp = (get(n) for n in ("at", "rt", "bt", "kt", "bh", "kh", "v"))
    gc = [prep[d]["gc"][j][:, p * LANES:(p + 1) * LANES] for d, p, j in chains]
    x2 = [jnp.concatenate([a_, r_], axis=0) for a_, r_ in zip(at, rt)]
    sbk = [_mm(x, jnp.concatenate([bd2(b_), bd2(k_)], axis=0), 1, 1) for x, b_, k_ in zip(x2, bt, kt)]
    sb = [z[:, :LANES] for z in sbk]
    sk = [z[:, LANES:] for z in sbk]
    lab = [jnp.where(strict[d], z[:c], 0.0) for d, z in zip(ds, sb)]
    prb = [jnp.where(incl[d], z[c:], 0.0) for d, z in zip(ds, sb)]
    lak = [jnp.where(strict[d], z[:c], 0.0) for d, z in zip(ds, sk)]
    prk = [jnp.where(incl[d], z[c:], 0.0) for d, z in zip(ds, sk)]
    bdv = [bd2(y) for y in vp]
    xa = at
    xv = [_mm(x, y) for x, y in zip(lak, bdv)]
    lp = lab
    n_it = 6
    for it in range(n_it):
        upd = [_mm(l_, jnp.concatenate([bd2(a_), bd2(v_)], axis=1)) for l_, a_, v_ in zip(lp, xa, xv)]
        xa = [a_ + u[:, :LANES] for a_, u in zip(xa, upd)]
        xv = [v_ + u[:, LANES:] for v_, u in zip(xv, upd)]
        if it < n_it - 1:
            lp = [_mm(l_, bd2(l_)) for l_ in lp]
    px = [_mm(x, jnp.concatenate([bd2(a_), bd2(v_)], axis=1)) for x, a_, v_ in zip(prb, xa, xv)]
    pv = [_mm(x, y) for x, y in zip(prk, bdv)]
    zb = [_mm(x, jnp.concatenate([a_, v_], axis=1), 0, 0) for x, a_, v_ in zip(bh, xa, xv)]
    zk = [_mm(x, y, 0, 0) for x, y in zip(kh, vp)]
    qg, yl, hh = {}, {}, {}
    for i, key in enumerate(chains):
        qp = rt[i] + px[i][:, :LANES]
        g = jnp.where(diag, gc[i], 0.0) + fold(zb[i][:, :LANES])
        qg[key] = jnp.concatenate([qp, g], axis=0)
        yl[key] = px[i][:, LANES:] + pv[i]
        hh[key] = fold(zb[i][:, LANES:]) + fold(zk[i])
    state = {(d, p): m_scr[d, p] for p in range(n_pair) for d in (0, 1)}
    for step in range(n_blk):
        for p in range(n_pair):
            for d in (0, 1):
                j = n_blk - 1 - step if d == 1 else step
                yg = _mm(qg[(d, p, j)], bd2(state[(d, p)]))
                y_refs[d][j * c:(j + 1) * c, p * LANES:(p + 1) * LANES] = yg[:c] + yl[(d, p, j)]
                state[(d, p)] = yg[c:] + hh[(d, p, j)]
    for (d, p), m in state.items():
        m_scr[d, p] = m


def _rwkv_scan(p, batch, seq_len, w0, w2, a0, a2, kk, ka, rk, blk):
    n = p.shape[0]
    rows = RWKV_CPS * CHUNK
    assert seq_len % rows == 0
    ns = seq_len // rows
    w = RWKV_WIDTH

    def fwd(col, width):
        return pl.BlockSpec((rows, width), lambda b, c: (b * ns + c, col // width))

    def bwd(col, width):
        return pl.BlockSpec((rows, width), lambda b, c: (b * ns + ns - 1 - c, col // width))

    def full(a):
        return pl.BlockSpec(a.shape, lambda b, c: (0,) * a.ndim)

    lora_w = 2 * LANES
    in_specs = [fwd(COL_R, w), fwd(COL_K, w), fwd(COL_V, w), fwd(COL_LORA, lora_w),
                bwd(COL_R, w), bwd(COL_K, w), bwd(COL_V, w), bwd(COL_LORA, lora_w),
                full(w0), full(w2), full(a0), full(a2), full(kk), full(ka), full(rk), full(blk)]
    ospec_f = pl.BlockSpec((rows, w), lambda b, c: (b * ns + c, 0))
    ospec_b = pl.BlockSpec((rows, w), lambda b, c: (b * ns + ns - 1 - c, 0))
    osh = jax.ShapeDtypeStruct((n, w), F32)
    return pl.pallas_call(
        _rwkv_kernel,
        grid=(batch, ns),
        in_specs=in_specs,
        out_specs=[ospec_f, ospec_b, ospec_f, ospec_b],
        out_shape=[osh, osh, osh, osh],
        scratch_shapes=[pltpu.VMEM((2, w // LANES, CHUNK, LANES), F32)],
        compiler_params=_cparams(("parallel", "arbitrary")),
        name="rwkv7_scan",
    )(p, p, p, p, p, p, p, p, w0, w2, a0, a2, kk, ka, rk, blk)


GLA_CPS = 4


def _gla_gates(d, g_ref, gw_ref, gb_ref, cum_ref):
    c = CHUNK
    xg = _mm(g_ref[...], gw_ref[d]) + gb_ref[d]
    g = -_softplus(-xg) / GLA_TAU
    return [_mm_ones(g[j * c:(j + 1) * c], cum_ref[d], left=True, terms=2)
            for j in range(g.shape[0] // c)]


def _gla_kernel(gf, qf, kf, vf, gb_, qb, kb, vb, gw, gbias, cum, of, ob, st_scr):
    @pl.when(pl.program_id(1) == 0)
    def _():
        st_scr[...] = jnp.zeros(st_scr.shape, F32)

    c = CHUNK
    n_blk = qf.shape[0] // c
    n_pair = GLA_K_WIDTH // LANES
    o_refs = [of, ob]
    cds = [_gla_gates(0, gf, gw, gbias, cum), _gla_gates(1, gb_, gw, gbias, cum)]
    qs = [qf[...] * (GLA_DK ** -0.5), qb[...] * (GLA_DK ** -0.5)]
    ks = [kf[...], kb[...]]
    vs = [vf[...], vb[...]]

    row = lax.broadcasted_iota(jnp.int32, (c, LANES), 0)
    lane = lax.broadcasted_iota(jnp.int32, (c, LANES), 1)
    s_idx = lane & (GLA_DK - 1)
    m0 = lane < GLA_DK
    diag = row == s_idx
    wide_row = lax.broadcasted_iota(jnp.int32, (c, GLA_K_WIDTH), 0)
    wide_prog = [wide_row, c - 1 - wide_row]

    chains = [(d, p, j) for j in range(n_blk) for p in range(n_pair) for d in (0, 1)]

    def blk_rows(x, j):
        return x[j * c:(j + 1) * c]

    def pair(x, p):
        return x[:, p * LANES:(p + 1) * LANES]

    def nt(x, y):
        return _mm(x, _blockdiag2(y, m0), 1, 1)

    qb_ = {(d, j): blk_rows(qs[d], j) for d in (0, 1) for j in range(n_blk)}
    kb_ = {(d, j): blk_rows(ks[d], j) for d in (0, 1) for j in range(n_blk)}
    a = [jnp.where(diag, nt(pair(qb_[(d, j)], p), pair(kb_[(d, j)], p)), 0.0) for d, p, j in chains]
    s = 1
    lvl = 1
    while s < c:
        same = (row // (2 * s)) == (s_idx // (2 * s))
        ql, kl = {}, {}
        for d in (0, 1):
            upper = (wide_prog[d] & (2 * s - 1)) >= s
            for j in range(n_blk):
                e = jnp.exp(-jnp.abs(cds[d][j][lvl * c:(lvl + 1) * c]))
                ql[(d, j)] = qb_[(d, j)] * jnp.where(upper, e, 0.0)
                kl[(d, j)] = kb_[(d, j)] * jnp.where(upper, 0.0, e)
        upd = [jnp.where(same, nt(pair(ql[(d, j)], p), pair(kl[(d, j)], p)), 0.0)
               for d, p, j in chains]
        a = [x + y for x, y in zip(a, upd)]
        s *= 2
        lvl += 1

    qt, kh, gc = {}, {}, {}
    for d in (0, 1):
        for j in range(n_blk):
            bc = cds[d][j][0:c]
            tot = bc[0:1, :] if d == 1 else bc[c - 1:c, :]
            qt[(d, j)] = qb_[(d, j)] * jnp.exp(bc)
            kh[(d, j)] = kb_[(d, j)] * jnp.exp(tot - bc)
            gc[(d, j)] = jnp.exp(tot)

    def vhead(d, j, h):
        return vs[d][j * c:(j + 1) * c, h * GLA_DV:(h + 1) * GLA_DV]

    oa = {key: _mm(_blockdiag2(a[i], m0),
                   jnp.concatenate([vhead(key[0], key[2], 2 * key[1]),
                                    vhead(key[0], key[2], 2 * key[1] + 1)], axis=0))
          for i, key in enumerate(chains)}
    z = {(d, p, j): _mm(jnp.concatenate([vhead(d, j, 2 * p), vhead(d, j, 2 * p + 1)], axis=1),
                        pair(kh[(d, j)], p), 0, 0)
         for d, p, j in chains}
    mv = lax.broadcasted_iota(jnp.int32, (GLA_DV, LANES), 1) < GLA_DK
    state = {(d, p): st_scr[d, p] for p in range(n_pair) for d in (0, 1)}
    for step in range(n_blk):
        for p in range(n_pair):
            for d in (0, 1):
                j = n_blk - 1 - step if d == 1 else step
                st = state[(d, p)]
                o = _mm(_blockdiag2(pair(qt[(d, j)], p), m0), st, 1, 1) + oa[(d, p, j)]
                o_refs[d][j * c:(j + 1) * c, (2 * p) * GLA_DV:(2 * p + 1) * GLA_DV] = o[:c]
                o_refs[d][j * c:(j + 1) * c, (2 * p + 1) * GLA_DV:(2 * p + 2) * GLA_DV] = o[c:]
                zz = z[(d, p, j)]
                state[(d, p)] = (st * pair(gc[(d, j)], p) + jnp.where(mv, zz[:GLA_DV], 0.0)
                                 + jnp.where(mv, 0.0, zz[GLA_DV:]))
    for (d, p), st in state.items():
        st_scr[d, p] = st


def _gla_cumsum_matrices():
    c = CHUNK
    i = np.arange(c)
    out = []
    for rev in (False, True):
        tri = (i[None, :] >= i[:, None]) if rev else (i[None, :] <= i[:, None])
        tri = tri.astype(np.float32)
        mats = [tri]
        s = 1
        while s < c:
            ref = (i // (2 * s)) * (2 * s) + (s if rev else s - 1)
            mats.append(tri - tri[ref])
            s *= 2
        out.append(np.concatenate(mats, axis=0))
    return jnp.asarray(np.stack(out), dtype=BF16)


def _gla_scan(p, batch, seq_len, gw, gbias):
    n = p.shape[0]
    rows = GLA_CPS * CHUNK
    assert seq_len % rows == 0
    nc = seq_len // rows

    def fwd(col, width):
        return pl.BlockSpec((rows, width), lambda b, c: (b * nc + c, col // width))

    def bwd(col, width):
        return pl.BlockSpec((rows, width), lambda b, c: (b * nc + nc - 1 - c, col // width))

    def full(a):
        return pl.BlockSpec(a.shape, lambda b, c: (0,) * a.ndim)

    kw, vw = GLA_K_WIDTH, GLA_V_WIDTH
    cum = _gla_cumsum_matrices()
    in_specs = [fwd(COL_GG, LANES), fwd(COL_GQ, kw), fwd(COL_GK, kw), fwd(COL_GV, vw),
                bwd(COL_GG, LANES), bwd(COL_GQ, kw), bwd(COL_GK, kw), bwd(COL_GV, vw),
                full(gw), full(gbias), full(cum)]
    ospec_f = pl.BlockSpec((rows, vw), lambda b, c: (b * nc + c, 0))
    ospec_b = pl.BlockSpec((rows, vw), lambda b, c: (b * nc + nc - 1 - c, 0))
    osh = jax.ShapeDtypeStruct((n, vw), F32)
    return pl.pallas_call(
        _gla_kernel,
        grid=(batch, nc),
        in_specs=in_specs,
        out_specs=[ospec_f, ospec_b],
        out_shape=[osh, osh],
        scratch_shapes=[pltpu.VMEM((2, kw // LANES, GLA_DV, LANES), F32)],
        compiler_params=_cparams(("parallel", "arbitrary")),
        name="gla_scan",
    )(p, p, p, p, p, p, p, p, gw, gbias, cum)


def _mix_out_kernel(x_ref, yf, yb, bdf, bdb, v_ref, gl_ref, of, ob, og_ref,
                    g2_ref, lnw_ref, lnb_ref, blk_ref, nw_ref, wo_ref, ln2_ref, wr_ref, br_ref,
                    o_ref, h_ref, idx_ref, wc_ref, cnt_ref, base):
    blk = blk_ref[...]
    inv = 1.0 / RWKV_HEAD_DIM
    y = yf[...] + yb[...]
    mean = _mm_ones(y, blk, terms=2) * inv
    yc = y - mean
    var = _mm_ones(yc * yc, blk, terms=1) * inv
    yn = yc * lax.rsqrt(var + RWKV_GN_EPS) * lnw_ref[...] + lnb_ref[...]
    bonus = (bdf[...] + bdb[...]) * v_ref[...]
    gate = _mm(_sigmoid(gl_ref[...]), g2_ref[...])
    y_rw = (yn + bonus) * gate

    o = of[...] + ob[...]
    og = og_ref[...]
    gsil = og * _sigmoid(og)
    nw = nw_ref[...]
    acc = x_ref[...] + _mm(y_rw, wo_ref[0:RWKV_WIDTH, :])
    for h in range(GLA_HEADS):
        sl = slice(h * GLA_DV, (h + 1) * GLA_DV)
        oh = o[:, sl]
        ms = jnp.mean(oh * oh, axis=-1, keepdims=True)
        yg = oh * lax.rsqrt(ms + RMS_EPS) * nw[:, sl] * gsil[:, sl]
        acc = acc + _mm(yg, wo_ref[RWKV_WIDTH + h * GLA_DV:RWKV_WIDTH + (h + 1) * GLA_DV, :])
    o_ref[...] = acc
    _route(acc, ln2_ref, wr_ref, br_ref, h_ref, idx_ref, wc_ref, cnt_ref, base)


def _mix_out(xf, p, yf, yb, bdf, bdb, of, ob, g2, lnw, lnb, blk, nw, wo, ln2, wr, br, tm=256):
    n, d = xf.shape
    w = RWKV_WIDTH

    def rows(width, col=0):
        return pl.BlockSpec((tm, width), lambda i: (i, col // width))

    def full(a):
        return pl.BlockSpec(a.shape, lambda i: (0,) * a.ndim)

    in_specs = [rows(d), rows(w), rows(w), rows(w), rows(w), rows(w, COL_V), rows(LANES, COL_GL),
                rows(w), rows(w), rows(w, COL_OG),
                full(g2), full(lnw), full(lnb), full(blk), full(nw), full(wo),
                full(ln2), full(wr), full(br)]
    return pl.pallas_call(
        _mix_out_kernel,
        grid=(n // tm,),
        in_specs=in_specs,
        out_specs=[rows(d), rows(d),
                   pl.BlockSpec((8, tm), lambda i: (0, i)),
                   pl.BlockSpec((tm, 8), lambda i: (i, 0)),
                   pl.BlockSpec((N_EXPERTS, LANES), lambda i: (0, 0))],
        out_shape=[jax.ShapeDtypeStruct((n, d), F32),
                   jax.ShapeDtypeStruct((n, d), F32),
                   jax.ShapeDtypeStruct((8, n), jnp.int32),
                   jax.ShapeDtypeStruct((n, 8), F32),
                   jax.ShapeDtypeStruct((N_EXPERTS, LANES), F32)],
        scratch_shapes=[pltpu.VMEM((N_EXPERTS, LANES), F32)],
        compiler_params=_cparams(("arbitrary",)),
        name="mix_outproj_router",
    )(xf, yf, yb, bdf, bdb, p, p, of, ob, p, g2, lnw, lnb, blk, nw, wo, ln2, wr, br)


def _route(x1, lnw_ref, wr_ref, br_ref, h_ref, idx_ref, wc_ref, cnt_ref, base):
    @pl.when(pl.program_id(0) == 0)
    def _():
        base[...] = jnp.zeros(base.shape, F32)

    h = _rmsnorm_rows(x1, lnw_ref[...])
    h_ref[...] = h
    logits = _mm_x3(wr_ref[...], h, 1, 1) + br_ref[...]
    tm = logits.shape[1]
    coarse = logits[0:N_GROUPS, :]
    fine = logits[8:8 + N_EXPERTS, :]
    rowg = lax.broadcasted_iota(jnp.int32, (N_GROUPS, tm), 0)
    cmax = jnp.max(coarse, axis=0, keepdims=True)
    gsel = jnp.min(jnp.where(coarse == cmax, rowg, N_GROUPS), axis=0, keepdims=True)
    pg = 1.0 / jnp.sum(jnp.exp(coarse - cmax), axis=0, keepdims=True)
    sel = jnp.zeros((EXPERTS_PER_GROUP, tm), F32)
    for g in range(N_GROUPS):
        sel = sel + jnp.where(gsel == g, fine[g * EXPERTS_PER_GROUP:(g + 1) * EXPERTS_PER_GROUP, :], 0.0)
    rowe = lax.broadcasted_iota(jnp.int32, (EXPERTS_PER_GROUP, tm), 0)
    l1 = jnp.max(sel, axis=0, keepdims=True)
    i1 = jnp.min(jnp.where(sel == l1, rowe, EXPERTS_PER_GROUP), axis=0, keepdims=True)
    sel2 = jnp.where(rowe == i1, -jnp.inf, sel)
    l2 = jnp.max(sel2, axis=0, keepdims=True)
    i2 = jnp.min(jnp.where(sel2 == l2, rowe, EXPERTS_PER_GROUP), axis=0, keepdims=True)
    t = jnp.exp(l2 - l1)
    w1 = pg / (1.0 + t)
    w2 = pg * t / (1.0 + t)
    e1 = gsel * EXPERTS_PER_GROUP + i1
    e2 = gsel * EXPERTS_PER_GROUP + i2
    rowx = lax.broadcasted_iota(jnp.int32, (N_EXPERTS, tm), 0)
    hit1 = rowx == e1
    hit2 = rowx == e2
    oh = jnp.where(hit1 | hit2, 1.0, 0.0)
    tr = lax.broadcasted_iota(jnp.int32, (tm, tm), 0)
    tc = lax.broadcasted_iota(jnp.int32, (tm, tm), 1)
    before = jnp.where(tr < tc, 1.0, 0.0).astype(BF16)
    pos = base[:, 0:1] + _mm(oh, before)
    r1 = jnp.sum(jnp.where(hit1, pos, 0.0), axis=0, keepdims=True)
    r2 = jnp.sum(jnp.where(hit2, pos, 0.0), axis=0, keepdims=True)
    new_base = base[...] + jnp.sum(oh, axis=1, keepdims=True)
    base[...] = new_base
    cnt_ref[...] = new_base
    zi = jnp.zeros((4, tm), jnp.int32)
    idx_ref[...] = jnp.concatenate([e1, e2, r1.astype(jnp.int32), r2.astype(jnp.int32), zi], axis=0)
    zf = jnp.zeros((6, tm), F32)
    wc_ref[...] = jnp.transpose(jnp.concatenate([w1, w2, zf], axis=0))


MOE_ROWS = 256
DMA_UNROLL = 8


def _slots_kernel(idx_ref, ps_ref, o_ref):
    idx = idx_ref[...]
    tm = idx.shape[1]
    rowx = lax.broadcasted_iota(jnp.int32, (N_EXPERTS, tm), 0)
    ps = ps_ref[:, 0:1]
    out = []
    for k in range(2):
        start = jnp.sum(jnp.where(rowx == idx[k:k + 1], ps, 0), axis=0, keepdims=True)
        out.append(start + idx[2 + k:3 + k])
    o_ref[...] = jnp.concatenate(out + [jnp.zeros((6, tm), jnp.int32)], axis=0)


def _slots(idx, pstart, tm=2048):
    n = idx.shape[1]
    tm = min(tm, n)
    ps = jnp.broadcast_to(pstart[:, None], (N_EXPERTS, LANES))
    return pl.pallas_call(
        _slots_kernel,
        grid=(n // tm,),
        in_specs=[pl.BlockSpec((8, tm), lambda i: (0, i)),
                  pl.BlockSpec((N_EXPERTS, LANES), lambda i: (0, 0))],
        out_specs=pl.BlockSpec((8, tm), lambda i: (0, i)),
        out_shape=jax.ShapeDtypeStruct((8, n), jnp.int32),
        compiler_params=_cparams(("parallel",)),
        name="moe_slots",
    )(idx, ps)


def _dispatch_kernel(pstart_ref, pend_ref, idx_ref, h_ref, xs_out, zbuf, sem, zsem):
    tm = idx_ref.shape[1]

    @pl.when(pl.program_id(0) == 0)
    def _():
        zbuf[...] = jnp.zeros(zbuf.shape, F32)

        def zero_copy(e):
            first = pl.multiple_of(pend_ref[e] - MOE_ROWS, MOE_ROWS)
            return pltpu.make_async_copy(zbuf, xs_out.at[pl.ds(first, MOE_ROWS)], zsem)

        def zstart(e, carry):
            @pl.when(pend_ref[e] > pstart_ref[e])
            def _():
                zero_copy(e).start()
            return carry

        def zwait(e, carry):
            @pl.when(pend_ref[e] > pstart_ref[e])
            def _():
                zero_copy(e).wait()
            return carry

        lax.fori_loop(0, N_EXPERTS, zstart, 0)
        lax.fori_loop(0, N_EXPERTS, zwait, 0)

        def tail_copy(b):
            return pltpu.make_async_copy(
                zbuf, xs_out.at[pl.ds(pl.multiple_of(b * MOE_ROWS, MOE_ROWS), MOE_ROWS)], zsem)

        def tstart(b, carry):
            tail_copy(b).start()
            return carry

        def twait(b, carry):
            tail_copy(b).wait()
            return carry

        first_free = pend_ref[N_EXPERTS - 1] // MOE_ROWS
        n_blocks = xs_out.shape[0] // MOE_ROWS
        lax.fori_loop(first_free, n_blocks, tstart, 0)
        lax.fori_loop(first_free, n_blocks, twait, 0)

    def row_copy(t, k):
        return pltpu.make_async_copy(h_ref.at[pl.ds(t, 1)], xs_out.at[pl.ds(idx_ref[k, t], 1)], sem)

    def start(t, carry):
        row_copy(t, 0).start()
        row_copy(t, 1).start()
        return carry

    def wait(t, carry):
        row_copy(t, 0).wait()
        row_copy(t, 1).wait()
        return carry

    lax.fori_loop(0, tm, start, 0, unroll=DMA_UNROLL)
    lax.fori_loop(0, tm, wait, 0, unroll=DMA_UNROLL)


def _dispatch(pstart, pend, idx, h2, n_rows, tm=512):
    n, d = h2.shape
    return pl.pallas_call(
        _dispatch_kernel,
        grid_spec=pltpu.PrefetchScalarGridSpec(
            num_scalar_prefetch=2,
            grid=(n // tm,),
            in_specs=[pl.BlockSpec((8, tm), lambda i, ps, pe: (0, i), memory_space=pltpu.SMEM),
                      pl.BlockSpec((tm, d), lambda i, ps, pe: (i, 0))],
            out_specs=pl.BlockSpec(memory_space=pl.ANY),
            scratch_shapes=[pltpu.VMEM((MOE_ROWS, d), F32), pltpu.SemaphoreType.DMA(()),
                            pltpu.SemaphoreType.DMA(())]),
        out_shape=jax.ShapeDtypeStruct((n_rows, d), F32),
        compiler_params=_cparams(("arbitrary",)),
        name="moe_dispatch",
    )(pstart, pend, idx, h2)


def _expert_kernel(be_ref, nu_ref, eslot_ref, enext_ref, x_ref, wg_hbm, wu_hbm, wd_hbm, o_ref,
                   wg_f, wu_f, wd_f, wg_b, wu_b, wd_b, sem):
    i = pl.program_id(0)
    used = i < nu_ref[0]
    e = be_ref[i]
    new_expert = (i == 0) | (e != be_ref[jnp.maximum(i - 1, 0)])

    def fetch(expert, slot):
        return [pltpu.make_async_copy(src.at[expert], dst.at[slot], sem.at[slot, k])
                for k, (src, dst) in enumerate(((wg_hbm, wg_f), (wu_hbm, wu_f), (wd_hbm, wd_f)))]

    @pl.when(i == 0)
    def _():
        for c in fetch(e, eslot_ref[e]):
            c.start()

    @pl.when(used & new_expert)
    def _():
        slot = eslot_ref[e]
        for c in fetch(e, slot):
            c.wait()
        nxt = enext_ref[e]

        @pl.when(nxt >= 0)
        def _():
            for c in fetch(nxt, 1 - slot):
                c.start()

        wg_b[...] = wg_f[slot].astype(BF16)
        wu_b[...] = wu_f[slot].astype(BF16)
        wd_b[...] = wd_f[slot].astype(BF16)

    @pl.when(used)
    def _():
        x = x_ref[...].astype(BF16)
        a = jnp.dot(x, wg_b[...], preferred_element_type=F32)
        u = jnp.dot(x, wu_b[...], preferred_element_type=F32)
        hh = (a * _sigmoid(a)) * u
        o_ref[...] = jnp.dot(hh.astype(BF16), wd_b[...], preferred_element_type=F32)

    @pl.when(i >= nu_ref[0])
    def _():
        o_ref[...] = jnp.zeros(o_ref.shape, F32)


def _experts(block_e, n_used, eslot, enext, xs, wg, wu, wd):
    n_rows, d = xs.shape
    _, _, de = wg.shape
    return pl.pallas_call(
        _expert_kernel,
        grid_spec=pltpu.PrefetchScalarGridSpec(
            num_scalar_prefetch=4,
            grid=(n_rows // MOE_ROWS,),
            in_specs=[pl.BlockSpec((MOE_ROWS, d), lambda i, be, nu, es, en: (jnp.minimum(i, nu[0] - 1), 0)),
                      pl.BlockSpec(memory_space=pl.ANY),
                      pl.BlockSpec(memory_space=pl.ANY),
                      pl.BlockSpec(memory_space=pl.ANY)],
            out_specs=pl.BlockSpec((MOE_ROWS, d), lambda i, be, nu, es, en: (i, 0)),
            scratch_shapes=[pltpu.VMEM((2, d, de), F32), pltpu.VMEM((2, d, de), F32),
                            pltpu.VMEM((2, de, d), F32),
                            pltpu.VMEM((d, de), BF16), pltpu.VMEM((d, de), BF16),
                            pltpu.VMEM((de, d), BF16), pltpu.SemaphoreType.DMA((2, 3))]),
        out_shape=jax.ShapeDtypeStruct((n_rows, d), F32),
        compiler_params=_cparams(("arbitrary",)),
        name="moe_experts",
    )(block_e, n_used, eslot, enext, xs, wg, wu, wd)


def _combine_kernel(idx_ref, ys_hbm, x_ref, w_ref, lnf_ref, o_ref, y1, y2, sem, *,
                    final_norm):
    tm = idx_ref.shape[1]

    def row_copy(t, k):
        buf = y1 if k == 0 else y2
        return pltpu.make_async_copy(ys_hbm.at[pl.ds(idx_ref[k, t], 1)], buf.at[pl.ds(t, 1)], sem)

    def start(t, carry):
        row_copy(t, 0).start()
        row_copy(t, 1).start()
        return carry

    def wait(t, carry):
        row_copy(t, 0).wait()
        row_copy(t, 1).wait()
        return carry

    lax.fori_loop(0, tm, start, 0, unroll=DMA_UNROLL)
    lax.fori_loop(0, tm, wait, 0, unroll=DMA_UNROLL)
    w = w_ref[...]
    y = x_ref[...] + w[:, 0:1] * y1[...] + w[:, 1:2] * y2[...]
    o_ref[...] = _rmsnorm_rows(y, lnf_ref[...]) if final_norm else y


def _combine(slots, ys, x1, wc, lnf, final_norm, tm=512):
    n, d = x1.shape
    return pl.pallas_call(
        functools.partial(_combine_kernel, final_norm=final_norm),
        grid=(n // tm,),
        in_specs=[pl.BlockSpec((8, tm), lambda i: (0, i), memory_space=pltpu.SMEM),
                  pl.BlockSpec(memory_space=pl.ANY),
                  pl.BlockSpec((tm, d), lambda i: (i, 0)),
                  pl.BlockSpec((tm, 8), lambda i: (i, 0)),
                  pl.BlockSpec((1, d), lambda i: (0, 0))],
        out_specs=pl.BlockSpec((tm, d), lambda i: (i, 0)),
        scratch_shapes=[pltpu.VMEM((tm, d), F32), pltpu.VMEM((tm, d), F32),
                        pltpu.SemaphoreType.DMA(())],
        out_shape=jax.ShapeDtypeStruct((n, d), F32),
        compiler_params=_cparams(("arbitrary",)),
        name="moe_combine",
    )(slots, ys, x1, wc, lnf)


def _moe(h2, idx, wc, counts, wg, wu, wd, x1, lnf, final_norm):
    n = h2.shape[0]
    n_rows = 2 * n + N_EXPERTS * MOE_ROWS
    cnt = counts[:, 0].astype(jnp.int32)
    padded = ((cnt + MOE_ROWS - 1) // MOE_ROWS) * MOE_ROWS
    pend = jnp.cumsum(padded)
    pstart = pend - padded
    blk_first = jnp.arange(n_rows // MOE_ROWS, dtype=jnp.int32) * MOE_ROWS
    block_e = jnp.minimum(jnp.sum(pend[None, :] <= blk_first[:, None], axis=1), N_EXPERTS - 1)
    n_used = (pend[-1:] // MOE_ROWS).astype(jnp.int32)
    has = padded > 0
    eslot = ((jnp.cumsum(has) - 1) % 2).astype(jnp.int32)
    ee = jnp.arange(N_EXPERTS, dtype=jnp.int32)
    later = jnp.where((ee[None, :] > ee[:, None]) & has[None, :], ee[None, :], N_EXPERTS)
    enext = jnp.min(later, axis=1)
    enext = jnp.where(enext == N_EXPERTS, -1, enext).astype(jnp.int32)
    slots = _slots(idx, pstart)
    xs = _dispatch(pstart, pend, slots, h2, n_rows)
    ys = _experts(block_e.astype(jnp.int32), n_used, eslot, enext, xs, wg, wu, wd)
    return _combine(slots, ys, x1, wc, lnf, final_norm)


def _pad_rows(a, before, total):
    return jnp.pad(a, ((before, total - before - a.shape[0]), (0, 0)))


def _layer(xf, batch, seq_len, ln1_w, w_in, rw_mu, rw_w0_f, rw_w2_f, rw_w0_b, rw_w2_b, rw_a0_f,
           rw_a2_f, rw_a0_b, rw_a2_b, rw_g2, rw_k_k, rw_k_a, rw_r_k, rw_ln_w, rw_ln_b, gla_gw2_f,
           gla_gb_f, gla_gw2_b, gla_gb_b, gla_norm_w, w_out, ln2_w, moe_w_coarse, moe_b_coarse,
           moe_w_fine, moe_b_fine, moe_w_gate, moe_w_up, moe_w_down):
    d = xf.shape[1]
    rw_cols = 3 * RWKV_WIDTH + 4 * LORA + GATE_LORA
    w_rw, w_gla = w_in[:, :rw_cols], w_in[:, rw_cols:]
    o = 0
    w_gq = w_gla[:, o:o + GLA_K_WIDTH]; o += GLA_K_WIDTH
    w_gk = w_gla[:, o:o + GLA_K_WIDTH]; o += GLA_K_WIDTH
    w_gv = w_gla[:, o:o + GLA_V_WIDTH]; o += GLA_V_WIDTH
    w_gg = w_gla[:, o:o + 2 * GLA_GATE_LORA]; o += 2 * GLA_GATE_LORA
    w_og = w_gla[:, o:o + GLA_V_WIDTH]
    w_gg = jnp.pad(w_gg, ((0, 0), (0, LANES - 2 * GLA_GATE_LORA)))
    w_pad = jnp.concatenate([w_rw, w_gg, w_gq, w_gk, w_gv, w_og], axis=1).astype(BF16)
    mu_pad = jnp.pad(rw_mu, (0, NP - rw_cols))[None, :]

    p = _inproj(xf, ln1_w[None, :], w_pad, mu_pad, seq_len, rw_cols)

    head = lax.broadcasted_iota(jnp.int32, (RWKV_WIDTH, RWKV_WIDTH), 0) // RWKV_HEAD_DIM
    blk = (head == head.T).astype(BF16)
    w0 = jnp.stack([rw_w0_f, rw_w0_b])[:, None, :]
    a0 = jnp.stack([rw_a0_f, rw_a0_b])[:, None, :]
    w2 = jnp.stack([_pad_rows(rw_w2_f, 0, LANES), _pad_rows(rw_w2_b, LORA, LANES)]).astype(BF16)
    a2 = jnp.stack([_pad_rows(rw_a2_f, 0, LANES), _pad_rows(rw_a2_b, LORA, LANES)]).astype(BF16)
    yf, yb, bdf, bdb = _rwkv_scan(p, batch, seq_len, w0, w2, a0, a2, rw_k_k[None, :],
                                  rw_k_a[None, :], rw_r_k.reshape(1, -1), blk)

    gw = jnp.stack([_pad_rows(gla_gw2_f, 0, LANES),
                    _pad_rows(gla_gw2_b, GLA_GATE_LORA, LANES)]).astype(BF16)
    gbias = jnp.stack([gla_gb_f, gla_gb_b])[:, None, :]
    of, ob = _gla_scan(p, batch, seq_len, gw, gbias)

    wr = jnp.concatenate([moe_w_coarse.T, jnp.zeros((8 - N_GROUPS, d), F32), moe_w_fine.T], axis=0)
    br = jnp.concatenate([moe_b_coarse, jnp.zeros((8 - N_GROUPS,), F32), moe_b_fine])[:, None]
    x1, h2, idx, wc, counts = _mix_out(
        xf, p, yf, yb, bdf, bdb, of, ob, rw_g2.astype(BF16), rw_ln_w[None, :], rw_ln_b[None, :], blk,
        gla_norm_w[None, :], w_out.astype(BF16), ln2_w[None, :], wr, br)
    return h2, idx, wc, counts, x1


def kernel(x, ln1_w, w_in, rw_mu, rw_w0_f, rw_w2_f, rw_w0_b, rw_w2_b, rw_a0_f, rw_a2_f, rw_a0_b, rw_a2_b, rw_g2, rw_k_k, rw_k_a, rw_r_k, rw_ln_w, rw_ln_b, gla_gw2_f, gla_gb_f, gla_gw2_b, gla_gb_b, gla_norm_w, w_out, ln2_w, moe_w_coarse, moe_b_coarse, moe_w_fine, moe_b_fine, moe_w_gate, moe_w_up, moe_w_down, ln_f_w):
    batch, seq_len, d = x.shape
    xf = x.reshape(batch * seq_len, d)
    depth = w_in.shape[0]
    for l in range(depth):
        h2, idx, wc, counts, x1 = _layer(
            xf, batch, seq_len, ln1_w[l], w_in[l], rw_mu[l], rw_w0_f[l], rw_w2_f[l], rw_w0_b[l],
            rw_w2_b[l], rw_a0_f[l], rw_a2_f[l], rw_a0_b[l], rw_a2_b[l], rw_g2[l], rw_k_k[l],
            rw_k_a[l], rw_r_k[l], rw_ln_w[l], rw_ln_b[l], gla_gw2_f[l], gla_gb_f[l], gla_gw2_b[l],
            gla_gb_b[l], gla_norm_w[l], w_out[l], ln2_w[l], moe_w_coarse[l], moe_b_coarse[l],
            moe_w_fine[l], moe_b_fine[l], moe_w_gate[l], moe_w_up[l], moe_w_down[l])
        xf = _moe(h2, idx, wc, counts, moe_w_gate[l], moe_w_up[l], moe_w_down[l], x1,
                  ln_f_w[None, :], l == depth - 1)
    return xf.reshape(batch, seq_len, d)
```

```python
import functools

import numpy as np
import jax
import jax.numpy as jnp
from jax import lax
from jax.experimental import pallas as pl
from jax.experimental.pallas import tpu as pltpu

F32 = jnp.float32
BF16 = jnp.bfloat16

RMS_EPS = 1e-6
RWKV_GN_EPS = 64e-5
RWKV_WIDTH = 512
RWKV_HEAD_DIM = 64
LORA = 64
GATE_LORA = 128
GLA_HEADS = 4
GLA_DK = 64
GLA_DV = 128
GLA_K_WIDTH = GLA_HEADS * GLA_DK
GLA_V_WIDTH = GLA_HEADS * GLA_DV
GLA_GATE_LORA = 16
GLA_TAU = 16.0
N_GROUPS = 4
EXPERTS_PER_GROUP = 8
N_EXPERTS = N_GROUPS * EXPERTS_PER_GROUP

CHUNK = 64
LANES = 128

COL_R, COL_K, COL_V = 0, 512, 1024
COL_LORA = 1536
COL_GL = 1792
COL_GG = 1920
COL_GQ, COL_GK, COL_GV, COL_OG = 2048, 2304, 2560, 3072
NP = 3584

VMEM_LIMIT = 56 * 1024 * 1024


def _cparams(sem):
    return pltpu.CompilerParams(dimension_semantics=sem, vmem_limit_bytes=VMEM_LIMIT)


MM_DTYPE = BF16


def _mm(a, b, ca=1, cb=0):
    dt = MM_DTYPE
    return lax.dot_general(a.astype(dt), b.astype(dt), (((ca,), (cb,)), ((), ())),
                           preferred_element_type=F32)


def _split2(x):
    hi = x.astype(BF16)
    lo = (x - hi.astype(F32)).astype(BF16)
    return hi, lo


def _split3(x):
    hi = x.astype(BF16)
    r1 = x - hi.astype(F32)
    mid = r1.astype(BF16)
    lo = (r1 - mid.astype(F32)).astype(BF16)
    return hi, mid, lo


def _mm_ones(x, ones_bf16, left=False, terms=3):
    parts = _split3(x)[:terms]
    prods = [_mm(ones_bf16, t) if left else _mm(t, ones_bf16) for t in parts]
    out = prods[-1]
    for t in prods[-2::-1]:
        out = out + t
    return out


def _mm_x3(a, b, ca=1, cb=0):
    ah, al = _split2(a)
    bh, bl = _split2(b)
    return (_mm(al, bh, ca, cb) + _mm(ah, bl, ca, cb)) + _mm(ah, bh, ca, cb)


def _pack_bf16_pairs(x):
    half = x.shape[1] // 2
    bits = lax.bitcast_convert_type(x.astype(BF16).astype(F32), jnp.uint32)
    return (bits[:, :half] >> 16) | (bits[:, half:] & jnp.uint32(0xFFFF0000))


def _unpack_bf16_pairs(w):
    lo = lax.bitcast_convert_type(w << 16, F32)
    hi = lax.bitcast_convert_type(w & jnp.uint32(0xFFFF0000), F32)
    return lo, hi


def _rmsnorm_rows(x, w):
    ms = jnp.mean(x * x, axis=-1, keepdims=True)
    return x * lax.rsqrt(ms + RMS_EPS) * w


def _softplus(x):
    return jnp.maximum(x, 0.0) + jnp.log(1.0 + jnp.exp(-jnp.abs(x)))


def _sigmoid(x):
    return 1.0 / (1.0 + jnp.exp(-x))


def _seg_cumsum(x, seg, rev):
    n = x.shape[0]
    pos = lax.broadcasted_iota(jnp.int32, x.shape, 0) & (seg - 1)
    sh = 1
    while sh < seg:
        if not rev:
            x = x + jnp.where(pos >= sh, pltpu.roll(x, sh, 0), 0.0)
        else:
            x = x + jnp.where(pos < seg - sh, pltpu.roll(x, n - sh, 0), 0.0)
        sh *= 2
    return x


def _blockdiag2(y, m0):
    return jnp.concatenate([jnp.where(m0, y, 0.0), jnp.where(m0, 0.0, y)], axis=0)


def _inproj_kernel(x_ref, xp_ref, xn_ref, lnw_ref, w_ref, mu_ref, o_ref, *, tiles_per_seq,
                   shift_cols, tn):
    i = pl.program_id(0)
    lnw = lnw_ref[...]
    h = _rmsnorm_rows(x_ref[...], lnw).astype(BF16)
    t = i % tiles_per_seq
    hp = jnp.where(t == 0, 0.0, _rmsnorm_rows(xp_ref[...], lnw))
    hn = jnp.where(t == tiles_per_seq - 1, 0.0, _rmsnorm_rows(xn_ref[...], lnw))
    hh = jnp.concatenate([hp, hn], axis=0).astype(BF16)
    tm = h.shape[0]
    npad = w_ref.shape[1]
    bounds = [(c0, min(c0 + tn, shift_cols)) for c0 in range(0, shift_cols, tn)]
    bounds += [(c0, min(c0 + tn, npad)) for c0 in range(shift_cols, npad, tn)]
    for c0, c1 in bounds:
        w = w_ref[:, c0:c1]
        p = jnp.dot(h, w, preferred_element_type=F32)
        if c0 < shift_cols:
            ph = jnp.dot(hh, w, preferred_element_type=F32)
            row = lax.broadcasted_iota(jnp.int32, p.shape, 0)
            prev = jnp.where(row == 0, ph[7:8, :], pltpu.roll(p, 1, 0))
            nxt = jnp.where(row == tm - 1, ph[8:9, :], pltpu.roll(p, tm - 1, 0))
            p = p + mu_ref[:, c0:c1] * (0.5 * (prev + nxt) - p)
        o_ref[:, c0:c1] = p


def _inproj(xf, lnw, w_pad, mu_pad, seq_len, shift_cols, tm=512, tn=640):
    n, d = xf.shape
    npad = w_pad.shape[1]
    assert seq_len % tm == 0 and shift_cols % LANES == 0 and npad % LANES == 0 and tn % LANES == 0
    tps = seq_len // tm
    nb8 = n // 8
    return pl.pallas_call(
        functools.partial(_inproj_kernel, tiles_per_seq=tps, shift_cols=shift_cols, tn=tn),
        grid=(n // tm,),
        in_specs=[
            pl.BlockSpec((tm, d), lambda i: (i, 0)),
            pl.BlockSpec((8, d), lambda i: (jnp.maximum(i * (tm // 8) - 1, 0), 0)),
            pl.BlockSpec((8, d), lambda i: (jnp.minimum((i + 1) * (tm // 8), nb8 - 1), 0)),
            pl.BlockSpec((1, d), lambda i: (0, 0)),
            pl.BlockSpec((d, npad), lambda i: (0, 0)),
            pl.BlockSpec((1, npad), lambda i: (0, 0)),
        ],
        out_specs=pl.BlockSpec((tm, npad), lambda i: (i, 0)),
        out_shape=jax.ShapeDtypeStruct((n, npad), F32),
        compiler_params=_cparams(("parallel",)),
        name="inproj_shift",
    )(xf, xf, xf, lnw, w_pad, mu_pad)


RWKV_CPS = 2


def _rwkv_prep(d, r_ref, k_ref, v_ref, l_ref, w0_ref, w2_ref, a0_ref, a2_ref,
               kk_ref, ka_ref, rk_ref, blk_ref, bd_ref):
    rev = d == 1
    c = CHUNK
    r = r_ref[...]
    k = k_ref[...]
    lo = l_ref[...]
    blk = blk_ref[...]

    zw = w0_ref[d] + _mm(jnp.tanh(lo[:, 0:LANES]), w2_ref[d])
    lw = -jnp.exp(-_softplus(-zw) - 0.5)
    a = _sigmoid(a0_ref[d] + _mm(lo[:, LANES:2 * LANES], a2_ref[d]))
    kkr = k * kk_ref[...]
    ss = _mm_ones(kkr * kkr, blk, terms=1)
    kk = kkr / jnp.maximum(jnp.sqrt(ss), 1e-12)
    kd = k * (1.0 + (a - 1.0) * ka_ref[...])
    b = kk * a
    bd_ref[...] = _mm_ones(r * kd * rk_ref[...], blk, terms=2)

    cum = _seg_cumsum(lw, c, rev)
    n_blk = r.shape[0] // c
    tot = [cum[j * c:j * c + 1, :] if rev else cum[j * c + c - 1:j * c + c, :] for j in range(n_blk)]
    ginv = jnp.exp(-cum)
    gend = jnp.concatenate([jnp.exp(tot[j] - cum[j * c:(j + 1) * c]) for j in range(n_blk)], axis=0)
    return dict(at=-kk * jnp.exp(cum - lw), rt=r * jnp.exp(cum), bt=b * ginv, kt=kd * ginv,
                bh=b * gend, kh=kd * gend, gc=[jnp.exp(t) for t in tot], v=v_ref[...])


def _rwkv_kernel(rf, kf, vf, lf, rb, kb, vb, lb, w0, w2, a0, a2, kk, ka, rk, blk,
                 yf, yb, bdf, bdb, m_scr):
    @pl.when(pl.program_id(1) == 0)
    def _():
        m_scr[...] = jnp.zeros(m_scr.shape, F32)

    c = CHUNK
    n_blk = rf.shape[0] // c
    n_pair = RWKV_WIDTH // LANES
    prep = [_rwkv_prep(0, rf, kf, vf, lf, w0, w2, a0, a2, kk, ka, rk, blk, bdf),
            _rwkv_prep(1, rb, kb, vb, lb, w0, w2, a0, a2, kk, ka, rk, blk, bdb)]
    y_refs = [yf, yb]

    row = lax.broadcasted_iota(jnp.int32, (c, LANES), 0)
    lane = lax.broadcasted_iota(jnp.int32, (c, LANES), 1)
    s_idx = lane & (RWKV_HEAD_DIM - 1)
    m0 = lane < RWKV_HEAD_DIM
    strict = [s_idx < row, s_idx > row]
    incl = [s_idx <= row, s_idx >= row]
    diag = s_idx == row

    def bd2(y):
        return _blockdiag2(y, m0)

    def fold(z):
        return jnp.where(m0, z[:c], 0.0) + jnp.where(m0, 0.0, z[c:])

    chains = [(d, p, j) for j in range(n_blk) for p in range(n_pair) for d in (0, 1)]
    ds = [d for d, _, _ in chains]

    def get(name):
        return [prep[d][name][j * c:(j + 1) * c, p * LANES:(p + 1) * LANES] for d, p, j in chains]

    at, rt, bt, kt, bh, kh, vp = (get(n) for n in ("at", "rt", "bt", "kt", "bh", "kh", "v"))
    gc = [prep[d]["gc"][j][:, p * LANES:(p + 1) * LANES] for d, p, j in chains]
    x2 = [jnp.concatenate([a_, r_], axis=0) for a_, r_ in zip(at, rt)]
    sbk = [_mm(x, jnp.concatenate([bd2(b_), bd2(k_)], axis=0), 1, 1) for x, b_, k_ in zip(x2, bt, kt)]
    sb = [z[:, :LANES] for z in sbk]
    sk = [z[:, LANES:] for z in sbk]
    lab = [jnp.where(strict[d], z[:c], 0.0) for d, z in zip(ds, sb)]
    prb = [jnp.where(incl[d], z[c:], 0.0) for d, z in zip(ds, sb)]
    lak = [jnp.where(strict[d], z[:c], 0.0) for d, z in zip(ds, sk)]
    prk = [jnp.where(incl[d], z[c:], 0.0) for d, z in zip(ds, sk)]
    bdv = [bd2(y) for y in vp]
    xa = at
    xv = [_mm(x, y) for x, y in zip(lak, bdv)]
    lp = lab
    n_it = 6
    for it in range(n_it):
        upd = [_mm(l_, jnp.concatenate([bd2(a_), bd2(v_)], axis=1)) for l_, a_, v_ in zip(lp, xa, xv)]
        xa = [a_ + u[:, :LANES] for a_, u in zip(xa, upd)]
        xv = [v_ + u[:, LANES:] for v_, u in zip(xv, upd)]
        if it < n_it - 1:
            lp = [_mm(l_, bd2(l_)) for l_ in lp]
    px = [_mm(x, jnp.concatenate([bd2(a_), bd2(v_)], axis=1)) for x, a_, v_ in zip(prb, xa, xv)]
    pv = [_mm(x, y) for x, y in zip(prk, bdv)]
    zb = [_mm(x, jnp.concatenate([a_, v_], axis=1), 0, 0) for x, a_, v_ in zip(bh, xa, xv)]
    zk = [_mm(x, y, 0, 0) for x, y in zip(kh, vp)]
    qg, yl, hh = {}, {}, {}
    for i, key in enumerate(chains):
        qp = rt[i] + px[i][:, :LANES]
        g = jnp.where(diag, gc[i], 0.0) + fold(zb[i][:, :LANES])
        qg[key] = jnp.concatenate([qp, g], axis=0)
        yl[key] = px[i][:, LANES:] + pv[i]
        hh[key] = fold(zb[i][:, LANES:]) + fold(zk[i])
    state = {(d, p): m_scr[d, p] for p in range(n_pair) for d in (0, 1)}
    for step in range(n_blk):
        for p in range(n_pair):
            for d in (0, 1):
                j = n_blk - 1 - step if d == 1 else step
                yg = _mm(qg[(d, p, j)], bd2(state[(d, p)]))
                y_refs[d][j * c:(j + 1) * c, p * LANES:(p + 1) * LANES] = yg[:c] + yl[(d, p, j)]
                state[(d, p)] = yg[c:] + hh[(d, p, j)]
    for (d, p), m in state.items():
        m_scr[d, p] = m


def _rwkv_scan(p, batch, seq_len, w0, w2, a0, a2, kk, ka, rk, blk):
    n = p.shape[0]
    rows = RWKV_CPS * CHUNK
    assert seq_len % rows == 0
    ns = seq_len // rows
    w = RWKV_WIDTH

    def fwd(col, width):
        return pl.BlockSpec((rows, width), lambda b, c: (b * ns + c, col // width))

    def bwd(col, width):
        return pl.BlockSpec((rows, width), lambda b, c: (b * ns + ns - 1 - c, col // width))

    def full(a):
        return pl.BlockSpec(a.shape, lambda b, c: (0,) * a.ndim)

    lora_w = 2 * LANES
    in_specs = [fwd(COL_R, w), fwd(COL_K, w), fwd(COL_V, w), fwd(COL_LORA, lora_w),
                bwd(COL_R, w), bwd(COL_K, w), bwd(COL_V, w), bwd(COL_LORA, lora_w),
                full(w0), full(w2), full(a0), full(a2), full(kk), full(ka), full(rk), full(blk)]
    ospec_f = pl.BlockSpec((rows, w), lambda b, c: (b * ns + c, 0))
    ospec_b = pl.BlockSpec((rows, w), lambda b, c: (b * ns + ns - 1 - c, 0))
    osh = jax.ShapeDtypeStruct((n, w), F32)
    return pl.pallas_call(
        _rwkv_kernel,
        grid=(batch, ns),
        in_specs=in_specs,
        out_specs=[ospec_f, ospec_b, ospec_f, ospec_b],
        out_shape=[osh, osh, osh, osh],
        scratch_shapes=[pltpu.VMEM((2, w // LANES, CHUNK, LANES), F32)],
        compiler_params=_cparams(("parallel", "arbitrary")),
        name="rwkv7_scan",
    )(p, p, p, p, p, p, p, p, w0, w2, a0, a2, kk, ka, rk, blk)


GLA_CPS = 4


def _gla_gates(d, g_ref, gw_ref, gb_ref, cum_ref):
    c = CHUNK
    xg = _mm(g_ref[...], gw_ref[d]) + gb_ref[d]
    g = -_softplus(-xg) / GLA_TAU
    return [_mm_ones(g[j * c:(j + 1) * c], cum_ref[d], left=True, terms=2)
            for j in range(g.shape[0] // c)]


def _gla_kernel(gf, qf, kf, vf, gb_, qb, kb, vb, gw, gbias, cum, of, ob, st_scr):
    @pl.when(pl.program_id(1) == 0)
    def _():
        st_scr[...] = jnp.zeros(st_scr.shape, F32)

    c = CHUNK
    n_blk = qf.shape[0] // c
    n_pair = GLA_K_WIDTH // LANES
    o_refs = [of, ob]
    cds = [_gla_gates(0, gf, gw, gbias, cum), _gla_gates(1, gb_, gw, gbias, cum)]
    qs = [qf[...] * (GLA_DK ** -0.5), qb[...] * (GLA_DK ** -0.5)]
    ks = [kf[...], kb[...]]
    vs = [vf[...], vb[...]]

    row = lax.broadcasted_iota(jnp.int32, (c, LANES), 0)
    lane = lax.broadcasted_iota(jnp.int32, (c, LANES), 1)
    s_idx = lane & (GLA_DK - 1)
    m0 = lane < GLA_DK
    diag = row == s_idx
    wide_row = lax.broadcasted_iota(jnp.int32, (c, GLA_K_WIDTH), 0)
    wide_prog = [wide_row, c - 1 - wide_row]

    chains = [(d, p, j) for j in range(n_blk) for p in range(n_pair) for d in (0, 1)]

    def blk_rows(x, j):
        return x[j * c:(j + 1) * c]

    def pair(x, p):
        return x[:, p * LANES:(p + 1) * LANES]

    def nt(x, y):
        return _mm(x, _blockdiag2(y, m0), 1, 1)

    qb_ = {(d, j): blk_rows(qs[d], j) for d in (0, 1) for j in range(n_blk)}
    kb_ = {(d, j): blk_rows(ks[d], j) for d in (0, 1) for j in range(n_blk)}
    a = [jnp.where(diag, nt(pair(qb_[(d, j)], p), pair(kb_[(d, j)], p)), 0.0) for d, p, j in chains]
    s = 1
    lvl = 1
    while s < c:
        same = (row // (2 * s)) == (s_idx // (2 * s))
        ql, kl = {}, {}
        for d in (0, 1):
            upper = (wide_prog[d] & (2 * s - 1)) >= s
            for j in range(n_blk):
                e = jnp.exp(-jnp.abs(cds[d][j][lvl * c:(lvl + 1) * c]))
                ql[(d, j)] = qb_[(d, j)] * jnp.where(upper, e, 0.0)
                kl[(d, j)] = kb_[(d, j)] * jnp.where(upper, 0.0, e)
        upd = [jnp.where(same, nt(pair(ql[(d, j)], p), pair(kl[(d, j)], p)), 0.0)
               for d, p, j in chains]
        a = [x + y for x, y in zip(a, upd)]
        s *= 2
        lvl += 1

    qt, kh, gc = {}, {}, {}
    for d in (0, 1):
        for j in range(n_blk):
            bc = cds[d][j][0:c]
            tot = bc[0:1, :] if d == 1 else bc[c - 1:c, :]
            qt[(d, j)] = qb_[(d, j)] * jnp.exp(bc)
            kh[(d, j)] = kb_[(d, j)] * jnp.exp(tot - bc)
            gc[(d, j)] = jnp.exp(tot)

    def vhead(d, j, h):
        return vs[d][j * c:(j + 1) * c, h * GLA_DV:(h + 1) * GLA_DV]

    oa = {key: _mm(_blockdiag2(a[i], m0),
                   jnp.concatenate([vhead(key[0], key[2], 2 * key[1]),
                                    vhead(key[0], key[2], 2 * key[1] + 1)], axis=0))
          for i, key in enumerate(chains)}
    z = {(d, p, j): _mm(jnp.concatenate([vhead(d, j, 2 * p), vhead(d, j, 2 * p + 1)], axis=1),
                        pair(kh[(d, j)], p), 0, 0)
         for d, p, j in chains}
    mv = lax.broadcasted_iota(jnp.int32, (GLA_DV, LANES), 1) < GLA_DK
    state = {(d, p): st_scr[d, p] for p in range(n_pair) for d in (0, 1)}
    for step in range(n_blk):
        for p in range(n_pair):
            for d in (0, 1):
                j = n_blk - 1 - step if d == 1 else step
                st = state[(d, p)]
                o = _mm(_blockdiag2(pair(qt[(d, j)], p), m0), st, 1, 1) + oa[(d, p, j)]
                o_refs[d][j * c:(j + 1) * c, (2 * p) * GLA_DV:(2 * p + 1) * GLA_DV] = o[:c]
                o_refs[d][j * c:(j + 1) * c, (2 * p + 1) * GLA_DV:(2 * p + 2) * GLA_DV] = o[c:]
                zz = z[(d, p, j)]
                state[(d, p)] = (st * pair(gc[(d, j)], p) + jnp.where(mv, zz[:GLA_DV], 0.0)
                                 + jnp.where(mv, 0.0, zz[GLA_DV:]))
    for (d, p), st in state.items():
        st_scr[d, p] = st


def _gla_cumsum_matrices():
    c = CHUNK
    i = np.arange(c)
    out = []
    for rev in (False, True):
        tri = (i[None, :] >= i[:, None]) if rev else (i[None, :] <= i[:, None])
        tri = tri.astype(np.float32)
        mats = [tri]
        s = 1
        while s < c:
            ref = (i // (2 * s)) * (2 * s) + (s if rev else s - 1)
            mats.append(tri - tri[ref])
            s *= 2
        out.append(np.concatenate(mats, axis=0))
    return jnp.asarray(np.stack(out), dtype=BF16)


def _gla_scan(p, batch, seq_len, gw, gbias):
    n = p.shape[0]
    rows = GLA_CPS * CHUNK
    assert seq_len % rows == 0
    nc = seq_len // rows

    def fwd(col, width):
        return pl.BlockSpec((rows, width), lambda b, c: (b * nc + c, col // width))

    def bwd(col, width):
        return pl.BlockSpec((rows, width), lambda b, c: (b * nc + nc - 1 - c, col // width))

    def full(a):
        return pl.BlockSpec(a.shape, lambda b, c: (0,) * a.ndim)

    kw, vw = GLA_K_WIDTH, GLA_V_WIDTH
    cum = _gla_cumsum_matrices()
    in_specs = [fwd(COL_GG, LANES), fwd(COL_GQ, kw), fwd(COL_GK, kw), fwd(COL_GV, vw),
                bwd(COL_GG, LANES), bwd(COL_GQ, kw), bwd(COL_GK, kw), bwd(COL_GV, vw),
                full(gw), full(gbias), full(cum)]
    ospec_f = pl.BlockSpec((rows, vw), lambda b, c: (b * nc + c, 0))
    ospec_b = pl.BlockSpec((rows, vw), lambda b, c: (b * nc + nc - 1 - c, 0))
    osh = jax.ShapeDtypeStruct((n, vw), F32)
    return pl.pallas_call(
        _gla_kernel,
        grid=(batch, nc),
        in_specs=in_specs,
        out_specs=[ospec_f, ospec_b],
        out_shape=[osh, osh],
        scratch_shapes=[pltpu.VMEM((2, kw // LANES, GLA_DV, LANES), F32)],
        compiler_params=_cparams(("parallel", "arbitrary")),
        name="gla_scan",
    )(p, p, p, p, p, p, p, p, gw, gbias, cum)


def _mix_out_kernel(x_ref, yf, yb, bdf, bdb, v_ref, gl_ref, of, ob, og_ref,
                    g2_ref, lnw_ref, lnb_ref, blk_ref, nw_ref, wo_ref, ln2_ref, wr_ref, br_ref,
                    o_ref, h_ref, idx_ref, wc_ref, cnt_ref, base):
    blk = blk_ref[...]
    inv = 1.0 / RWKV_HEAD_DIM
    y = yf[...] + yb[...]
    mean = _mm_ones(y, blk, terms=2) * inv
    yc = y - mean
    var = _mm_ones(yc * yc, blk, terms=1) * inv
    yn = yc * lax.rsqrt(var + RWKV_GN_EPS) * lnw_ref[...] + lnb_ref[...]
    bonus = (bdf[...] + bdb[...]) * v_ref[...]
    gate = _mm(_sigmoid(gl_ref[...]), g2_ref[...])
    y_rw = (yn + bonus) * gate

    o = of[...] + ob[...]
    og = og_ref[...]
    gsil = og * _sigmoid(og)
    nw = nw_ref[...]
    acc = x_ref[...] + _mm(y_rw, wo_ref[0:RWKV_WIDTH, :])
    for h in range(GLA_HEADS):
        sl = slice(h * GLA_DV, (h + 1) * GLA_DV)
        oh = o[:, sl]
        ms = jnp.mean(oh * oh, axis=-1, keepdims=True)
        yg = oh * lax.rsqrt(ms + RMS_EPS) * nw[:, sl] * gsil[:, sl]
        acc = acc + _mm(yg, wo_ref[RWKV_WIDTH + h * GLA_DV:RWKV_WIDTH + (h + 1) * GLA_DV, :])
    o_ref[...] = acc
    _route(acc, ln2_ref, wr_ref, br_ref, h_ref, idx_ref, wc_ref, cnt_ref, base)


def _mix_out(xf, p, yf, yb, bdf, bdb, of, ob, g2, lnw, lnb, blk, nw, wo, ln2, wr, br, tm=256):
    n, d = xf.shape
    w = RWKV_WIDTH

    def rows(width, col=0):
        return pl.BlockSpec((tm, width), lambda i: (i, col // width))

    def full(a):
        return pl.BlockSpec(a.shape, lambda i: (0,) * a.ndim)

    in_specs = [rows(d), rows(w), rows(w), rows(w), rows(w), rows(w, COL_V), rows(LANES, COL_GL),
                rows(w), rows(w), rows(w, COL_OG),
                full(g2), full(lnw), full(lnb), full(blk), full(nw), full(wo),
                full(ln2), full(wr), full(br)]
    return pl.pallas_call(
        _mix_out_kernel,
        grid=(n // tm,),
        in_specs=in_specs,
        out_specs=[rows(d), rows(d // 2),
                   pl.BlockSpec((8, tm), lambda i: (0, i)),
                   pl.BlockSpec((tm, 8), lambda i: (i, 0)),
                   pl.BlockSpec((N_EXPERTS, LANES), lambda i: (0, 0))],
        out_shape=[jax.ShapeDtypeStruct((n, d), F32),
                   jax.ShapeDtypeStruct((n, d // 2), jnp.uint32),
                   jax.ShapeDtypeStruct((8, n), jnp.int32),
                   jax.ShapeDtypeStruct((n, 8), F32),
                   jax.ShapeDtypeStruct((N_EXPERTS, LANES), F32)],
        scratch_shapes=[pltpu.VMEM((N_EXPERTS, LANES), F32)],
        compiler_params=_cparams(("arbitrary",)),
        name="mix_outproj_router",
    )(xf, yf, yb, bdf, bdb, p, p, of, ob, p, g2, lnw, lnb, blk, nw, wo, ln2, wr, br)


def _route(x1, lnw_ref, wr_ref, br_ref, h_ref, idx_ref, wc_ref, cnt_ref, base):
    @pl.when(pl.program_id(0) == 0)
    def _():
        base[...] = jnp.zeros(base.shape, F32)

    h = _rmsnorm_rows(x1, lnw_ref[...])
    h_ref[...] = _pack_bf16_pairs(h)
    logits = _mm_x3(wr_ref[...], h, 1, 1) + br_ref[...]
    tm = logits.shape[1]
    coarse = logits[0:N_GROUPS, :]
    fine = logits[8:8 + N_EXPERTS, :]
    rowg = lax.broadcasted_iota(jnp.int32, (N_GROUPS, tm), 0)
    cmax = jnp.max(coarse, axis=0, keepdims=True)
    gsel = jnp.min(jnp.where(coarse == cmax, rowg, N_GROUPS), axis=0, keepdims=True)
    pg = 1.0 / jnp.sum(jnp.exp(coarse - cmax), axis=0, keepdims=True)
    sel = jnp.zeros((EXPERTS_PER_GROUP, tm), F32)
    for g in range(N_GROUPS):
        sel = sel + jnp.where(gsel == g, fine[g * EXPERTS_PER_GROUP:(g + 1) * EXPERTS_PER_GROUP, :], 0.0)
    rowe = lax.broadcasted_iota(jnp.int32, (EXPERTS_PER_GROUP, tm), 0)
    l1 = jnp.max(sel, axis=0, keepdims=True)
    i1 = jnp.min(jnp.where(sel == l1, rowe, EXPERTS_PER_GROUP), axis=0, keepdims=True)
    sel2 = jnp.where(rowe == i1, -jnp.inf, sel)
    l2 = jnp.max(sel2, axis=0, keepdims=True)
    i2 = jnp.min(jnp.where(sel2 == l2, rowe, EXPERTS_PER_GROUP), axis=0, keepdims=True)
    t = jnp.exp(l2 - l1)
    w1 = pg / (1.0 + t)
    w2 = pg * t / (1.0 + t)
    e1 = gsel * EXPERTS_PER_GROUP + i1
    e2 = gsel * EXPERTS_PER_GROUP + i2
    rowx = lax.broadcasted_iota(jnp.int32, (N_EXPERTS, tm), 0)
    hit1 = rowx == e1
    hit2 = rowx == e2
    oh = jnp.where(hit1 | hit2, 1.0, 0.0)
    tr = lax.broadcasted_iota(jnp.int32, (tm, tm), 0)
    tc = lax.broadcasted_iota(jnp.int32, (tm, tm), 1)
    before = jnp.where(tr < tc, 1.0, 0.0).astype(BF16)
    pos = base[:, 0:1] + _mm(oh, before)
    r1 = jnp.sum(jnp.where(hit1, pos, 0.0), axis=0, keepdims=True)
    r2 = jnp.sum(jnp.where(hit2, pos, 0.0), axis=0, keepdims=True)
    new_base = base[...] + jnp.sum(oh, axis=1, keepdims=True)
    base[...] = new_base
    cnt_ref[...] = new_base
    zi = jnp.zeros((4, tm), jnp.int32)
    idx_ref[...] = jnp.concatenate([e1, e2, r1.astype(jnp.int32), r2.astype(jnp.int32), zi], axis=0)
    zf = jnp.zeros((6, tm), F32)
    wc_ref[...] = jnp.transpose(jnp.concatenate([w1, w2, zf], axis=0))


MOE_ROWS = 256
ROW_GROUP = 8


def _slots_kernel(idx_ref, ps_ref, o_ref):
    idx = idx_ref[...]
    tm = idx.shape[1]
    rowx = lax.broadcasted_iota(jnp.int32, (N_EXPERTS, tm), 0)
    ps = ps_ref[:, 0:1]
    out = []
    for k in range(2):
        start = jnp.sum(jnp.where(rowx == idx[k:k + 1], ps, 0), axis=0, keepdims=True)
        out.append(start + idx[2 + k:3 + k])
    o_ref[...] = jnp.concatenate(out + [jnp.zeros((6, tm), jnp.int32)], axis=0)


def _slots(idx, pstart, tm=2048):
    n = idx.shape[1]
    tm = min(tm, n)
    ps = jnp.broadcast_to(pstart[:, None], (N_EXPERTS, LANES))
    return pl.pallas_call(
        _slots_kernel,
        grid=(n // tm,),
        in_specs=[pl.BlockSpec((8, tm), lambda i: (0, i)),
                  pl.BlockSpec((N_EXPERTS, LANES), lambda i: (0, 0))],
        out_specs=pl.BlockSpec((8, tm), lambda i: (0, i)),
        out_shape=jax.ShapeDtypeStruct((8, n), jnp.int32),
        compiler_params=_cparams(("parallel",)),
        name="moe_slots",
    )(idx, ps)


def _dispatch_kernel(pstart_ref, pend_ref, s1_ref, s2_ref, h_ref, xs_out, zbuf, sem, zsem):
    tm = s1_ref.shape[0]
    slot_refs = (s1_ref, s2_ref)

    @pl.when(pl.program_id(0) == 0)
    def _():
        zbuf[...] = jnp.zeros(zbuf.shape, zbuf.dtype)

        def zero_copy(e):
            first = pl.multiple_of(pend_ref[e] - MOE_ROWS, MOE_ROWS)
            return pltpu.make_async_copy(zbuf, xs_out.at[pl.ds(first, MOE_ROWS)], zsem)

        def zstart(e, carry):
            @pl.when(pend_ref[e] > pstart_ref[e])
            def _():
                zero_copy(e).start()
            return carry

        def zwait(e, carry):
            @pl.when(pend_ref[e] > pstart_ref[e])
            def _():
                zero_copy(e).wait()
            return carry

        lax.fori_loop(0, N_EXPERTS, zstart, 0)
        lax.fori_loop(0, N_EXPERTS, zwait, 0)

        def tail_copy(b):
            return pltpu.make_async_copy(
                zbuf, xs_out.at[pl.ds(pl.multiple_of(b * MOE_ROWS, MOE_ROWS), MOE_ROWS)], zsem)

        def tstart(b, carry):
            tail_copy(b).start()
            return carry

        def twait(b, carry):
            tail_copy(b).wait()
            return carry

        first_free = pend_ref[N_EXPERTS - 1] // MOE_ROWS
        n_blocks = xs_out.shape[0] // MOE_ROWS
        lax.fori_loop(first_free, n_blocks, tstart, 0)
        lax.fori_loop(first_free, n_blocks, twait, 0)

    def row_copy(g, u, k):
        slot = slot_refs[k][g * ROW_GROUP + u]
        return pltpu.make_async_copy(h_ref.at[g, pl.ds(u, 1)], xs_out.at[pl.ds(slot, 1)], sem)

    def start(g, carry):
        for u in range(ROW_GROUP):
            row_copy(g, u, 0).start()
            row_copy(g, u, 1).start()
        return carry

    def wait(g, carry):
        for u in range(ROW_GROUP):
            row_copy(g, u, 0).wait()
            row_copy(g, u, 1).wait()
        return carry

    lax.fori_loop(0, tm // ROW_GROUP, start, 0)
    lax.fori_loop(0, tm // ROW_GROUP, wait, 0)


def _dispatch(pstart, pend, slots, h2, n_rows, tm=512):
    n, d = h2.shape
    return pl.pallas_call(
        _dispatch_kernel,
        grid_spec=pltpu.PrefetchScalarGridSpec(
            num_scalar_prefetch=2,
            grid=(n // tm,),
            in_specs=[pl.BlockSpec((tm,), lambda i, ps, pe: (i,), memory_space=pltpu.SMEM),
                      pl.BlockSpec((tm,), lambda i, ps, pe: (i,), memory_space=pltpu.SMEM),
                      pl.BlockSpec((tm // ROW_GROUP, ROW_GROUP, d), lambda i, ps, pe: (i, 0, 0))],
            out_specs=pl.BlockSpec(memory_space=pl.ANY),
            scratch_shapes=[pltpu.VMEM((MOE_ROWS, d), h2.dtype), pltpu.SemaphoreType.DMA(()),
                            pltpu.SemaphoreType.DMA(())]),
        out_shape=jax.ShapeDtypeStruct((n_rows, d), h2.dtype),
        compiler_params=_cparams(("arbitrary",)),
        name="moe_dispatch",
    )(pstart, pend, slots[0], slots[1], h2.reshape(n // ROW_GROUP, ROW_GROUP, d))


def _expert_kernel(be_ref, nu_ref, eslot_ref, enext_ref, x_ref, wg_hbm, wu_hbm, wd_hbm, o_ref,
                   wg_f, wu_f, wd_f, wg_b, wu_b, wd_b, sem):
    i = pl.program_id(0)
    used = i < nu_ref[0]
    e = be_ref[i]
    new_expert = (i == 0) | (e != be_ref[jnp.maximum(i - 1, 0)])

    def fetch(expert, slot):
        return [pltpu.make_async_copy(src.at[expert], dst.at[slot], sem.at[slot, k])
                for k, (src, dst) in enumerate(((wg_hbm, wg_f), (wu_hbm, wu_f), (wd_hbm, wd_f)))]

    @pl.when(i == 0)
    def _():
        for c in fetch(e, eslot_ref[e]):
            c.start()

    @pl.when(used & new_expert)
    def _():
        slot = eslot_ref[e]
        for c in fetch(e, slot):
            c.wait()
        nxt = enext_ref[e]

        @pl.when(nxt >= 0)
        def _():
            for c in fetch(nxt, 1 - slot):
                c.start()

        wg_b[...] = wg_f[slot].astype(BF16)
        wu_b[...] = wu_f[slot].astype(BF16)
        wd_b[...] = wd_f[slot].astype(BF16)

    @pl.when(used)
    def _():
        x_lo, x_hi = (t.astype(BF16) for t in _unpack_bf16_pairs(x_ref[...]))
        half = x_lo.shape[1]

        def proj(w_b):
            return (jnp.dot(x_lo, w_b[0:half, :], preferred_element_type=F32)
                    + jnp.dot(x_hi, w_b[half:, :], preferred_element_type=F32))

        a = proj(wg_b)
        u = proj(wu_b)
        hh = (a * _sigmoid(a)) * u
        y = jnp.dot(hh.astype(BF16), wd_b[...], preferred_element_type=F32)
        o_ref[...] = _pack_bf16_pairs(y)

    @pl.when(i >= nu_ref[0])
    def _():
        o_ref[...] = jnp.zeros(o_ref.shape, o_ref.dtype)


def _experts(block_e, n_used, eslot, enext, xs, wg, wu, wd):
    n_rows, dh = xs.shape
    _, d, de = wg.shape
    return pl.pallas_call(
        _expert_kernel,
        grid_spec=pltpu.PrefetchScalarGridSpec(
            num_scalar_prefetch=4,
            grid=(n_rows // MOE_ROWS,),
            in_specs=[pl.BlockSpec((MOE_ROWS, dh), lambda i, be, nu, es, en: (jnp.minimum(i, nu[0] - 1), 0)),
                      pl.BlockSpec(memory_space=pl.ANY),
                      pl.BlockSpec(memory_space=pl.ANY),
                      pl.BlockSpec(memory_space=pl.ANY)],
            out_specs=pl.BlockSpec((MOE_ROWS, dh), lambda i, be, nu, es, en: (i, 0)),
            scratch_shapes=[pltpu.VMEM((2, d, de), F32), pltpu.VMEM((2, d, de), F32),
                            pltpu.VMEM((2, de, d), F32),
                            pltpu.VMEM((d, de), BF16), pltpu.VMEM((d, de), BF16),
                            pltpu.VMEM((de, d), BF16), pltpu.SemaphoreType.DMA((2, 3))]),
        out_shape=jax.ShapeDtypeStruct((n_rows, dh), jnp.uint32),
        compiler_params=_cparams(("arbitrary",)),
        name="moe_experts",
    )(block_e, n_used, eslot, enext, xs, wg, wu, wd)


def _combine_kernel(s1_ref, s2_ref, ys_hbm, x_ref, w_ref, lnf_ref, o_ref, y1, y2, sem, *,
                    final_norm):
    tm = s1_ref.shape[0]
    slot_refs = (s1_ref, s2_ref)

    def row_copy(g, u, k):
        buf = y1 if k == 0 else y2
        slot = slot_refs[k][g * ROW_GROUP + u]
        return pltpu.make_async_copy(ys_hbm.at[pl.ds(slot, 1)], buf.at[g, pl.ds(u, 1)], sem)

    def start(g, carry):
        for u in range(ROW_GROUP):
            row_copy(g, u, 0).start()
            row_copy(g, u, 1).start()
        return carry

    def wait(g, carry):
        for u in range(ROW_GROUP):
            row_copy(g, u, 0).wait()
            row_copy(g, u, 1).wait()
        return carry

    lax.fori_loop(0, tm // ROW_GROUP, start, 0)
    lax.fori_loop(0, tm // ROW_GROUP, wait, 0)
    w = w_ref[...]
    dh = y1.shape[2]
    a_lo, a_hi = _unpack_bf16_pairs(y1[...].reshape(tm, dh))
    b_lo, b_hi = _unpack_bf16_pairs(y2[...].reshape(tm, dh))
    moe = jnp.concatenate([w[:, 0:1] * a_lo + w[:, 1:2] * b_lo,
                           w[:, 0:1] * a_hi + w[:, 1:2] * b_hi], axis=1)
    y = x_ref[...] + moe
    o_ref[...] = _rmsnorm_rows(y, lnf_ref[...]) if final_norm else y


def _combine(slots, ys, x1, wc, lnf, final_norm, tm=512):
    n, d = x1.shape
    return pl.pallas_call(
        functools.partial(_combine_kernel, final_norm=final_norm),
        grid=(n // tm,),
        in_specs=[pl.BlockSpec((tm,), lambda i: (i,), memory_space=pltpu.SMEM),
                  pl.BlockSpec((tm,), lambda i: (i,), memory_space=pltpu.SMEM),
                  pl.BlockSpec(memory_space=pl.ANY),
                  pl.BlockSpec((tm, d), lambda i: (i, 0)),
                  pl.BlockSpec((tm, 8), lambda i: (i, 0)),
                  pl.BlockSpec((1, d), lambda i: (0, 0))],
        out_specs=pl.BlockSpec((tm, d), lambda i: (i, 0)),
        scratch_shapes=[pltpu.VMEM((tm // ROW_GROUP, ROW_GROUP, d // 2), jnp.uint32),
                        pltpu.VMEM((tm // ROW_GROUP, ROW_GROUP, d // 2), jnp.uint32),
                        pltpu.SemaphoreType.DMA(())],
        out_shape=jax.ShapeDtypeStruct((n, d), F32),
        compiler_params=_cparams(("arbitrary",)),
        name="moe_combine",
    )(slots[0], slots[1], ys, x1, wc, lnf)


def _moe(h2, idx, wc, counts, wg, wu, wd, x1, lnf, final_norm):
    n = h2.shape[0]
    n_rows = 2 * n + N_EXPERTS * MOE_ROWS
    cnt = counts[:, 0].astype(jnp.int32)
    padded = ((cnt + MOE_ROWS - 1) // MOE_ROWS) * MOE_ROWS
    pend = jnp.cumsum(padded)
    pstart = pend - padded
    blk_first = jnp.arange(n_rows // MOE_ROWS, dtype=jnp.int32) * MOE_ROWS
    block_e = jnp.minimum(jnp.sum(pend[None, :] <= blk_first[:, None], axis=1), N_EXPERTS - 1)
    n_used = (pend[-1:] // MOE_ROWS).astype(jnp.int32)
    has = padded > 0
    eslot = ((jnp.cumsum(has) - 1) % 2).astype(jnp.int32)
    ee = jnp.arange(N_EXPERTS, dtype=jnp.int32)
    later = jnp.where((ee[None, :] > ee[:, None]) & has[None, :], ee[None, :], N_EXPERTS)
    enext = jnp.min(later, axis=1)
    enext = jnp.where(enext == N_EXPERTS, -1, enext).astype(jnp.int32)
    slots = _slots(idx, pstart)
    xs = _dispatch(pstart, pend, slots, h2, n_rows)
    ys = _experts(block_e.astype(jnp.int32), n_used, eslot, enext, xs, wg, wu, wd)
    return _combine(slots, ys, x1, wc, lnf, final_norm)


def _pad_rows(a, before, total):
    return jnp.pad(a, ((before, total - before - a.shape[0]), (0, 0)))


def _layer(xf, batch, seq_len, ln1_w, w_in, rw_mu, rw_w0_f, rw_w2_f, rw_w0_b, rw_w2_b, rw_a0_f,
           rw_a2_f, rw_a0_b, rw_a2_b, rw_g2, rw_k_k, rw_k_a, rw_r_k, rw_ln_w, rw_ln_b, gla_gw2_f,
           gla_gb_f, gla_gw2_b, gla_gb_b, gla_norm_w, w_out, ln2_w, moe_w_coarse, moe_b_coarse,
           moe_w_fine, moe_b_fine, moe_w_gate, moe_w_up, moe_w_down):
    d = xf.shape[1]
    rw_cols = 3 * RWKV_WIDTH + 4 * LORA + GATE_LORA
    w_rw, w_gla = w_in[:, :rw_cols], w_in[:, rw_cols:]
    o = 0
    w_gq = w_gla[:, o:o + GLA_K_WIDTH]; o += GLA_K_WIDTH
    w_gk = w_gla[:, o:o + GLA_K_WIDTH]; o += GLA_K_WIDTH
    w_gv = w_gla[:, o:o + GLA_V_WIDTH]; o += GLA_V_WIDTH
    w_gg = w_gla[:, o:o + 2 * GLA_GATE_LORA]; o += 2 * GLA_GATE_LORA
    w_og = w_gla[:, o:o + GLA_V_WIDTH]
    w_gg = jnp.pad(w_gg, ((0, 0), (0, LANES - 2 * GLA_GATE_LORA)))
    w_pad = jnp.concatenate([w_rw, w_gg, w_gq, w_gk, w_gv, w_og], axis=1).astype(BF16)
    mu_pad = jnp.pad(rw_mu, (0, NP - rw_cols))[None, :]

    p = _inproj(xf, ln1_w[None, :], w_pad, mu_pad, seq_len, rw_cols)

    head = lax.broadcasted_iota(jnp.int32, (RWKV_WIDTH, RWKV_WIDTH), 0) // RWKV_HEAD_DIM
    blk = (head == head.T).astype(BF16)
    w0 = jnp.stack([rw_w0_f, rw_w0_b])[:, None, :]
    a0 = jnp.stack([rw_a0_f, rw_a0_b])[:, None, :]
    w2 = jnp.stack([_pad_rows(rw_w2_f, 0, LANES), _pad_rows(rw_w2_b, LORA, LANES)]).astype(BF16)
    a2 = jnp.stack([_pad_rows(rw_a2_f, 0, LANES), _pad_rows(rw_a2_b, LORA, LANES)]).astype(BF16)
    yf, yb, bdf, bdb = _rwkv_scan(p, batch, seq_len, w0, w2, a0, a2, rw_k_k[None, :],
                                  rw_k_a[None, :], rw_r_k.reshape(1, -1), blk)

    gw = jnp.stack([_pad_rows(gla_gw2_f, 0, LANES),
                    _pad_rows(gla_gw2_b, GLA_GATE_LORA, LANES)]).astype(BF16)
    gbias = jnp.stack([gla_gb_f, gla_gb_b])[:, None, :]
    of, ob = _gla_scan(p, batch, seq_len, gw, gbias)

    wr = jnp.concatenate([moe_w_coarse.T, jnp.zeros((8 - N_GROUPS, d), F32), moe_w_fine.T], axis=0)
    br = jnp.concatenate([moe_b_coarse, jnp.zeros((8 - N_GROUPS,), F32), moe_b_fine])[:, None]
    x1, h2, idx, wc, counts = _mix_out(
        xf, p, yf, yb, bdf, bdb, of, ob, rw_g2.astype(BF16), rw_ln_w[None, :], rw_ln_b[None, :], blk,
        gla_norm_w[None, :], w_out.astype(BF16), ln2_w[None, :], wr, br)
    return h2, idx, wc, counts, x1


def kernel(x, ln1_w, w_in, rw_mu, rw_w0_f, rw_w2_f, rw_w0_b, rw_w2_b, rw_a0_f, rw_a2_f, rw_a0_b, rw_a2_b, rw_g2, rw_k_k, rw_k_a, rw_r_k, rw_ln_w, rw_ln_b, gla_gw2_f, gla_gb_f, gla_gw2_b, gla_gb_b, gla_norm_w, w_out, ln2_w, moe_w_coarse, moe_b_coarse, moe_w_fine, moe_b_fine, moe_w_gate, moe_w_up, moe_w_down, ln_f_w):
    batch, seq_len, d = x.shape
    xf = x.reshape(batch * seq_len, d)
    depth = w_in.shape[0]
    for l in range(depth):
        h2, idx, wc, counts, x1 = _layer(
            xf, batch, seq_len, ln1_w[l], w_in[l], rw_mu[l], rw_w0_f[l], rw_w2_f[l], rw_w0_b[l],
            rw_w2_b[l], rw_a0_f[l], rw_a2_f[l], rw_a0_b[l], rw_a2_b[l], rw_g2[l], rw_k_k[l],
            rw_k_a[l], rw_r_k[l], rw_ln_w[l], rw_ln_b[l], gla_gw2_f[l], gla_gb_f[l], gla_gw2_b[l],
            gla_gb_b[l], gla_norm_w[l], w_out[l], ln2_w[l], moe_w_coarse[l], moe_b_coarse[l],
            moe_w_fine[l], moe_b_fine[l], moe_w_gate[l], moe_w_up[l], moe_w_down[l])
        xf = _moe(h2, idx, wc, counts, moe_w_gate[l], moe_w_up[l], moe_w_down[l], x1,
                  ln_f_w[None, :], l == depth - 1)
    return xf.reshape(batch, seq_len, d)
```

```python
import functools

import numpy as np
import jax
import jax.numpy as jnp
from jax import lax
from jax.experimental import pallas as pl
from jax.experimental.pallas import tpu as pltpu

F32 = jnp.float32
BF16 = jnp.bfloat16

RMS_EPS = 1e-6
RWKV_GN_EPS = 64e-5
RWKV_WIDTH = 512
RWKV_HEAD_DIM = 64
LORA = 64
GATE_LORA = 128
GLA_HEADS = 4
GLA_DK = 64
GLA_DV = 128
GLA_K_WIDTH = GLA_HEADS * GLA_DK
GLA_V_WIDTH = GLA_HEADS * GLA_DV
GLA_GATE_LORA = 16
GLA_TAU = 16.0
N_GROUPS = 4
EXPERTS_PER_GROUP = 8
N_EXPERTS = N_GROUPS * EXPERTS_PER_GROUP

CHUNK = 64
LANES = 128

COL_R, COL_K, COL_V = 0, 512, 1024
COL_LORA = 1536
COL_GL = 1792
COL_GG = 1920
COL_GQ, COL_GK, COL_GV, COL_OG = 2048, 2304, 2560, 3072
NP = 3584

VMEM_LIMIT = 56 * 1024 * 1024


def _cparams(sem):
    return pltpu.CompilerParams(dimension_semantics=sem, vmem_limit_bytes=VMEM_LIMIT)


MM_DTYPE = BF16


def _mm(a, b, ca=1, cb=0):
    dt = MM_DTYPE
    return lax.dot_general(a.astype(dt), b.astype(dt), (((ca,), (cb,)), ((), ())),
                           preferred_element_type=F32)


def _split2(x):
    hi = x.astype(BF16)
    lo = (x - hi.astype(F32)).astype(BF16)
    return hi, lo


def _split3(x):
    hi = x.astype(BF16)
    r1 = x - hi.astype(F32)
    mid = r1.astype(BF16)
    lo = (r1 - mid.astype(F32)).astype(BF16)
    return hi, mid, lo


def _mm_ones(x, ones_bf16, left=False, terms=3):
    parts = _split3(x)[:terms]
    prods = [_mm(ones_bf16, t) if left else _mm(t, ones_bf16) for t in parts]
    out = prods[-1]
    for t in prods[-2::-1]:
        out = out + t
    return out


def _mm_x3(a, b, ca=1, cb=0):
    ah, al = _split2(a)
    bh, bl = _split2(b)
    return (_mm(al, bh, ca, cb) + _mm(ah, bl, ca, cb)) + _mm(ah, bh, ca, cb)


def _pack_bf16_pairs(x):
    half = x.shape[1] // 2
    bits = lax.bitcast_convert_type(x.astype(BF16).astype(F32), jnp.uint32)
    return (bits[:, :half] >> 16) | (bits[:, half:] & jnp.uint32(0xFFFF0000))


def _unpack_bf16_pairs(w):
    lo = lax.bitcast_convert_type(w << 16, F32)
    hi = lax.bitcast_convert_type(w & jnp.uint32(0xFFFF0000), F32)
    return lo, hi


def _rmsnorm_rows(x, w):
    ms = jnp.mean(x * x, axis=-1, keepdims=True)
    return x * lax.rsqrt(ms + RMS_EPS) * w


def _softplus(x):
    return jnp.maximum(x, 0.0) + jnp.log(1.0 + jnp.exp(-jnp.abs(x)))


def _sigmoid(x):
    return 1.0 / (1.0 + jnp.exp(-x))


def _seg_cumsum(x, seg, rev):
    n = x.shape[0]
    pos = lax.broadcasted_iota(jnp.int32, x.shape, 0) & (seg - 1)
    sh = 1
    while sh < seg:
        if not rev:
            x = x + jnp.where(pos >= sh, pltpu.roll(x, sh, 0), 0.0)
        else:
            x = x + jnp.where(pos < seg - sh, pltpu.roll(x, n - sh, 0), 0.0)
        sh *= 2
    return x


def _blockdiag2(y, m0):
    return jnp.concatenate([jnp.where(m0, y, 0.0), jnp.where(m0, 0.0, y)], axis=0)


def _inproj_kernel(x_ref, xp_ref, xn_ref, lnw_ref, w_ref, mu_ref, o_ref, *, tiles_per_seq,
                   shift_cols, tn):
    i = pl.program_id(0)
    lnw = lnw_ref[...]
    h = _rmsnorm_rows(x_ref[...], lnw).astype(BF16)
    t = i % tiles_per_seq
    hp = jnp.where(t == 0, 0.0, _rmsnorm_rows(xp_ref[...], lnw))
    hn = jnp.where(t == tiles_per_seq - 1, 0.0, _rmsnorm_rows(xn_ref[...], lnw))
    hh = jnp.concatenate([hp, hn], axis=0).astype(BF16)
    tm = h.shape[0]
    npad = w_ref.shape[1]
    bounds = [(c0, min(c0 + tn, shift_cols)) for c0 in range(0, shift_cols, tn)]
    bounds += [(c0, min(c0 + tn, npad)) for c0 in range(shift_cols, npad, tn)]
    for c0, c1 in bounds:
        w = w_ref[:, c0:c1]
        p = jnp.dot(h, w, preferred_element_type=F32)
        if c0 < shift_cols:
            ph = jnp.dot(hh, w, preferred_element_type=F32)
            row = lax.broadcasted_iota(jnp.int32, p.shape, 0)
            prev = jnp.where(row == 0, ph[7:8, :], pltpu.roll(p, 1, 0))
            nxt = jnp.where(row == tm - 1, ph[8:9, :], pltpu.roll(p, tm - 1, 0))
            p = p + mu_ref[:, c0:c1] * (0.5 * (prev + nxt) - p)
        o_ref[:, c0:c1] = p


def _inproj(xf, lnw, w_pad, mu_pad, seq_len, shift_cols, tm=512, tn=640):
    n, d = xf.shape
    npad = w_pad.shape[1]
    assert seq_len % tm == 0 and shift_cols % LANES == 0 and npad % LANES == 0 and tn % LANES == 0
    tps = seq_len // tm
    nb8 = n // 8
    return pl.pallas_call(
        functools.partial(_inproj_kernel, tiles_per_seq=tps, shift_cols=shift_cols, tn=tn),
        grid=(n // tm,),
        in_specs=[
            pl.BlockSpec((tm, d), lambda i: (i, 0)),
            pl.BlockSpec((8, d), lambda i: (jnp.maximum(i * (tm // 8) - 1, 0), 0)),
            pl.BlockSpec((8, d), lambda i: (jnp.minimum((i + 1) * (tm // 8), nb8 - 1), 0)),
            pl.BlockSpec((1, d), lambda i: (0, 0)),
            pl.BlockSpec((d, npad), lambda i: (0, 0)),
            pl.BlockSpec((1, npad), lambda i: (0, 0)),
        ],
        out_specs=pl.BlockSpec((tm, npad), lambda i: (i, 0)),
        out_shape=jax.ShapeDtypeStruct((n, npad), F32),
        compiler_params=_cparams(("parallel",)),
        name="inproj_shift",
    )(xf, xf, xf, lnw, w_pad, mu_pad)


RWKV_CPS = 2


def _rwkv_prep(d, r_ref, k_ref, v_ref, l_ref, w0_ref, w2_ref, a0_ref, a2_ref,
               kk_ref, ka_ref, rk_ref, blk_ref, bd_ref):
    rev = d == 1
    c = CHUNK
    r = r_ref[...]
    k = k_ref[...]
    lo = l_ref[...]
    blk = blk_ref[...]

    zw = w0_ref[d] + _mm(jnp.tanh(lo[:, 0:LANES]), w2_ref[d])
    lw = -jnp.exp(-_softplus(-zw) - 0.5)
    a = _sigmoid(a0_ref[d] + _mm(lo[:, LANES:2 * LANES], a2_ref[d]))
    kkr = k * kk_ref[...]
    ss = _mm_ones(kkr * kkr, blk, terms=1)
    kk = kkr / jnp.maximum(jnp.sqrt(ss), 1e-12)
    kd = k * (1.0 + (a - 1.0) * ka_ref[...])
    b = kk * a
    bd_ref[...] = _mm_ones(r * kd * rk_ref[...], blk, terms=2)

    cum = _seg_cumsum(lw, c, rev)
    n_blk = r.shape[0] // c
    tot = [cum[j * c:j * c + 1, :] if rev else cum[j * c + c - 1:j * c + c, :] for j in range(n_blk)]
    ginv = jnp.exp(-cum)
    gend = jnp.concatenate([jnp.exp(tot[j] - cum[j * c:(j + 1) * c]) for j in range(n_blk)], axis=0)
    return dict(at=-kk * jnp.exp(cum - lw), rt=r * jnp.exp(cum), bt=b * ginv, kt=kd * ginv,
                bh=b * gend, kh=kd * gend, gc=[jnp.exp(t) for t in tot], v=v_ref[...])


def _rwkv_kernel(rf, kf, vf, lf, rb, kb, vb, lb, w0, w2, a0, a2, kk, ka, rk, blk,
                 yf, yb, bdf, bdb, m_scr):
    @pl.when(pl.program_id(1) == 0)
    def _():
        m_scr[...] = jnp.zeros(m_scr.shape, F32)

    c = CHUNK
    n_blk = rf.shape[0] // c
    n_pair = RWKV_WIDTH // LANES
    prep = [_rwkv_prep(0, rf, kf, vf, lf, w0, w2, a0, a2, kk, ka, rk, blk, bdf),
            _rwkv_prep(1, rb, kb, vb, lb, w0, w2, a0, a2, kk, ka, rk, blk, bdb)]
    y_refs = [yf, yb]

    row = lax.broadcasted_iota(jnp.int32, (c, LANES), 0)
    lane = lax.broadcasted_iota(jnp.int32, (c, LANES), 1)
    s_idx = lane & (RWKV_HEAD_DIM - 1)
    m0 = lane < RWKV_HEAD_DIM
    strict = [s_idx < row, s_idx > row]
    incl = [s_idx <= row, s_idx >= row]
    diag = s_idx == row

    def bd2(y):
        return _blockdiag2(y, m0)

    def fold(z):
        return jnp.where(m0, z[:c], 0.0) + jnp.where(m0, 0.0, z[c:])

    chains = [(d, p, j) for j in range(n_blk) for p in range(n_pair) for d in (0, 1)]
    ds = [d for d, _, _ in chains]

    def get(name):
        return [prep[d][name][j * c:(j + 1) * c, p * LANES:(p + 1) * LANES] for d, p, j in chains]

    at, rt, bt, kt, bh, kh, vp = (get(n) for n in ("at", "rt", "bt", "kt", "bh", "kh", "v"))
    gc = [prep[d]["gc"][j][:, p * LANES:(p + 1) * LANES] for d, p, j in chains]
    x2 = [jnp.concatenate([a_, r_], axis=0) for a_, r_ in zip(at, rt)]
    sbk = [_mm(x, jnp.concatenate([bd2(b_), bd2(k_)], axis=0), 1, 1) for x, b_, k_ in zip(x2, bt, kt)]
    sb = [z[:, :LANES] for z in sbk]
    sk = [z[:, LANES:] for z in sbk]
    lab = [jnp.where(strict[d], z[:c], 0.0) for d, z in zip(ds, sb)]
    prb = [jnp.where(incl[d], z[c:], 0.0) for d, z in zip(ds, sb)]
    lak = [jnp.where(strict[d], z[:c], 0.0) for d, z in zip(ds, sk)]
    prk = [jnp.where(incl[d], z[c:], 0.0) for d, z in zip(ds, sk)]
    bdv = [bd2(y) for y in vp]
    xa = at
    xv = [_mm(x, y) for x, y in zip(lak, bdv)]
    lp = lab
    n_it = 6
    for it in range(n_it):
        upd = [_mm(l_, jnp.concatenate([bd2(a_), bd2(v_)], axis=1)) for l_, a_, v_ in zip(lp, xa, xv)]
        xa = [a_ + u[:, :LANES] for a_, u in zip(xa, upd)]
        xv = [v_ + u[:, LANES:] for v_, u in zip(xv, upd)]
        if it < n_it - 1:
            lp = [_mm(l_, bd2(l_)) for l_ in lp]
    px = [_mm(x, jnp.concatenate([bd2(a_), bd2(v_)], axis=1)) for x, a_, v_ in zip(prb, xa, xv)]
    pv = [_mm(x, y) for x, y in zip(prk, bdv)]
    zb = [_mm(x, jnp.concatenate([a_, v_], axis=1), 0, 0) for x, a_, v_ in zip(bh, xa, xv)]
    zk = [_mm(x, y, 0, 0) for x, y in zip(kh, vp)]
    qg, yl, hh = {}, {}, {}
    for i, key in enumerate(chains):
        qp = rt[i] + px[i][:, :LANES]
        g = jnp.where(diag, gc[i], 0.0) + fold(zb[i][:, :LANES])
        qg[key] = jnp.concatenate([qp, g], axis=0)
        yl[key] = px[i][:, LANES:] + pv[i]
        hh[key] = fold(zb[i][:, LANES:]) + fold(zk[i])
    state = {(d, p): m_scr[d, p] for p in range(n_pair) for d in (0, 1)}
    for step in range(n_blk):
        for p in range(n_pair):
            for d in (0, 1):
                j = n_blk - 1 - step if d == 1 else step
                yg = _mm(qg[(d, p, j)], bd2(state[(d, p)]))
                y_refs[d][j * c:(j + 1) * c, p * LANES:(p + 1) * LANES] = yg[:c] + yl[(d, p, j)]
                state[(d, p)] = yg[c:] + hh[(d, p, j)]
    for (d, p), m in state.items():
        m_scr[d, p] = m


def _rwkv_scan(p, batch, seq_len, w0, w2, a0, a2, kk, ka, rk, blk):
    n = p.shape[0]
    rows = RWKV_CPS * CHUNK
    assert seq_len % rows == 0
    ns = seq_len // rows
    w = RWKV_WIDTH

    def fwd(col, width):
        return pl.BlockSpec((rows, width), lambda b, c: (b * ns + c, col // width))

    def bwd(col, width):
        return pl.BlockSpec((rows, width), lambda b, c: (b * ns + ns - 1 - c, col // width))

    def full(a):
        return pl.BlockSpec(a.shape, lambda b, c: (0,) * a.ndim)

    lora_w = 2 * LANES
    in_specs = [fwd(COL_R, w), fwd(COL_K, w), fwd(COL_V, w), fwd(COL_LORA, lora_w),
                bwd(COL_R, w), bwd(COL_K, w), bwd(COL_V, w), bwd(COL_LORA, lora_w),
                full(w0), full(w2), full(a0), full(a2), full(kk), full(ka), full(rk), full(blk)]
    ospec_f = pl.BlockSpec((rows, w), lambda b, c: (b * ns + c, 0))
    ospec_b = pl.BlockSpec((rows, w), lambda b, c: (b * ns + ns - 1 - c, 0))
    osh = jax.ShapeDtypeStruct((n, w), F32)
    return pl.pallas_call(
        _rwkv_kernel,
        grid=(batch, ns),
        in_specs=in_specs,
        out_specs=[ospec_f, ospec_b, ospec_f, ospec_b],
        out_shape=[osh, osh, osh, osh],
        scratch_shapes=[pltpu.VMEM((2, w // LANES, CHUNK, LANES), F32)],
        compiler_params=_cparams(("parallel", "arbitrary")),
        name="rwkv7_scan",
    )(p, p, p, p, p, p, p, p, w0, w2, a0, a2, kk, ka, rk, blk)


GLA_CPS = 4


def _gla_gates(d, g_ref, gw_ref, gb_ref, cum_ref):
    c = CHUNK
    xg = _mm(g_ref[...], gw_ref[d]) + gb_ref[d]
    g = -_softplus(-xg) / GLA_TAU
    return [_mm_ones(g[j * c:(j + 1) * c], cum_ref[d], left=True, terms=2)
            for j in range(g.shape[0] // c)]


def _gla_kernel(gf, qf, kf, vf, gb_, qb, kb, vb, gw, gbias, cum, of, ob, st_scr):
    @pl.when(pl.program_id(1) == 0)
    def _():
        st_scr[...] = jnp.zeros(st_scr.shape, F32)

    c = CHUNK
    n_blk = qf.shape[0] // c
    n_pair = GLA_K_WIDTH // LANES
    o_refs = [of, ob]
    cds = [_gla_gates(0, gf, gw, gbias, cum), _gla_gates(1, gb_, gw, gbias, cum)]
    qs = [qf[...] * (GLA_DK ** -0.5), qb[...] * (GLA_DK ** -0.5)]
    ks = [kf[...], kb[...]]
    vs = [vf[...], vb[...]]

    row = lax.broadcasted_iota(jnp.int32, (c, LANES), 0)
    lane = lax.broadcasted_iota(jnp.int32, (c, LANES), 1)
    s_idx = lane & (GLA_DK - 1)
    m0 = lane < GLA_DK
    diag = row == s_idx
    wide_row = lax.broadcasted_iota(jnp.int32, (c, GLA_K_WIDTH), 0)
    wide_prog = [wide_row, c - 1 - wide_row]

    chains = [(d, p, j) for j in range(n_blk) for p in range(n_pair) for d in (0, 1)]

    def blk_rows(x, j):
        return x[j * c:(j + 1) * c]

    def pair(x, p):
        return x[:, p * LANES:(p + 1) * LANES]

    def nt(x, y):
        return _mm(x, _blockdiag2(y, m0), 1, 1)

    qb_ = {(d, j): blk_rows(qs[d], j) for d in (0, 1) for j in range(n_blk)}
    kb_ = {(d, j): blk_rows(ks[d], j) for d in (0, 1) for j in range(n_blk)}
    a = [jnp.where(diag, nt(pair(qb_[(d, j)], p), pair(kb_[(d, j)], p)), 0.0) for d, p, j in chains]
    s = 1
    lvl = 1
    while s < c:
        same = (row // (2 * s)) == (s_idx // (2 * s))
        ql, kl = {}, {}
        for d in (0, 1):
            upper = (wide_prog[d] & (2 * s - 1)) >= s
            for j in range(n_blk):
                e = jnp.exp(-jnp.abs(cds[d][j][lvl * c:(lvl + 1) * c]))
                ql[(d, j)] = qb_[(d, j)] * jnp.where(upper, e, 0.0)
                kl[(d, j)] = kb_[(d, j)] * jnp.where(upper, 0.0, e)
        upd = [jnp.where(same, nt(pair(ql[(d, j)], p), pair(kl[(d, j)], p)), 0.0)
               for d, p, j in chains]
        a = [x + y for x, y in zip(a, upd)]
        s *= 2
        lvl += 1

    qt, kh, gc = {}, {}, {}
    for d in (0, 1):
        for j in range(n_blk):
            bc = cds[d][j][0:c]
            tot = bc[0:1, :] if d == 1 else bc[c - 1:c, :]
            qt[(d, j)] = qb_[(d, j)] * jnp.exp(bc)
            kh[(d, j)] = kb_[(d, j)] * jnp.exp(tot - bc)
            gc[(d, j)] = jnp.exp(tot)

    def vhead(d, j, h):
        return vs[d][j * c:(j + 1) * c, h * GLA_DV:(h + 1) * GLA_DV]

    oa = {key: _mm(_blockdiag2(a[i], m0),
                   jnp.concatenate([vhead(key[0], key[2], 2 * key[1]),
                                    vhead(key[0], key[2], 2 * key[1] + 1)], axis=0))
          for i, key in enumerate(chains)}
    z = {(d, p, j): _mm(jnp.concatenate([vhead(d, j, 2 * p), vhead(d, j, 2 * p + 1)], axis=1),
                        pair(kh[(d, j)], p), 0, 0)
         for d, p, j in chains}
    mv = lax.broadcasted_iota(jnp.int32, (GLA_DV, LANES), 1) < GLA_DK
    state = {(d, p): st_scr[d, p] for p in range(n_pair) for d in (0, 1)}
    for step in range(n_blk):
        for p in range(n_pair):
            for d in (0, 1):
                j = n_blk - 1 - step if d == 1 else step
                st = state[(d, p)]
                o = _mm(_blockdiag2(pair(qt[(d, j)], p), m0), st, 1, 1) + oa[(d, p, j)]
                o_refs[d][j * c:(j + 1) * c, (2 * p) * GLA_DV:(2 * p + 1) * GLA_DV] = o[:c]
                o_refs[d][j * c:(j + 1) * c, (2 * p + 1) * GLA_DV:(2 * p + 2) * GLA_DV] = o[c:]
                zz = z[(d, p, j)]
                state[(d, p)] = (st * pair(gc[(d, j)], p) + jnp.where(mv, zz[:GLA_DV], 0.0)
                                 + jnp.where(mv, 0.0, zz[GLA_DV:]))
    for (d, p), st in state.items():
        st_scr[d, p] = st


def _gla_cumsum_matrices():
    c = CHUNK
    i = np.arange(c)
    out = []
    for rev in (False, True):
        tri = (i[None, :] >= i[:, None]) if rev else (i[None, :] <= i[:, None])
        tri = tri.astype(np.float32)
        mats = [tri]
        s = 1
        while s < c:
            ref = (i // (2 * s)) * (2 * s) + (s if rev else s - 1)
            mats.append(tri - tri[ref])
            s *= 2
        out.append(np.concatenate(mats, axis=0))
    return jnp.asarray(np.stack(out), dtype=BF16)


def _gla_scan(p, batch, seq_len, gw, gbias):
    n = p.shape[0]
    rows = GLA_CPS * CHUNK
    assert seq_len % rows == 0
    nc = seq_len // rows

    def fwd(col, width):
        return pl.BlockSpec((rows, width), lambda b, c: (b * nc + c, col // width))

    def bwd(col, width):
        return pl.BlockSpec((rows, width), lambda b, c: (b * nc + nc - 1 - c, col // width))

    def full(a):
        return pl.BlockSpec(a.shape, lambda b, c: (0,) * a.ndim)

    kw, vw = GLA_K_WIDTH, GLA_V_WIDTH
    cum = _gla_cumsum_matrices()
    in_specs = [fwd(COL_GG, LANES), fwd(COL_GQ, kw), fwd(COL_GK, kw), fwd(COL_GV, vw),
                bwd(COL_GG, LANES), bwd(COL_GQ, kw), bwd(COL_GK, kw), bwd(COL_GV, vw),
                full(gw), full(gbias), full(cum)]
    ospec_f = pl.BlockSpec((rows, vw), lambda b, c: (b * nc + c, 0))
    ospec_b = pl.BlockSpec((rows, vw), lambda b, c: (b * nc + nc - 1 - c, 0))
    osh = jax.ShapeDtypeStruct((n, vw), F32)
    return pl.pallas_call(
        _gla_kernel,
        grid=(batch, nc),
        in_specs=in_specs,
        out_specs=[ospec_f, ospec_b],
        out_shape=[osh, osh],
        scratch_shapes=[pltpu.VMEM((2, kw // LANES, GLA_DV, LANES), F32)],
        compiler_params=_cparams(("parallel", "arbitrary")),
        name="gla_scan",
    )(p, p, p, p, p, p, p, p, gw, gbias, cum)


MIX_ROWS = 256


def _mix_out_kernel(x_ref, yf, yb, bdf, bdb, v_ref, gl_ref, of, ob, og_ref,
                    g2_ref, lnw_ref, lnb_ref, blk_ref, nw_ref, wo_ref, ln2_ref, wr_ref, br_ref,
                    o_ref, h_ref, idx_ref, wc_ref, cnt_ref, base):
    blk = blk_ref[...]
    inv = 1.0 / RWKV_HEAD_DIM
    tm = x_ref.shape[0]
    rows = [slice(r0, r0 + MIX_ROWS) for r0 in range(0, tm, MIX_ROWS)]
    y = [yf[r] + yb[r] for r in rows]
    mean = [_mm_ones(t, blk, terms=2) * inv for t in y]
    yc = [a - b for a, b in zip(y, mean)]
    var = [_mm_ones(t * t, blk, terms=1) * inv for t in yc]
    gate = [_mm(_sigmoid(gl_ref[r]), g2_ref[...]) for r in rows]
    y_rw = [(c * lax.rsqrt(s + RWKV_GN_EPS) * lnw_ref[...] + lnb_ref[...]
             + (bdf[r] + bdb[r]) * v_ref[r]) * g
            for c, s, g, r in zip(yc, var, gate, rows)]
    acc = [x_ref[r] + _mm(t, wo_ref[0:RWKV_WIDTH, :]) for t, r in zip(y_rw, rows)]
    nw = nw_ref[...]
    for h in range(GLA_HEADS):
        sl = slice(h * GLA_DV, (h + 1) * GLA_DV)
        w_h = wo_ref[RWKV_WIDTH + h * GLA_DV:RWKV_WIDTH + (h + 1) * GLA_DV, :]
        for i, r in enumerate(rows):
            oh = of[r, sl] + ob[r, sl]
            og = og_ref[r, sl]
            ms = jnp.mean(oh * oh, axis=-1, keepdims=True)
            yg = oh * lax.rsqrt(ms + RMS_EPS) * nw[:, sl] * (og * _sigmoid(og))
            acc[i] = acc[i] + _mm(yg, w_h)
    x1 = jnp.concatenate(acc, axis=0)
    o_ref[...] = x1
    _route(x1, ln2_ref, wr_ref, br_ref, h_ref, idx_ref, wc_ref, cnt_ref, base)


def _mix_out(xf, p, yf, yb, bdf, bdb, of, ob, g2, lnw, lnb, blk, nw, wo, ln2, wr, br, tm=512):
    n, d = xf.shape
    w = RWKV_WIDTH

    def rows(width, col=0):
        return pl.BlockSpec((tm, width), lambda i: (i, col // width))

    def full(a):
        return pl.BlockSpec(a.shape, lambda i: (0,) * a.ndim)

    in_specs = [rows(d), rows(w), rows(w), rows(w), rows(w), rows(w, COL_V), rows(LANES, COL_GL),
                rows(w), rows(w), rows(w, COL_OG),
                full(g2), full(lnw), full(lnb), full(blk), full(nw), full(wo),
                full(ln2), full(wr), full(br)]
    return pl.pallas_call(
        _mix_out_kernel,
        grid=(n // tm,),
        in_specs=in_specs,
        out_specs=[rows(d), rows(d // 2),
                   pl.BlockSpec((8, tm), lambda i: (0, i)),
                   pl.BlockSpec((tm, 8), lambda i: (i, 0)),
                   pl.BlockSpec((N_EXPERTS, LANES), lambda i: (0, 0))],
        out_shape=[jax.ShapeDtypeStruct((n, d), F32),
                   jax.ShapeDtypeStruct((n, d // 2), jnp.uint32),
                   jax.ShapeDtypeStruct((8, n), jnp.int32),
                   jax.ShapeDtypeStruct((n, 8), F32),
                   jax.ShapeDtypeStruct((N_EXPERTS, LANES), F32)],
        scratch_shapes=[pltpu.VMEM((N_EXPERTS, LANES), F32)],
        compiler_params=_cparams(("arbitrary",)),
        name="mix_outproj_router",
    )(xf, yf, yb, bdf, bdb, p, p, of, ob, p, g2, lnw, lnb, blk, nw, wo, ln2, wr, br)


def _route(x1, lnw_ref, wr_ref, br_ref, h_ref, idx_ref, wc_ref, cnt_ref, base):
    @pl.when(pl.program_id(0) == 0)
    def _():
        base[...] = jnp.zeros(base.shape, F32)

    h = _rmsnorm_rows(x1, lnw_ref[...])
    h_ref[...] = _pack_bf16_pairs(h)
    logits = _mm_x3(wr_ref[...], h, 1, 1) + br_ref[...]
    tm = logits.shape[1]
    coarse = logits[0:N_GROUPS, :]
    fine = logits[8:8 + N_EXPERTS, :]
    rowg = lax.broadcasted_iota(jnp.int32, (N_GROUPS, tm), 0)
    cmax = jnp.max(coarse, axis=0, keepdims=True)
    gsel = jnp.min(jnp.where(coarse == cmax, rowg, N_GROUPS), axis=0, keepdims=True)
    pg = 1.0 / jnp.sum(jnp.exp(coarse - cmax), axis=0, keepdims=True)
    sel = jnp.zeros((EXPERTS_PER_GROUP, tm), F32)
    for g in range(N_GROUPS):
        sel = sel + jnp.where(gsel == g, fine[g * EXPERTS_PER_GROUP:(g + 1) * EXPERTS_PER_GROUP, :], 0.0)
    rowe = lax.broadcasted_iota(jnp.int32, (EXPERTS_PER_GROUP, tm), 0)
    l1 = jnp.max(sel, axis=0, keepdims=True)
    i1 = jnp.min(jnp.where(sel == l1, rowe, EXPERTS_PER_GROUP), axis=0, keepdims=True)
    sel2 = jnp.where(rowe == i1, -jnp.inf, sel)
    l2 = jnp.max(sel2, axis=0, keepdims=True)
    i2 = jnp.min(jnp.where(sel2 == l2, rowe, EXPERTS_PER_GROUP), axis=0, keepdims=True)
    t = jnp.exp(l2 - l1)
    w1 = pg / (1.0 + t)
    w2 = pg * t / (1.0 + t)
    e1 = gsel * EXPERTS_PER_GROUP + i1
    e2 = gsel * EXPERTS_PER_GROUP + i2
    rowx = lax.broadcasted_iota(jnp.int32, (N_EXPERTS, tm), 0)
    hit1 = rowx == e1
    hit2 = rowx == e2
    oh = jnp.where(hit1 | hit2, 1.0, 0.0)
    tr = lax.broadcasted_iota(jnp.int32, (tm, tm), 0)
    tc = lax.broadcasted_iota(jnp.int32, (tm, tm), 1)
    before = jnp.where(tr < tc, 1.0, 0.0).astype(BF16)
    pos = base[:, 0:1] + _mm(oh, before)
    r1 = jnp.sum(jnp.where(hit1, pos, 0.0), axis=0, keepdims=True)
    r2 = jnp.sum(jnp.where(hit2, pos, 0.0), axis=0, keepdims=True)
    new_base = base[...] + jnp.sum(oh, axis=1, keepdims=True)
    base[...] = new_base
    cnt_ref[...] = new_base
    zi = jnp.zeros((4, tm), jnp.int32)
    idx_ref[...] = jnp.concatenate([e1, e2, r1.astype(jnp.int32), r2.astype(jnp.int32), zi], axis=0)
    zf = jnp.zeros((6, tm), F32)
    wc_ref[...] = jnp.transpose(jnp.concatenate([w1, w2, zf], axis=0))


MOE_ROWS = 256
ROW_GROUP = 8


def _slots_kernel(idx_ref, ps_ref, o_ref):
    idx = idx_ref[...]
    tm = idx.shape[1]
    rowx = lax.broadcasted_iota(jnp.int32, (N_EXPERTS, tm), 0)
    ps = ps_ref[:, 0:1]
    out = []
    for k in range(2):
        start = jnp.sum(jnp.where(rowx == idx[k:k + 1], ps, 0), axis=0, keepdims=True)
        out.append(start + idx[2 + k:3 + k])
    o_ref[...] = jnp.concatenate(out + [jnp.zeros((6, tm), jnp.int32)], axis=0)


def _slots(idx, pstart, tm=2048):
    n = idx.shape[1]
    tm = min(tm, n)
    ps = jnp.broadcast_to(pstart[:, None], (N_EXPERTS, LANES))
    return pl.pallas_call(
        _slots_kernel,
        grid=(n // tm,),
        in_specs=[pl.BlockSpec((8, tm), lambda i: (0, i)),
                  pl.BlockSpec((N_EXPERTS, LANES), lambda i: (0, 0))],
        out_specs=pl.BlockSpec((8, tm), lambda i: (0, i)),
        out_shape=jax.ShapeDtypeStruct((8, n), jnp.int32),
        compiler_params=_cparams(("parallel",)),
        name="moe_slots",
    )(idx, ps)


def _dispatch_kernel(pstart_ref, pend_ref, s1_ref, s2_ref, h_ref, xs_out, zbuf, sem, zsem):
    tm = s1_ref.shape[0]
    slot_refs = (s1_ref, s2_ref)

    @pl.when(pl.program_id(0) == 0)
    def _():
        zbuf[...] = jnp.zeros(zbuf.shape, zbuf.dtype)

        def zero_copy(e):
            first = pl.multiple_of(pend_ref[e] - MOE_ROWS, MOE_ROWS)
            return pltpu.make_async_copy(zbuf, xs_out.at[pl.ds(first, MOE_ROWS)], zsem)

        def zstart(e, carry):
            @pl.when(pend_ref[e] > pstart_ref[e])
            def _():
                zero_copy(e).start()
            return carry

        def zwait(e, carry):
            @pl.when(pend_ref[e] > pstart_ref[e])
            def _():
                zero_copy(e).wait()
            return carry

        lax.fori_loop(0, N_EXPERTS, zstart, 0)
        lax.fori_loop(0, N_EXPERTS, zwait, 0)

        def tail_copy(b):
            return pltpu.make_async_copy(
                zbuf, xs_out.at[pl.ds(pl.multiple_of(b * MOE_ROWS, MOE_ROWS), MOE_ROWS)], zsem)

        def tstart(b, carry):
            tail_copy(b).start()
            return carry

        def twait(b, carry):
            tail_copy(b).wait()
            return carry

        first_free = pend_ref[N_EXPERTS - 1] // MOE_ROWS
        n_blocks = xs_out.shape[0] // MOE_ROWS
        lax.fori_loop(first_free, n_blocks, tstart, 0)
        lax.fori_loop(first_free, n_blocks, twait, 0)

    def row_copy(g, u, k):
        slot = slot_refs[k][g * ROW_GROUP + u]
        return pltpu.make_async_copy(h_ref.at[g, pl.ds(u, 1)], xs_out.at[pl.ds(slot, 1)], sem)

    def start(g, carry):
        for u in range(ROW_GROUP):
            row_copy(g, u, 0).start()
            row_copy(g, u, 1).start(priority=1)
        return carry

    def wait(g, carry):
        for u in range(ROW_GROUP):
            row_copy(g, u, 0).wait()
            row_copy(g, u, 1).wait()
        return carry

    lax.fori_loop(0, tm // ROW_GROUP, start, 0)
    lax.fori_loop(0, tm // ROW_GROUP, wait, 0)


def _dispatch(pstart, pend, slots, h2, n_rows, tm=512):
    n, d = h2.shape
    return pl.pallas_call(
        _dispatch_kernel,
        grid_spec=pltpu.PrefetchScalarGridSpec(
            num_scalar_prefetch=2,
            grid=(n // tm,),
            in_specs=[pl.BlockSpec((tm,), lambda i, ps, pe: (i,), memory_space=pltpu.SMEM),
                      pl.BlockSpec((tm,), lambda i, ps, pe: (i,), memory_space=pltpu.SMEM),
                      pl.BlockSpec((tm // ROW_GROUP, ROW_GROUP, d), lambda i, ps, pe: (i, 0, 0))],
            out_specs=pl.BlockSpec(memory_space=pl.ANY),
            scratch_shapes=[pltpu.VMEM((MOE_ROWS, d), h2.dtype), pltpu.SemaphoreType.DMA(()),
                            pltpu.SemaphoreType.DMA(())]),
        out_shape=jax.ShapeDtypeStruct((n_rows, d), h2.dtype),
        compiler_params=_cparams(("arbitrary",)),
        name="moe_dispatch",
    )(pstart, pend, slots[0], slots[1], h2.reshape(n // ROW_GROUP, ROW_GROUP, d))


def _expert_kernel(be_ref, nu_ref, eslot_ref, enext_ref, x_ref, wg_hbm, wu_hbm, wd_hbm, o_ref,
                   wg_f, wu_f, wd_f, wg_b, wu_b, wd_b, sem):
    i = pl.program_id(0)
    used = i < nu_ref[0]
    e = be_ref[i]
    new_expert = (i == 0) | (e != be_ref[jnp.maximum(i - 1, 0)])

    def fetch(expert, slot):
        return [pltpu.make_async_copy(src.at[expert], dst.at[slot], sem.at[slot, k])
                for k, (src, dst) in enumerate(((wg_hbm, wg_f), (wu_hbm, wu_f), (wd_hbm, wd_f)))]

    @pl.when(i == 0)
    def _():
        for c in fetch(e, eslot_ref[e]):
            c.start()

    @pl.when(used & new_expert)
    def _():
        slot = eslot_ref[e]
        for c in fetch(e, slot):
            c.wait()
        nxt = enext_ref[e]

        @pl.when(nxt >= 0)
        def _():
            for c in fetch(nxt, 1 - slot):
                c.start()

        wg_b[...] = wg_f[slot].astype(BF16)
        wu_b[...] = wu_f[slot].astype(BF16)
        wd_b[...] = wd_f[slot].astype(BF16)

    @pl.when(used)
    def _():
        x_lo, x_hi = (t.astype(BF16) for t in _unpack_bf16_pairs(x_ref[...]))
        half = x_lo.shape[1]

        def proj(w_b):
            return (jnp.dot(x_lo, w_b[0:half, :], preferred_element_type=F32)
                    + jnp.dot(x_hi, w_b[half:, :], preferred_element_type=F32))

        a = proj(wg_b)
        u = proj(wu_b)
        hh = (a * _sigmoid(a)) * u
        y = jnp.dot(hh.astype(BF16), wd_b[...], preferred_element_type=F32)
        o_ref[...] = _pack_bf16_pairs(y)

    @pl.when(i >= nu_ref[0])
    def _():
        o_ref[...] = jnp.zeros(o_ref.shape, o_ref.dtype)


def _experts(block_e, n_used, eslot, enext, xs, wg, wu, wd):
    n_rows, dh = xs.shape
    _, d, de = wg.shape
    return pl.pallas_call(
        _expert_kernel,
        grid_spec=pltpu.PrefetchScalarGridSpec(
            num_scalar_prefetch=4,
            grid=(n_rows // MOE_ROWS,),
            in_specs=[pl.BlockSpec((MOE_ROWS, dh), lambda i, be, nu, es, en: (jnp.minimum(i, nu[0] - 1), 0)),
                      pl.BlockSpec(memory_space=pl.ANY),
                      pl.BlockSpec(memory_space=pl.ANY),
                      pl.BlockSpec(memory_space=pl.ANY)],
            out_specs=pl.BlockSpec((MOE_ROWS, dh), lambda i, be, nu, es, en: (i, 0)),
            scratch_shapes=[pltpu.VMEM((2, d, de), F32), pltpu.VMEM((2, d, de), F32),
                            pltpu.VMEM((2, de, d), F32),
                            pltpu.VMEM((d, de), BF16), pltpu.VMEM((d, de), BF16),
                            pltpu.VMEM((de, d), BF16), pltpu.SemaphoreType.DMA((2, 3))]),
        out_shape=jax.ShapeDtypeStruct((n_rows, dh), jnp.uint32),
        compiler_params=_cparams(("arbitrary",)),
        name="moe_experts",
    )(block_e, n_used, eslot, enext, xs, wg, wu, wd)


def _combine_kernel(s1_ref, s2_ref, ys_hbm, x_ref, w_ref, lnf_ref, o_ref, y1, y2, sem, *,
                    final_norm):
    tm = s1_ref.shape[0]
    slot_refs = (s1_ref, s2_ref)

    def row_copy(g, u, k):
        buf = y1 if k == 0 else y2
        slot = slot_refs[k][g * ROW_GROUP + u]
        return pltpu.make_async_copy(ys_hbm.at[pl.ds(slot, 1)], buf.at[g, pl.ds(u, 1)], sem)

    def start(g, carry):
        for u in range(ROW_GROUP):
            row_copy(g, u, 0).start()
            row_copy(g, u, 1).start(priority=1)
        return carry

    def wait(g, carry):
        for u in range(ROW_GROUP):
            row_copy(g, u, 0).wait()
            row_copy(g, u, 1).wait()
        return carry

    lax.fori_loop(0, tm // ROW_GROUP, start, 0)
    lax.fori_loop(0, tm // ROW_GROUP, wait, 0)
    w = w_ref[...]
    dh = y1.shape[2]
    a_lo, a_hi = _unpack_bf16_pairs(y1[...].reshape(tm, dh))
    b_lo, b_hi = _unpack_bf16_pairs(y2[...].reshape(tm, dh))
    moe = jnp.concatenate([w[:, 0:1] * a_lo + w[:, 1:2] * b_lo,
                           w[:, 0:1] * a_hi + w[:, 1:2] * b_hi], axis=1)
    y = x_ref[...] + moe
    o_ref[...] = _rmsnorm_rows(y, lnf_ref[...]) if final_norm else y


def _combine(slots, ys, x1, wc, lnf, final_norm, tm=512):
    n, d = x1.shape
    return pl.pallas_call(
        functools.partial(_combine_kernel, final_norm=final_norm),
        grid=(n // tm,),
        in_specs=[pl.BlockSpec((tm,), lambda i: (i,), memory_space=pltpu.SMEM),
                  pl.BlockSpec((tm,), lambda i: (i,), memory_space=pltpu.SMEM),
                  pl.BlockSpec(memory_space=pl.ANY),
                  pl.BlockSpec((tm, d), lambda i: (i, 0)),
                  pl.BlockSpec((tm, 8), lambda i: (i, 0)),
                  pl.BlockSpec((1, d), lambda i: (0, 0))],
        out_specs=pl.BlockSpec((tm, d), lambda i: (i, 0)),
        scratch_shapes=[pltpu.VMEM((tm // ROW_GROUP, ROW_GROUP, d // 2), jnp.uint32),
                        pltpu.VMEM((tm // ROW_GROUP, ROW_GROUP, d // 2), jnp.uint32),
                        pltpu.SemaphoreType.DMA(())],
        out_shape=jax.ShapeDtypeStruct((n, d), F32),
        compiler_params=_cparams(("arbitrary",)),
        name="moe_combine",
    )(slots[0], slots[1], ys, x1, wc, lnf)


def _moe(h2, idx, wc, counts, wg, wu, wd, x1, lnf, final_norm):
    n = h2.shape[0]
    n_rows = 2 * n + N_EXPERTS * MOE_ROWS
    cnt = counts[:, 0].astype(jnp.int32)
    padded = ((cnt + MOE_ROWS - 1) // MOE_ROWS) * MOE_ROWS
    pend = jnp.cumsum(padded)
    pstart = pend - padded
    blk_first = jnp.arange(n_rows // MOE_ROWS, dtype=jnp.int32) * MOE_ROWS
    block_e = jnp.minimum(jnp.sum(pend[None, :] <= blk_first[:, None], axis=1), N_EXPERTS - 1)
    n_used = (pend[-1:] // MOE_ROWS).astype(jnp.int32)
    has = padded > 0
    eslot = ((jnp.cumsum(has) - 1) % 2).astype(jnp.int32)
    ee = jnp.arange(N_EXPERTS, dtype=jnp.int32)
    later = jnp.where((ee[None, :] > ee[:, None]) & has[None, :], ee[None, :], N_EXPERTS)
    enext = jnp.min(later, axis=1)
    enext = jnp.where(enext == N_EXPERTS, -1, enext).astype(jnp.int32)
    slots = _slots(idx, pstart)
    xs = _dispatch(pstart, pend, slots, h2, n_rows)
    ys = _experts(block_e.astype(jnp.int32), n_used, eslot, enext, xs, wg, wu, wd)
    return _combine(slots, ys, x1, wc, lnf, final_norm)


def _pad_rows(a, before, total):
    return jnp.pad(a, ((before, total - before - a.shape[0]), (0, 0)))


def _layer(xf, batch, seq_len, ln1_w, w_in, rw_mu, rw_w0_f, rw_w2_f, rw_w0_b, rw_w2_b, rw_a0_f,
           rw_a2_f, rw_a0_b, rw_a2_b, rw_g2, rw_k_k, rw_k_a, rw_r_k, rw_ln_w, rw_ln_b, gla_gw2_f,
           gla_gb_f, gla_gw2_b, gla_gb_b, gla_norm_w, w_out, ln2_w, moe_w_coarse, moe_b_coarse,
           moe_w_fine, moe_b_fine, moe_w_gate, moe_w_up, moe_w_down):
    d = xf.shape[1]
    rw_cols = 3 * RWKV_WIDTH + 4 * LORA + GATE_LORA
    w_rw, w_gla = w_in[:, :rw_cols], w_in[:, rw_cols:]
    o = 0
    w_gq = w_gla[:, o:o + GLA_K_WIDTH]; o += GLA_K_WIDTH
    w_gk = w_gla[:, o:o + GLA_K_WIDTH]; o += GLA_K_WIDTH
    w_gv = w_gla[:, o:o + GLA_V_WIDTH]; o += GLA_V_WIDTH
    w_gg = w_gla[:, o:o + 2 * GLA_GATE_LORA]; o += 2 * GLA_GATE_LORA
    w_og = w_gla[:, o:o + GLA_V_WIDTH]
    w_gg = jnp.pad(w_gg, ((0, 0), (0, LANES - 2 * GLA_GATE_LORA)))
    w_pad = jnp.concatenate([w_rw, w_gg, w_gq, w_gk, w_gv, w_og], axis=1).astype(BF16)
    mu_pad = jnp.pad(rw_mu, (0, NP - rw_cols))[None, :]

    p = _inproj(xf, ln1_w[None, :], w_pad, mu_pad, seq_len, rw_cols)

    head = lax.broadcasted_iota(jnp.int32, (RWKV_WIDTH, RWKV_WIDTH), 0) // RWKV_HEAD_DIM
    blk = (head == head.T).astype(BF16)
    w0 = jnp.stack([rw_w0_f, rw_w0_b])[:, None, :]
    a0 = jnp.stack([rw_a0_f, rw_a0_b])[:, None, :]
    w2 = jnp.stack([_pad_rows(rw_w2_f, 0, LANES), _pad_rows(rw_w2_b, LORA, LANES)]).astype(BF16)
    a2 = jnp.stack([_pad_rows(rw_a2_f, 0, LANES), _pad_rows(rw_a2_b, LORA, LANES)]).astype(BF16)
    yf, yb, bdf, bdb = _rwkv_scan(p, batch, seq_len, w0, w2, a0, a2, rw_k_k[None, :],
                                  rw_k_a[None, :], rw_r_k.reshape(1, -1), blk)

    gw = jnp.stack([_pad_rows(gla_gw2_f, 0, LANES),
                    _pad_rows(gla_gw2_b, GLA_GATE_LORA, LANES)]).astype(BF16)
    gbias = jnp.stack([gla_gb_f, gla_gb_b])[:, None, :]
    of, ob = _gla_scan(p, batch, seq_len, gw, gbias)

    wr = jnp.concatenate([moe_w_coarse.T, jnp.zeros((8 - N_GROUPS, d), F32), moe_w_fine.T], axis=0)
    br = jnp.concatenate([moe_b_coarse, jnp.zeros((8 - N_GROUPS,), F32), moe_b_fine])[:, None]
    x1, h2, idx, wc, counts = _mix_out(
        xf, p, yf, yb, bdf, bdb, of, ob, rw_g2.astype(BF16), rw_ln_w[None, :], rw_ln_b[None, :], blk,
        gla_norm_w[None, :], w_out.astype(BF16), ln2_w[None, :], wr, br)
    return h2, idx, wc, counts, x1


def kernel(x, ln1_w, w_in, rw_mu, rw_w0_f, rw_w2_f, rw_w0_b, rw_w2_b, rw_a0_f, rw_a2_f, rw_a0_b, rw_a2_b, rw_g2, rw_k_k, rw_k_a, rw_r_k, rw_ln_w, rw_ln_b, gla_gw2_f, gla_gb_f, gla_gw2_b, gla_gb_b, gla_norm_w, w_out, ln2_w, moe_w_coarse, moe_b_coarse, moe_w_fine, moe_b_fine, moe_w_gate, moe_w_up, moe_w_down, ln_f_w):
    batch, seq_len, d = x.shape
    xf = x.reshape(batch * seq_len, d)
    depth = w_in.shape[0]
    for l in range(depth):
        h2, idx, wc, counts, x1 = _layer(
            xf, batch, seq_len, ln1_w[l], w_in[l], rw_mu[l], rw_w0_f[l], rw_w2_f[l], rw_w0_b[l],
            rw_w2_b[l], rw_a0_f[l], rw_a2_f[l], rw_a0_b[l], rw_a2_b[l], rw_g2[l], rw_k_k[l],
            rw_k_a[l], rw_r_k[l], rw_ln_w[l], rw_ln_b[l], gla_gw2_f[l], gla_gb_f[l], gla_gw2_b[l],
            gla_gb_b[l], gla_norm_w[l], w_out[l], ln2_w[l], moe_w_coarse[l], moe_b_coarse[l],
            moe_w_fine[l], moe_b_fine[l], moe_w_gate[l], moe_w_up[l], moe_w_down[l])
        xf = _moe(h2, idx, wc, counts, moe_w_gate[l], moe_w_up[l], moe_w_down[l], x1,
                  ln_f_w[None, :], l == depth - 1)
    return xf.reshape(batch, seq_len, d)
```

```python
import functools

import numpy as np
import jax
import jax.numpy as jnp
from jax import lax
from jax.experimental import pallas as pl
from jax.experimental.pallas import tpu as pltpu

F32 = jnp.float32
BF16 = jnp.bfloat16

RMS_EPS = 1e-6
RWKV_GN_EPS = 64e-5
RWKV_WIDTH = 512
RWKV_HEAD_DIM = 64
LORA = 64
GATE_LORA = 128
GLA_HEADS = 4
GLA_DK = 64
GLA_DV = 128
GLA_K_WIDTH = GLA_HEADS * GLA_DK
GLA_V_WIDTH = GLA_HEADS * GLA_DV
GLA_GATE_LORA = 16
GLA_TAU = 16.0
N_GROUPS = 4
EXPERTS_PER_GROUP = 8
N_EXPERTS = N_GROUPS * EXPERTS_PER_GROUP

CHUNK = 64
LANES = 128

COL_R, COL_K, COL_V = 0, 512, 1024
COL_LORA = 1536
COL_GL = 1792
COL_GG = 1920
COL_GQ, COL_GK, COL_GV, COL_OG = 2048, 2304, 2560, 3072
NP = 3584

VMEM_LIMIT = 56 * 1024 * 1024


def _cparams(sem):
    return pltpu.CompilerParams(dimension_semantics=sem, vmem_limit_bytes=VMEM_LIMIT)


MM_DTYPE = BF16


def _mm(a, b, ca=1, cb=0):
    dt = MM_DTYPE
    return lax.dot_general(a.astype(dt), b.astype(dt), (((ca,), (cb,)), ((), ())),
                           preferred_element_type=F32)


def _split2(x):
    hi = x.astype(BF16)
    lo = (x - hi.astype(F32)).astype(BF16)
    return hi, lo


def _split3(x):
    hi = x.astype(BF16)
    r1 = x - hi.astype(F32)
    mid = r1.astype(BF16)
    lo = (r1 - mid.astype(F32)).astype(BF16)
    return hi, mid, lo


def _mm_ones(x, ones_bf16, left=False, terms=3):
    parts = _split3(x)[:terms]
    prods = [_mm(ones_bf16, t) if left else _mm(t, ones_bf16) for t in parts]
    out = prods[-1]
    for t in prods[-2::-1]:
        out = out + t
    return out


def _mm_x3(a, b, ca=1, cb=0):
    ah, al = _split2(a)
    bh, bl = _split2(b)
    return (_mm(al, bh, ca, cb) + _mm(ah, bl, ca, cb)) + _mm(ah, bh, ca, cb)


def _pack_bf16_pairs(x):
    half = x.shape[1] // 2
    bits = lax.bitcast_convert_type(x.astype(BF16).astype(F32), jnp.uint32)
    return (bits[:, :half] >> 16) | (bits[:, half:] & jnp.uint32(0xFFFF0000))


def _unpack_bf16_pairs(w):
    lo = lax.bitcast_convert_type(w << 16, F32)
    hi = lax.bitcast_convert_type(w & jnp.uint32(0xFFFF0000), F32)
    return lo, hi


def _rmsnorm_rows(x, w):
    ms = jnp.mean(x * x, axis=-1, keepdims=True)
    return x * lax.rsqrt(ms + RMS_EPS) * w


def _softplus(x):
    return jnp.maximum(x, 0.0) + jnp.log(1.0 + jnp.exp(-jnp.abs(x)))


def _sigmoid(x):
    return 1.0 / (1.0 + jnp.exp(-x))


def _seg_cumsum(x, seg, rev):
    n = x.shape[0]
    pos = lax.broadcasted_iota(jnp.int32, x.shape, 0) & (seg - 1)
    sh = 1
    while sh < seg:
        if not rev:
            x = x + jnp.where(pos >= sh, pltpu.roll(x, sh, 0), 0.0)
        else:
            x = x + jnp.where(pos < seg - sh, pltpu.roll(x, n - sh, 0), 0.0)
        sh *= 2
    return x


def _blockdiag2(y, m0):
    return jnp.concatenate([jnp.where(m0, y, 0.0), jnp.where(m0, 0.0, y)], axis=0)


def _inproj_kernel(x_ref, xp_ref, xn_ref, lnw_ref, w_ref, mu_ref, o_ref, *, tiles_per_seq,
                   shift_cols, tn):
    i = pl.program_id(0)
    lnw = lnw_ref[...]
    h = _rmsnorm_rows(x_ref[...], lnw).astype(BF16)
    t = i % tiles_per_seq
    hp = jnp.where(t == 0, 0.0, _rmsnorm_rows(xp_ref[...], lnw))
    hn = jnp.where(t == tiles_per_seq - 1, 0.0, _rmsnorm_rows(xn_ref[...], lnw))
    hh = jnp.concatenate([hp, hn], axis=0).astype(BF16)
    tm = h.shape[0]
    npad = w_ref.shape[1]
    for c0 in range(0, npad, tn):
        c1 = c0 + tn
        w = w_ref[:, c0:c1]
        p = jnp.dot(h, w, preferred_element_type=F32)
        if c0 < shift_cols:
            ph = jnp.dot(hh, w, preferred_element_type=F32)
            row = lax.broadcasted_iota(jnp.int32, p.shape, 0)
            prev = jnp.where(row == 0, ph[7:8, :], pltpu.roll(p, 1, 0))
            nxt = jnp.where(row == tm - 1, ph[8:9, :], pltpu.roll(p, tm - 1, 0))
            p = p + mu_ref[:, c0:c1] * (0.5 * (prev + nxt) - p)
        o_ref[:, c0:c1] = p


def _inproj(xf, lnw, w_pad, mu_pad, seq_len, shift_cols, tm=512, tn=512):
    n, d = xf.shape
    npad = w_pad.shape[1]
    assert seq_len % tm == 0 and npad % tn == 0
    tps = seq_len // tm
    nb8 = n // 8
    return pl.pallas_call(
        functools.partial(_inproj_kernel, tiles_per_seq=tps, shift_cols=shift_cols, tn=tn),
        grid=(n // tm,),
        in_specs=[
            pl.BlockSpec((tm, d), lambda i: (i, 0)),
            pl.BlockSpec((8, d), lambda i: (jnp.maximum(i * (tm // 8) - 1, 0), 0)),
            pl.BlockSpec((8, d), lambda i: (jnp.minimum((i + 1) * (tm // 8), nb8 - 1), 0)),
            pl.BlockSpec((1, d), lambda i: (0, 0)),
            pl.BlockSpec((d, npad), lambda i: (0, 0)),
            pl.BlockSpec((1, npad), lambda i: (0, 0)),
        ],
        out_specs=pl.BlockSpec((tm, npad), lambda i: (i, 0)),
        out_shape=jax.ShapeDtypeStruct((n, npad), F32),
        compiler_params=_cparams(("parallel",)),
        name="inproj_shift",
    )(xf, xf, xf, lnw, w_pad, mu_pad)


RWKV_CPS = 2


def _rwkv_prep(d, r_ref, k_ref, v_ref, l_ref, w0_ref, w2_ref, a0_ref, a2_ref,
               kk_ref, ka_ref, rk_ref, blk_ref, bd_ref):
    rev = d == 1
    c = CHUNK
    r = r_ref[...]
    k = k_ref[...]
    lo = l_ref[...]
    blk = blk_ref[...]

    zw = w0_ref[d] + _mm(jnp.tanh(lo[:, 0:LANES]), w2_ref[d])
    lw = -jnp.exp(-_softplus(-zw) - 0.5)
    a = _sigmoid(a0_ref[d] + _mm(lo[:, LANES:2 * LANES], a2_ref[d]))
    kkr = k * kk_ref[...]
    ss = _mm_ones(kkr * kkr, blk, terms=1)
    kk = kkr / jnp.maximum(jnp.sqrt(ss), 1e-12)
    kd = k * (1.0 + (a - 1.0) * ka_ref[...])
    b = kk * a
    bd_ref[...] = _mm_ones(r * kd * rk_ref[...], blk, terms=2)

    cum = _seg_cumsum(lw, c, rev)
    n_blk = r.shape[0] // c
    tot = [cum[j * c:j * c + 1, :] if rev else cum[j * c + c - 1:j * c + c, :] for j in range(n_blk)]
    ginv = jnp.exp(-cum)
    gend = jnp.concatenate([jnp.exp(tot[j] - cum[j * c:(j + 1) * c]) for j in range(n_blk)], axis=0)
    return dict(at=-kk * jnp.exp(cum - lw), rt=r * jnp.exp(cum), bt=b * ginv, kt=kd * ginv,
                bh=b * gend, kh=kd * gend, gc=[jnp.exp(t) for t in tot], v=v_ref[...])


def _rwkv_kernel(rf, kf, vf, lf, rb, kb, vb, lb, w0, w2, a0, a2, kk, ka, rk, blk,
                 yf, yb, bdf, bdb, m_scr):
    @pl.when(pl.program_id(1) == 0)
    def _():
        m_scr[...] = jnp.zeros(m_scr.shape, F32)

    c = CHUNK
    n_blk = rf.shape[0] // c
    n_pair = RWKV_WIDTH // LANES
    prep = [_rwkv_prep(0, rf, kf, vf, lf, w0, w2, a0, a2, kk, ka, rk, blk, bdf),
            _rwkv_prep(1, rb, kb, vb, lb, w0, w2, a0, a2, kk, ka, rk, blk, bdb)]
    y_refs = [yf, yb]

    row = lax.broadcasted_iota(jnp.int32, (c, LANES), 0)
    lane = lax.broadcasted_iota(jnp.int32, (c, LANES), 1)
    s_idx = lane & (RWKV_HEAD_DIM - 1)
    m0 = lane < RWKV_HEAD_DIM
    strict = [s_idx < row, s_idx > row]
    incl = [s_idx <= row, s_idx >= row]
    diag = s_idx == row

    def bd2(y):
        return _blockdiag2(y, m0)

    def fold(z):
        return jnp.where(m0, z[:c], 0.0) + jnp.where(m0, 0.0, z[c:])

    chains = [(d, p, j) for j in range(n_blk) for p in range(n_pair) for d in (0, 1)]
    ds = [d for d, _, _ in chains]

    def get(name):
        return [prep[d][name][j * c:(j + 1) * c, p * LANES:(p + 1) * LANES] for d, p, j in chains]

    at, rt, bt, kt, bh, kh, vp = (get(n) for n in ("at", "rt", "bt", "kt", "bh", "kh", "v"))
    gc = [prep[d]["gc"][j][:, p * LANES:(p + 1) * LANES] for d, p, j in chains]
    x2 = [jnp.concatenate([a_, r_], axis=0) for a_, r_ in zip(at, rt)]
    sbk = [_mm(x, jnp.concatenate([bd2(b_), bd2(k_)], axis=0), 1, 1) for x, b_, k_ in zip(x2, bt, kt)]
    sb = [z[:, :LANES] for z in sbk]
    sk = [z[:, LANES:] for z in sbk]
    lab = [jnp.where(strict[d], z[:c], 0.0) for d, z in zip(ds, sb)]
    prb = [jnp.where(incl[d], z[c:], 0.0) for d, z in zip(ds, sb)]
    lak = [jnp.where(strict[d], z[:c], 0.0) for d, z in zip(ds, sk)]
    prk = [jnp.where(incl[d], z[c:], 0.0) for d, z in zip(ds, sk)]
    bdv = [bd2(y) for y in vp]
    xa = at
    xvpv = [_mm(jnp.concatenate([x, z], axis=0), y) for x, z, y in zip(lak, prk, bdv)]
    xv = [t[:c] for t in xvpv]
    pv = [t[c:] for t in xvpv]
    lp = lab
    n_it = 6
    for it in range(n_it):
        upd = [_mm(l_, jnp.concatenate([bd2(a_), bd2(v_)], axis=1)) for l_, a_, v_ in zip(lp, xa, xv)]
        xa = [a_ + u[:, :LANES] for a_, u in zip(xa, upd)]
        xv = [v_ + u[:, LANES:] for v_, u in zip(xv, upd)]
        if it < n_it - 1:
            lp = [_mm(l_, bd2(l_)) for l_ in lp]
    px = [_mm(x, jnp.concatenate([bd2(a_), bd2(v_)], axis=1)) for x, a_, v_ in zip(prb, xa, xv)]
    zb = [_mm(x, jnp.concatenate([a_, v_], axis=1), 0, 0) for x, a_, v_ in zip(bh, xa, xv)]
    zk = [_mm(x, y, 0, 0) for x, y in zip(kh, vp)]
    qg, yl, hh = {}, {}, {}
    for i, key in enumerate(chains):
        qp = rt[i] + px[i][:, :LANES]
        g = jnp.where(diag, gc[i], 0.0) + fold(zb[i][:, :LANES])
        qg[key] = jnp.concatenate([qp, g], axis=0)
        yl[key] = px[i][:, LANES:] + pv[i]
        hh[key] = fold(zb[i][:, LANES:]) + fold(zk[i])
    state = {(d, p): m_scr[d, p] for p in range(n_pair) for d in (0, 1)}
    for step in range(n_blk):
        for p in range(n_pair):
            for d in (0, 1):
                j = n_blk - 1 - step if d == 1 else step
                yg = _mm(qg[(d, p, j)], bd2(state[(d, p)]))
                y_refs[d][j * c:(j + 1) * c, p * LANES:(p + 1) * LANES] = yg[:c] + yl[(d, p, j)]
                state[(d, p)] = yg[c:] + hh[(d, p, j)]
    for (d, p), m in state.items():
        m_scr[d, p] = m


def _rwkv_scan(p, batch, seq_len, w0, w2, a0, a2, kk, ka, rk, blk):
    n = p.shape[0]
    rows = RWKV_CPS * CHUNK
    assert seq_len % rows == 0
    ns = seq_len // rows
    w = RWKV_WIDTH

    def fwd(col, width):
        return pl.BlockSpec((rows, width), lambda b, c: (b * ns + c, col // width))

    def bwd(col, width):
        return pl.BlockSpec((rows, width), lambda b, c: (b * ns + ns - 1 - c, col // width))

    def full(a):
        return pl.BlockSpec(a.shape, lambda b, c: (0,) * a.ndim)

    lora_w = 2 * LANES
    in_specs = [fwd(COL_R, w), fwd(COL_K, w), fwd(COL_V, w), fwd(COL_LORA, lora_w),
                bwd(COL_R, w), bwd(COL_K, w), bwd(COL_V, w), bwd(COL_LORA, lora_w),
                full(w0), full(w2), full(a0), full(a2), full(kk), full(ka), full(rk), full(blk)]
    ospec_f = pl.BlockSpec((rows, w), lambda b, c: (b * ns + c, 0))
    ospec_b = pl.BlockSpec((rows, w), lambda b, c: (b * ns + ns - 1 - c, 0))
    osh = jax.ShapeDtypeStruct((n, w), F32)
    return pl.pallas_call(
        _rwkv_kernel,
        grid=(batch, ns),
        in_specs=in_specs,
        out_specs=[ospec_f, ospec_b, ospec_f, ospec_b],
        out_shape=[osh, osh, osh, osh],
        scratch_shapes=[pltpu.VMEM((2, w // LANES, CHUNK, LANES), F32)],
        compiler_params=_cparams(("parallel", "arbitrary")),
        name="rwkv7_scan",
    )(p, p, p, p, p, p, p, p, w0, w2, a0, a2, kk, ka, rk, blk)


GLA_CPS = 4


def _gla_gates(d, g_ref, gw_ref, gb_ref, cum_ref):
    c = CHUNK
    xg = _mm(g_ref[...], gw_ref[d]) + gb_ref[d]
    g = -_softplus(-xg) / GLA_TAU
    out = []
    for j in range(g.shape[0] // c):
        hi, mid, _ = _split3(g[j * c:(j + 1) * c])
        out.append(_mm(cum_ref[d], jnp.concatenate([hi, mid], axis=0)))
    return out


def _gla_kernel(gf, qf, kf, vf, gb_, qb, kb, vb, gw, gbias, cum, of, ob, st_scr):
    @pl.when(pl.program_id(1) == 0)
    def _():
        st_scr[...] = jnp.zeros(st_scr.shape, F32)

    c = CHUNK
    n_blk = qf.shape[0] // c
    n_pair = GLA_K_WIDTH // LANES
    o_refs = [of, ob]
    cds = [_gla_gates(0, gf, gw, gbias, cum), _gla_gates(1, gb_, gw, gbias, cum)]
    qs = [qf[...] * (GLA_DK ** -0.5), qb[...] * (GLA_DK ** -0.5)]
    ks = [kf[...], kb[...]]
    vs = [vf[...], vb[...]]

    row = lax.broadcasted_iota(jnp.int32, (c, LANES), 0)
    lane = lax.broadcasted_iota(jnp.int32, (c, LANES), 1)
    s_idx = lane & (GLA_DK - 1)
    m0 = lane < GLA_DK
    diag = row == s_idx
    wide_row = lax.broadcasted_iota(jnp.int32, (c, GLA_K_WIDTH), 0)
    wide_prog = [wide_row, c - 1 - wide_row]

    chains = [(d, p, j) for j in range(n_blk) for p in range(n_pair) for d in (0, 1)]

    def blk_rows(x, j):
        return x[j * c:(j + 1) * c]

    def pair(x, p):
        return x[:, p * LANES:(p + 1) * LANES]

    def nt(x, y):
        return _mm(x, _blockdiag2(y, m0), 1, 1)

    qb_ = {(d, j): blk_rows(qs[d], j) for d in (0, 1) for j in range(n_blk)}
    kb_ = {(d, j): blk_rows(ks[d], j) for d in (0, 1) for j in range(n_blk)}
    a = [jnp.where(diag, nt(pair(qb_[(d, j)], p), pair(kb_[(d, j)], p)), 0.0) for d, p, j in chains]
    s = 1
    lvl = 1
    while s < c:
        same = (row // (2 * s)) == (s_idx // (2 * s))
        ql, kl = {}, {}
        for d in (0, 1):
            upper = (wide_prog[d] & (2 * s - 1)) >= s
            for j in range(n_blk):
                e = jnp.exp(-jnp.abs(cds[d][j][lvl * c:(lvl + 1) * c]))
                ql[(d, j)] = qb_[(d, j)] * jnp.where(upper, e, 0.0)
                kl[(d, j)] = kb_[(d, j)] * jnp.where(upper, 0.0, e)
        upd = [jnp.where(same, nt(pair(ql[(d, j)], p), pair(kl[(d, j)], p)), 0.0)
               for d, p, j in chains]
        a = [x + y for x, y in zip(a, upd)]
        s *= 2
        lvl += 1

    qt, kh, gc = {}, {}, {}
    for d in (0, 1):
        for j in range(n_blk):
            bc = cds[d][j][0:c]
            tot = bc[0:1, :] if d == 1 else bc[c - 1:c, :]
            qt[(d, j)] = qb_[(d, j)] * jnp.exp(bc)
            kh[(d, j)] = kb_[(d, j)] * jnp.exp(tot - bc)
            gc[(d, j)] = jnp.exp(tot)

    def vhead(d, j, h):
        return vs[d][j * c:(j + 1) * c, h * GLA_DV:(h + 1) * GLA_DV]

    oa = {key: _mm(_blockdiag2(a[i], m0),
                   jnp.concatenate([vhead(key[0], key[2], 2 * key[1]),
                                    vhead(key[0], key[2], 2 * key[1] + 1)], axis=0))
          for i, key in enumerate(chains)}
    z = {(d, p, j): _mm(jnp.concatenate([vhead(d, j, 2 * p), vhead(d, j, 2 * p + 1)], axis=1),
                        pair(kh[(d, j)], p), 0, 0)
         for d, p, j in chains}
    mv = lax.broadcasted_iota(jnp.int32, (GLA_DV, LANES), 1) < GLA_DK
    state = {(d, p): st_scr[d, p] for p in range(n_pair) for d in (0, 1)}
    for step in range(n_blk):
        for p in range(n_pair):
            for d in (0, 1):
                j = n_blk - 1 - step if d == 1 else step
                st = state[(d, p)]
                o = _mm(_blockdiag2(pair(qt[(d, j)], p), m0), st, 1, 1) + oa[(d, p, j)]
                o_refs[d][j * c:(j + 1) * c, (2 * p) * GLA_DV:(2 * p + 1) * GLA_DV] = o[:c]
                o_refs[d][j * c:(j + 1) * c, (2 * p + 1) * GLA_DV:(2 * p + 2) * GLA_DV] = o[c:]
                zz = z[(d, p, j)]
                state[(d, p)] = (st * pair(gc[(d, j)], p) + jnp.where(mv, zz[:GLA_DV], 0.0)
                                 + jnp.where(mv, 0.0, zz[GLA_DV:]))
    for (d, p), st in state.items():
        st_scr[d, p] = st


def _gla_cumsum_matrices():
    c = CHUNK
    i = np.arange(c)
    out = []
    for rev in (False, True):
        tri = (i[None, :] >= i[:, None]) if rev else (i[None, :] <= i[:, None])
        tri = tri.astype(np.float32)
        mats = [tri]
        s = 1
        while s < c:
            ref = (i // (2 * s)) * (2 * s) + (s if rev else s - 1)
            mats.append(tri - tri[ref])
            s *= 2
        out.append(np.concatenate(mats, axis=0))
    mats = np.stack(out)
    return jnp.asarray(np.concatenate([mats, mats], axis=2), dtype=BF16)


def _gla_scan(p, batch, seq_len, gw, gbias):
    n = p.shape[0]
    rows = GLA_CPS * CHUNK
    assert seq_len % rows == 0
    nc = seq_len // rows

    def fwd(col, width):
        return pl.BlockSpec((rows, width), lambda b, c: (b * nc + c, col // width))

    def bwd(col, width):
        return pl.BlockSpec((rows, width), lambda b, c: (b * nc + nc - 1 - c, col // width))

    def full(a):
        return pl.BlockSpec(a.shape, lambda b, c: (0,) * a.ndim)

    kw, vw = GLA_K_WIDTH, GLA_V_WIDTH
    cum = _gla_cumsum_matrices()
    in_specs = [fwd(COL_GG, LANES), fwd(COL_GQ, kw), fwd(COL_GK, kw), fwd(COL_GV, vw),
                bwd(COL_GG, LANES), bwd(COL_GQ, kw), bwd(COL_GK, kw), bwd(COL_GV, vw),
                full(gw), full(gbias), full(cum)]
    ospec_f = pl.BlockSpec((rows, vw), lambda b, c: (b * nc + c, 0))
    ospec_b = pl.BlockSpec((rows, vw), lambda b, c: (b * nc + nc - 1 - c, 0))
    osh = jax.ShapeDtypeStruct((n, vw), F32)
    return pl.pallas_call(
        _gla_kernel,
        grid=(batch, nc),
        in_specs=in_specs,
        out_specs=[ospec_f, ospec_b],
        out_shape=[osh, osh],
        scratch_shapes=[pltpu.VMEM((2, kw // LANES, GLA_DV, LANES), F32)],
        compiler_params=_cparams(("parallel", "arbitrary")),
        name="gla_scan",
    )(p, p, p, p, p, p, p, p, gw, gbias, cum)


MIX_ROWS = 256


def _mix_out_kernel(x_ref, yf, yb, bdf, bdb, v_ref, gl_ref, of, ob, og_ref,
                    g2_ref, lnw_ref, lnb_ref, blk_ref, nw_ref, wo_ref, ln2_ref, wr_ref, br_ref,
                    o_ref, h_ref, idx_ref, wc_ref, cnt_ref, base):
    blk = blk_ref[...]
    inv = 1.0 / RWKV_HEAD_DIM
    tm = x_ref.shape[0]
    rows = [slice(r0, r0 + MIX_ROWS) for r0 in range(0, tm, MIX_ROWS)]
    y = [yf[r] + yb[r] for r in rows]
    mean = [_mm_ones(t, blk, terms=2) * inv for t in y]
    yc = [a - b for a, b in zip(y, mean)]
    var = [_mm_ones(t * t, blk, terms=1) * inv for t in yc]
    gate = [_mm(_sigmoid(gl_ref[r]), g2_ref[...]) for r in rows]
    y_rw = [(c * lax.rsqrt(s + RWKV_GN_EPS) * lnw_ref[...] + lnb_ref[...]
             + (bdf[r] + bdb[r]) * v_ref[r]) * g
            for c, s, g, r in zip(yc, var, gate, rows)]
    acc = [x_ref[r] + _mm(t, wo_ref[0:RWKV_WIDTH, :]) for t, r in zip(y_rw, rows)]
    nw = nw_ref[...]
    for h in range(GLA_HEADS):
        sl = slice(h * GLA_DV, (h + 1) * GLA_DV)
        w_h = wo_ref[RWKV_WIDTH + h * GLA_DV:RWKV_WIDTH + (h + 1) * GLA_DV, :]
        for i, r in enumerate(rows):
            oh = of[r, sl] + ob[r, sl]
            og = og_ref[r, sl]
            ms = jnp.mean(oh * oh, axis=-1, keepdims=True)
            yg = oh * lax.rsqrt(ms + RMS_EPS) * nw[:, sl] * (og * _sigmoid(og))
            acc[i] = acc[i] + _mm(yg, w_h)
    x1 = jnp.concatenate(acc, axis=0)
    o_ref[...] = x1
    _route(x1, ln2_ref, wr_ref, br_ref, h_ref, idx_ref, wc_ref, cnt_ref, base)


def _mix_out(xf, p, yf, yb, bdf, bdb, of, ob, g2, lnw, lnb, blk, nw, wo, ln2, wr, br, tm=512):
    n, d = xf.shape
    w = RWKV_WIDTH

    def rows(width, col=0):
        return pl.BlockSpec((tm, width), lambda i: (i, col // width))

    def full(a):
        return pl.BlockSpec(a.shape, lambda i: (0,) * a.ndim)

    in_specs = [rows(d), rows(w), rows(w), rows(w), rows(w), rows(w, COL_V), rows(LANES, COL_GL),
                rows(w), rows(w), rows(w, COL_OG),
                full(g2), full(lnw), full(lnb), full(blk), full(nw), full(wo),
                full(ln2), full(wr), full(br)]
    return pl.pallas_call(
        _mix_out_kernel,
        grid=(n // tm,),
        in_specs=in_specs,
        out_specs=[rows(d), rows(d // 2),
                   pl.BlockSpec((8, tm), lambda i: (0, i)),
                   pl.BlockSpec((tm, 8), lambda i: (i, 0)),
                   pl.BlockSpec((N_EXPERTS, LANES), lambda i: (0, 0))],
        out_shape=[jax.ShapeDtypeStruct((n, d), F32),
                   jax.ShapeDtypeStruct((n, d // 2), jnp.uint32),
                   jax.ShapeDtypeStruct((8, n), jnp.int32),
                   jax.ShapeDtypeStruct((n, 8), F32),
                   jax.ShapeDtypeStruct((N_EXPERTS, LANES), F32)],
        scratch_shapes=[pltpu.VMEM((N_EXPERTS, LANES), F32)],
        compiler_params=_cparams(("arbitrary",)),
        name="mix_outproj_router",
    )(xf, yf, yb, bdf, bdb, p, p, of, ob, p, g2, lnw, lnb, blk, nw, wo, ln2, wr, br)


def _route(x1, lnw_ref, wr_ref, br_ref, h_ref, idx_ref, wc_ref, cnt_ref, base):
    @pl.when(pl.program_id(0) == 0)
    def _():
        base[...] = jnp.zeros(base.shape, F32)

    h = _rmsnorm_rows(x1, lnw_ref[...])
    h_ref[...] = _pack_bf16_pairs(h)
    logits = _mm_x3(wr_ref[...], h, 1, 1) + br_ref[...]
    tm = logits.shape[1]
    coarse = logits[0:N_GROUPS, :]
    fine = logits[8:8 + N_EXPERTS, :]
    rowg = lax.broadcasted_iota(jnp.int32, (N_GROUPS, tm), 0)
    cmax = jnp.max(coarse, axis=0, keepdims=True)
    gsel = jnp.min(jnp.where(coarse == cmax, rowg, N_GROUPS), axis=0, keepdims=True)
    pg = 1.0 / jnp.sum(jnp.exp(coarse - cmax), axis=0, keepdims=True)
    sel = jnp.zeros((EXPERTS_PER_GROUP, tm), F32)
    for g in range(N_GROUPS):
        sel = sel + jnp.where(gsel == g, fine[g * EXPERTS_PER_GROUP:(g + 1) * EXPERTS_PER_GROUP, :], 0.0)
    rowe = lax.broadcasted_iota(jnp.int32, (EXPERTS_PER_GROUP, tm), 0)
    l1 = jnp.max(sel, axis=0, keepdims=True)
    i1 = jnp.min(jnp.where(sel == l1, rowe, EXPERTS_PER_GROUP), axis=0, keepdims=True)
    sel2 = jnp.where(rowe == i1, -jnp.inf, sel)
    l2 = jnp.max(sel2, axis=0, keepdims=True)
    i2 = jnp.min(jnp.where(sel2 == l2, rowe, EXPERTS_PER_GROUP), axis=0, keepdims=True)
    t = jnp.exp(l2 - l1)
    w1 = pg / (1.0 + t)
    w2 = pg * t / (1.0 + t)
    e1 = gsel * EXPERTS_PER_GROUP + i1
    e2 = gsel * EXPERTS_PER_GROUP + i2
    rowx = lax.broadcasted_iota(jnp.int32, (N_EXPERTS, tm), 0)
    hit1 = rowx == e1
    hit2 = rowx == e2
    oh = jnp.where(hit1 | hit2, 1.0, 0.0)
    tr = lax.broadcasted_iota(jnp.int32, (tm, tm), 0)
    tc = lax.broadcasted_iota(jnp.int32, (tm, tm), 1)
    before = jnp.where(tr < tc, 1.0, 0.0).astype(BF16)
    pos = base[:, 0:1] + _mm(oh, before)
    r1 = jnp.sum(jnp.where(hit1, pos, 0.0), axis=0, keepdims=True)
    r2 = jnp.sum(jnp.where(hit2, pos, 0.0), axis=0, keepdims=True)
    new_base = base[...] + jnp.sum(oh, axis=1, keepdims=True)
    base[...] = new_base
    cnt_ref[...] = new_base
    zi = jnp.zeros((4, tm), jnp.int32)
    idx_ref[...] = jnp.concatenate([e1, e2, r1.astype(jnp.int32), r2.astype(jnp.int32), zi], axis=0)
    zf = jnp.zeros((6, tm), F32)
    wc_ref[...] = jnp.transpose(jnp.concatenate([w1, w2, zf], axis=0))


MOE_ROWS = 256
ROW_GROUP = 8


def _slots_kernel(idx_ref, ps_ref, o_ref):
    idx = idx_ref[...]
    tm = idx.shape[1]
    rowx = lax.broadcasted_iota(jnp.int32, (N_EXPERTS, tm), 0)
    ps = ps_ref[:, 0:1]
    out = []
    for k in range(2):
        start = jnp.sum(jnp.where(rowx == idx[k:k + 1], ps, 0), axis=0, keepdims=True)
        out.append(start + idx[2 + k:3 + k])
    o_ref[...] = jnp.concatenate(out + [jnp.zeros((6, tm), jnp.int32)], axis=0)


def _slots(idx, pstart, tm=2048):
    n = idx.shape[1]
    tm = min(tm, n)
    ps = jnp.broadcast_to(pstart[:, None], (N_EXPERTS, LANES))
    return pl.pallas_call(
        _slots_kernel,
        grid=(n // tm,),
        in_specs=[pl.BlockSpec((8, tm), lambda i: (0, i)),
                  pl.BlockSpec((N_EXPERTS, LANES), lambda i: (0, 0))],
        out_specs=pl.BlockSpec((8, tm), lambda i: (0, i)),
        out_shape=jax.ShapeDtypeStruct((8, n), jnp.int32),
        compiler_params=_cparams(("parallel",)),
        name="moe_slots",
    )(idx, ps)


def _dispatch_kernel(pstart_ref, pend_ref, s1_ref, s2_ref, h_ref, xs_out, zbuf, sem, zsem):
    tm = s1_ref.shape[0]
    slot_refs = (s1_ref, s2_ref)

    @pl.when(pl.program_id(0) == 0)
    def _():
        zbuf[...] = jnp.zeros(zbuf.shape, zbuf.dtype)

        def zero_copy(e):
            first = pl.multiple_of(pend_ref[e] - MOE_ROWS, MOE_ROWS)
            return pltpu.make_async_copy(zbuf, xs_out.at[pl.ds(first, MOE_ROWS)], zsem)

        def zstart(e, carry):
            @pl.when(pend_ref[e] > pstart_ref[e])
            def _():
                zero_copy(e).start()
            return carry

        def zwait(e, carry):
            @pl.when(pend_ref[e] > pstart_ref[e])
            def _():
                zero_copy(e).wait()
            return carry

        lax.fori_loop(0, N_EXPERTS, zstart, 0)
        lax.fori_loop(0, N_EXPERTS, zwait, 0)

        def tail_copy(b):
            return pltpu.make_async_copy(
                zbuf, xs_out.at[pl.ds(pl.multiple_of(b * MOE_ROWS, MOE_ROWS), MOE_ROWS)], zsem)

        def tstart(b, carry):
            tail_copy(b).start()
            return carry

        def twait(b, carry):
            tail_copy(b).wait()
            return carry

        first_free = pend_ref[N_EXPERTS - 1] // MOE_ROWS
        n_blocks = xs_out.shape[0] // MOE_ROWS
        lax.fori_loop(first_free, n_blocks, tstart, 0)
        lax.fori_loop(first_free, n_blocks, twait, 0)

    def row_copy(g, u, k):
        slot = slot_refs[k][g * ROW_GROUP + u]
        return pltpu.make_async_copy(h_ref.at[g, pl.ds(u, 1)], xs_out.at[pl.ds(slot, 1)], sem)

    def start(g, carry):
        for u in range(ROW_GROUP):
            row_copy(g, u, 0).start()
            row_copy(g, u, 1).start(priority=1)
        return carry

    def wait(g, carry):
        for u in range(ROW_GROUP):
            row_copy(g, u, 0).wait()
            row_copy(g, u, 1).wait()
        return carry

    lax.fori_loop(0, tm // ROW_GROUP, start, 0)
    lax.fori_loop(0, tm // ROW_GROUP, wait, 0)


def _dispatch(pstart, pend, slots, h2, n_rows, tm=512):
    n, d = h2.shape
    return pl.pallas_call(
        _dispatch_kernel,
        grid_spec=pltpu.PrefetchScalarGridSpec(
            num_scalar_prefetch=2,
            grid=(n // tm,),
            in_specs=[pl.BlockSpec((tm,), lambda i, ps, pe: (i,), memory_space=pltpu.SMEM),
                      pl.BlockSpec((tm,), lambda i, ps, pe: (i,), memory_space=pltpu.SMEM),
                      pl.BlockSpec((tm // ROW_GROUP, ROW_GROUP, d), lambda i, ps, pe: (i, 0, 0))],
            out_specs=pl.BlockSpec(memory_space=pl.ANY),
            scratch_shapes=[pltpu.VMEM((MOE_ROWS, d), h2.dtype), pltpu.SemaphoreType.DMA(()),
                            pltpu.SemaphoreType.DMA(())]),
        out_shape=jax.ShapeDtypeStruct((n_rows, d), h2.dtype),
        compiler_params=_cparams(("arbitrary",)),
        name="moe_dispatch",
    )(pstart, pend, slots[0], slots[1], h2.reshape(n // ROW_GROUP, ROW_GROUP, d))


def _expert_kernel(be_ref, nu_ref, eslot_ref, enext_ref, x_ref, wg_hbm, wu_hbm, wd_hbm, o_ref,
                   wg_f, wu_f, wd_f, wg_b, wu_b, wd_b, sem):
    i = pl.program_id(0)
    used = i < nu_ref[0]
    e = be_ref[i]
    new_expert = (i == 0) | (e != be_ref[jnp.maximum(i - 1, 0)])

    def fetch(expert, slot):
        return [pltpu.make_async_copy(src.at[expert], dst.at[slot], sem.at[slot, k])
                for k, (src, dst) in enumerate(((wg_hbm, wg_f), (wu_hbm, wu_f), (wd_hbm, wd_f)))]

    @pl.when(i == 0)
    def _():
        for c in fetch(e, eslot_ref[e]):
            c.start()

    @pl.when(used & new_expert)
    def _():
        slot = eslot_ref[e]
        for c in fetch(e, slot):
            c.wait()
        nxt = enext_ref[e]

        @pl.when(nxt >= 0)
        def _():
            for c in fetch(nxt, 1 - slot):
                c.start()

        wg_b[...] = wg_f[slot].astype(BF16)
        wu_b[...] = wu_f[slot].astype(BF16)
        wd_b[...] = wd_f[slot].astype(BF16)

    @pl.when(used)
    def _():
        x_lo, x_hi = (t.astype(BF16) for t in _unpack_bf16_pairs(x_ref[...]))
        half = x_lo.shape[1]

        def proj(w_b):
            return (jnp.dot(x_lo, w_b[0:half, :], preferred_element_type=F32)
                    + jnp.dot(x_hi, w_b[half:, :], preferred_element_type=F32))

        a = proj(wg_b)
        u = proj(wu_b)
        hh = (a * _sigmoid(a)) * u
        y = jnp.dot(hh.astype(BF16), wd_b[...], preferred_element_type=F32)
        o_ref[...] = _pack_bf16_pairs(y)

    @pl.when(i >= nu_ref[0])
    def _():
        o_ref[...] = jnp.zeros(o_ref.shape, o_ref.dtype)


def _experts(block_e, n_used, eslot, enext, xs, wg, wu, wd):
    n_rows, dh = xs.shape
    _, d, de = wg.shape
    return pl.pallas_call(
        _expert_kernel,
        grid_spec=pltpu.PrefetchScalarGridSpec(
            num_scalar_prefetch=4,
            grid=(n_rows // MOE_ROWS,),
            in_specs=[pl.BlockSpec((MOE_ROWS, dh), lambda i, be, nu, es, en: (jnp.minimum(i, nu[0] - 1), 0)),
                      pl.BlockSpec(memory_space=pl.ANY),
                      pl.BlockSpec(memory_space=pl.ANY),
                      pl.BlockSpec(memory_space=pl.ANY)],
            out_specs=pl.BlockSpec((MOE_ROWS, dh), lambda i, be, nu, es, en: (i, 0)),
            scratch_shapes=[pltpu.VMEM((2, d, de), F32), pltpu.VMEM((2, d, de), F32),
                            pltpu.VMEM((2, de, d), F32),
                            pltpu.VMEM((d, de), BF16), pltpu.VMEM((d, de), BF16),
                            pltpu.VMEM((de, d), BF16), pltpu.SemaphoreType.DMA((2, 3))]),
        out_shape=jax.ShapeDtypeStruct((n_rows, dh), jnp.uint32),
        compiler_params=_cparams(("arbitrary",)),
        name="moe_experts",
    )(block_e, n_used, eslot, enext, xs, wg, wu, wd)


def _combine_kernel(s1_ref, s2_ref, ys_hbm, x_ref, w_ref, lnf_ref, o_ref, y1, y2, sem, *,
                    final_norm):
    tm = s1_ref.shape[0]
    slot_refs = (s1_ref, s2_ref)

    def row_copy(g, u, k):
        buf = y1 if k == 0 else y2
        slot = slot_refs[k][g * ROW_GROUP + u]
        return pltpu.make_async_copy(ys_hbm.at[pl.ds(slot, 1)], buf.at[g, pl.ds(u, 1)], sem)

    def start(g, carry):
        for u in range(ROW_GROUP):
            row_copy(g, u, 0).start()
            row_copy(g, u, 1).start(priority=1)
        return carry

    def wait(g, carry):
        for u in range(ROW_GROUP):
            row_copy(g, u, 0).wait()
            row_copy(g, u, 1).wait()
        return carry

    lax.fori_loop(0, tm // ROW_GROUP, start, 0)
    lax.fori_loop(0, tm // ROW_GROUP, wait, 0)
    w = w_ref[...]
    dh = y1.shape[2]
    a_lo, a_hi = _unpack_bf16_pairs(y1[...].reshape(tm, dh))
    b_lo, b_hi = _unpack_bf16_pairs(y2[...].reshape(tm, dh))
    moe = jnp.concatenate([w[:, 0:1] * a_lo + w[:, 1:2] * b_lo,
                           w[:, 0:1] * a_hi + w[:, 1:2] * b_hi], axis=1)
    y = x_ref[...] + moe
    o_ref[...] = _rmsnorm_rows(y, lnf_ref[...]) if final_norm else y


def _combine(slots, ys, x1, wc, lnf, final_norm, tm=512):
    n, d = x1.shape
    return pl.pallas_call(
        functools.partial(_combine_kernel, final_norm=final_norm),
        grid=(n // tm,),
        in_specs=[pl.BlockSpec((tm,), lambda i: (i,), memory_space=pltpu.SMEM),
                  pl.BlockSpec((tm,), lambda i: (i,), memory_space=pltpu.SMEM),
                  pl.BlockSpec(memory_space=pl.ANY),
                  pl.BlockSpec((tm, d), lambda i: (i, 0)),
                  pl.BlockSpec((tm, 8), lambda i: (i, 0)),
                  pl.BlockSpec((1, d), lambda i: (0, 0))],
        out_specs=pl.BlockSpec((tm, d), lambda i: (i, 0)),
        scratch_shapes=[pltpu.VMEM((tm // ROW_GROUP, ROW_GROUP, d // 2), jnp.uint32),
                        pltpu.VMEM((tm // ROW_GROUP, ROW_GROUP, d // 2), jnp.uint32),
                        pltpu.SemaphoreType.DMA(())],
        out_shape=jax.ShapeDtypeStruct((n, d), F32),
        compiler_params=_cparams(("arbitrary",)),
        name="moe_combine",
    )(slots[0], slots[1], ys, x1, wc, lnf)


def _moe(h2, idx, wc, counts, wg, wu, wd, x1, lnf, final_norm):
    n = h2.shape[0]
    n_rows = 2 * n + N_EXPERTS * MOE_ROWS
    cnt = counts[:, 0].astype(jnp.int32)
    padded = ((cnt + MOE_ROWS - 1) // MOE_ROWS) * MOE_ROWS
    pend = jnp.cumsum(padded)
    pstart = pend - padded
    blk_first = jnp.arange(n_rows // MOE_ROWS, dtype=jnp.int32) * MOE_ROWS
    block_e = jnp.minimum(jnp.sum(pend[None, :] <= blk_first[:, None], axis=1), N_EXPERTS - 1)
    n_used = (pend[-1:] // MOE_ROWS).astype(jnp.int32)
    has = padded > 0
    eslot = ((jnp.cumsum(has) - 1) % 2).astype(jnp.int32)
    ee = jnp.arange(N_EXPERTS, dtype=jnp.int32)
    later = jnp.where((ee[None, :] > ee[:, None]) & has[None, :], ee[None, :], N_EXPERTS)
    enext = jnp.min(later, axis=1)
    enext = jnp.where(enext == N_EXPERTS, -1, enext).astype(jnp.int32)
    slots = _slots(idx, pstart)
    xs = _dispatch(pstart, pend, slots, h2, n_rows)
    ys = _experts(block_e.astype(jnp.int32), n_used, eslot, enext, xs, wg, wu, wd)
    return _combine(slots, ys, x1, wc, lnf, final_norm)


def _pad_rows(a, before, total):
    return jnp.pad(a, ((before, total - before - a.shape[0]), (0, 0)))


def _layer(xf, batch, seq_len, ln1_w, w_in, rw_mu, rw_w0_f, rw_w2_f, rw_w0_b, rw_w2_b, rw_a0_f,
           rw_a2_f, rw_a0_b, rw_a2_b, rw_g2, rw_k_k, rw_k_a, rw_r_k, rw_ln_w, rw_ln_b, gla_gw2_f,
           gla_gb_f, gla_gw2_b, gla_gb_b, gla_norm_w, w_out, ln2_w, moe_w_coarse, moe_b_coarse,
           moe_w_fine, moe_b_fine, moe_w_gate, moe_w_up, moe_w_down):
    d = xf.shape[1]
    rw_cols = 3 * RWKV_WIDTH + 4 * LORA + GATE_LORA
    w_rw, w_gla = w_in[:, :rw_cols], w_in[:, rw_cols:]
    o = 0
    w_gq = w_gla[:, o:o + GLA_K_WIDTH]; o += GLA_K_WIDTH
    w_gk = w_gla[:, o:o + GLA_K_WIDTH]; o += GLA_K_WIDTH
    w_gv = w_gla[:, o:o + GLA_V_WIDTH]; o += GLA_V_WIDTH
    w_gg = w_gla[:, o:o + 2 * GLA_GATE_LORA]; o += 2 * GLA_GATE_LORA
    w_og = w_gla[:, o:o + GLA_V_WIDTH]
    w_gg = jnp.pad(w_gg, ((0, 0), (0, LANES - 2 * GLA_GATE_LORA)))
    w_pad = jnp.concatenate([w_rw, w_gg, w_gq, w_gk, w_gv, w_og], axis=1).astype(BF16)
    mu_pad = jnp.pad(rw_mu, (0, NP - rw_cols))[None, :]

    p = _inproj(xf, ln1_w[None, :], w_pad, mu_pad, seq_len, rw_cols)

    head = lax.broadcasted_iota(jnp.int32, (RWKV_WIDTH, RWKV_WIDTH), 0) // RWKV_HEAD_DIM
    blk = (head == head.T).astype(BF16)
    w0 = jnp.stack([rw_w0_f, rw_w0_b])[:, None, :]
    a0 = jnp.stack([rw_a0_f, rw_a0_b])[:, None, :]
    w2 = jnp.stack([_pad_rows(rw_w2_f, 0, LANES), _pad_rows(rw_w2_b, LORA, LANES)]).astype(BF16)
    a2 = jnp.stack([_pad_rows(rw_a2_f, 0, LANES), _pad_rows(rw_a2_b, LORA, LANES)]).astype(BF16)
    yf, yb, bdf, bdb = _rwkv_scan(p, batch, seq_len, w0, w2, a0, a2, rw_k_k[None, :],
                                  rw_k_a[None, :], rw_r_k.reshape(1, -1), blk)

    gw = jnp.stack([_pad_rows(gla_gw2_f, 0, LANES),
                    _pad_rows(gla_gw2_b, GLA_GATE_LORA, LANES)]).astype(BF16)
    gbias = jnp.stack([gla_gb_f, gla_gb_b])[:, None, :]
    of, ob = _gla_scan(p, batch, seq_len, gw, gbias)

    wr = jnp.concatenate([moe_w_coarse.T, jnp.zeros((8 - N_GROUPS, d), F32), moe_w_fine.T], axis=0)
    br = jnp.concatenate([moe_b_coarse, jnp.zeros((8 - N_GROUPS,), F32), moe_b_fine])[:, None]
    x1, h2, idx, wc, counts = _mix_out(
        xf, p, yf, yb, bdf, bdb, of, ob, rw_g2.astype(BF16), rw_ln_w[None, :], rw_ln_b[None, :], blk,
        gla_norm_w[None, :], w_out.astype(BF16), ln2_w[None, :], wr, br)
    return h2, idx, wc, counts, x1


def kernel(x, ln1_w, w_in, rw_mu, rw_w0_f, rw_w2_f, rw_w0_b, rw_w2_b, rw_a0_f, rw_a2_f, rw_a0_b, rw_a2_b, rw_g2, rw_k_k, rw_k_a, rw_r_k, rw_ln_w, rw_ln_b, gla_gw2_f, gla_gb_f, gla_gw2_b, gla_gb_b, gla_norm_w, w_out, ln2_w, moe_w_coarse, moe_b_coarse, moe_w_fine, moe_b_fine, moe_w_gate, moe_w_up, moe_w_down, ln_f_w):
    batch, seq_len, d = x.shape
    xf = x.reshape(batch * seq_len, d)
    depth = w_in.shape[0]
    for l in range(depth):
        h2, idx, wc, counts, x1 = _layer(
            xf, batch, seq_len, ln1_w[l], w_in[l], rw_mu[l], rw_w0_f[l], rw_w2_f[l], rw_w0_b[l],
            rw_w2_b[l], rw_a0_f[l], rw_a2_f[l], rw_a0_b[l], rw_a2_b[l], rw_g2[l], rw_k_k[l],
            rw_k_a[l], rw_r_k[l], rw_ln_w[l], rw_ln_b[l], gla_gw2_f[l], gla_gb_f[l], gla_gw2_b[l],
            gla_gb_b[l], gla_norm_w[l], w_out[l], ln2_w[l], moe_w_coarse[l], moe_b_coarse[l],
            moe_w_fine[l], moe_b_fine[l], moe_w_gate[l], moe_w_up[l], moe_w_down[l])
        xf = _moe(h2, idx, wc, counts, moe_w_gate[l], moe_w_up[l], moe_w_down[l], x1,
                  ln_f_w[None, :], l == depth - 1)
    return xf.reshape(batch, seq_len, d)
```

```python
import functools

import numpy as np
import jax
import jax.numpy as jnp
from jax import lax
from jax.experimental import pallas as pl
from jax.experimental.pallas import tpu as pltpu

F32 = jnp.float32
BF16 = jnp.bfloat16

RMS_EPS = 1e-6
RWKV_GN_EPS = 64e-5
RWKV_WIDTH = 512
RWKV_HEAD_DIM = 64
LORA = 64
GATE_LORA = 128
GLA_HEADS = 4
GLA_DK = 64
GLA_DV = 128
GLA_K_WIDTH = GLA_HEADS * GLA_DK
GLA_V_WIDTH = GLA_HEADS * GLA_DV
GLA_GATE_LORA = 16
GLA_TAU = 16.0
N_GROUPS = 4
EXPERTS_PER_GROUP = 8
N_EXPERTS = N_GROUPS * EXPERTS_PER_GROUP

CHUNK = 64
LANES = 128

COL_R, COL_K, COL_V = 0, 512, 1024
COL_LORA = 1536
COL_GL = 1792
COL_GG = 1920
COL_GQ, COL_GK, COL_GV, COL_OG = 2048, 2304, 2560, 3072
NP = 3584

VMEM_LIMIT = 56 * 1024 * 1024


def _cparams(sem):
    return pltpu.CompilerParams(dimension_semantics=sem, vmem_limit_bytes=VMEM_LIMIT)


MM_DTYPE = BF16


def _mm(a, b, ca=1, cb=0):
    dt = MM_DTYPE
    return lax.dot_general(a.astype(dt), b.astype(dt), (((ca,), (cb,)), ((), ())),
                           preferred_element_type=F32)


def _split2(x):
    hi = x.astype(BF16)
    lo = (x - hi.astype(F32)).astype(BF16)
    return hi, lo


def _split3(x):
    hi = x.astype(BF16)
    r1 = x - hi.astype(F32)
    mid = r1.astype(BF16)
    lo = (r1 - mid.astype(F32)).astype(BF16)
    return hi, mid, lo


def _mm_ones(x, ones_bf16, left=False, terms=3):
    parts = _split3(x)[:terms]
    prods = [_mm(ones_bf16, t) if left else _mm(t, ones_bf16) for t in parts]
    out = prods[-1]
    for t in prods[-2::-1]:
        out = out + t
    return out


def _mm_x3(a, b, ca=1, cb=0):
    ah, al = _split2(a)
    bh, bl = _split2(b)
    return (_mm(al, bh, ca, cb) + _mm(ah, bl, ca, cb)) + _mm(ah, bh, ca, cb)


def _pack_bf16_pairs(x):
    half = x.shape[1] // 2
    bits = lax.bitcast_convert_type(x.astype(BF16).astype(F32), jnp.uint32)
    return (bits[:, :half] >> 16) | (bits[:, half:] & jnp.uint32(0xFFFF0000))


def _unpack_bf16_pairs(w):
    lo = lax.bitcast_convert_type(w << 16, F32)
    hi = lax.bitcast_convert_type(w & jnp.uint32(0xFFFF0000), F32)
    return lo, hi


def _rmsnorm_rows(x, w):
    ms = jnp.mean(x * x, axis=-1, keepdims=True)
    return x * lax.rsqrt(ms + RMS_EPS) * w


def _softplus(x):
    return jnp.maximum(x, 0.0) + jnp.log(1.0 + jnp.exp(-jnp.abs(x)))


def _sigmoid(x):
    return 1.0 / (1.0 + jnp.exp(-x))


def _seg_cumsum(x, seg, rev):
    n = x.shape[0]
    pos = lax.broadcasted_iota(jnp.int32, x.shape, 0) & (seg - 1)
    sh = 1
    while sh < seg:
        if not rev:
            x = x + jnp.where(pos >= sh, pltpu.roll(x, sh, 0), 0.0)
        else:
            x = x + jnp.where(pos < seg - sh, pltpu.roll(x, n - sh, 0), 0.0)
        sh *= 2
    return x


def _blockdiag2(y, m0):
    return jnp.concatenate([jnp.where(m0, y, 0.0), jnp.where(m0, 0.0, y)], axis=0)


def _inproj_kernel(x_ref, xp_ref, xn_ref, lnw_ref, w_ref, mu_ref, o_ref, *, tiles_per_seq,
                   shift_cols, tn):
    i = pl.program_id(0)
    lnw = lnw_ref[...]
    h = _rmsnorm_rows(x_ref[...], lnw).astype(BF16)
    t = i % tiles_per_seq
    hp = jnp.where(t == 0, 0.0, _rmsnorm_rows(xp_ref[...], lnw))
    hn = jnp.where(t == tiles_per_seq - 1, 0.0, _rmsnorm_rows(xn_ref[...], lnw))
    hh = jnp.concatenate([hp, hn], axis=0).astype(BF16)
    tm = h.shape[0]
    npad = w_ref.shape[1]
    for c0 in range(0, npad, tn):
        c1 = c0 + tn
        w = w_ref[:, c0:c1]
        p = jnp.dot(h, w, preferred_element_type=F32)
        if c0 < shift_cols:
            ph = jnp.dot(hh, w, preferred_element_type=F32)
            row = lax.broadcasted_iota(jnp.int32, p.shape, 0)
            prev = jnp.where(row == 0, ph[7:8, :], pltpu.roll(p, 1, 0))
            nxt = jnp.where(row == tm - 1, ph[8:9, :], pltpu.roll(p, tm - 1, 0))
            p = p + mu_ref[:, c0:c1] * (0.5 * (prev + nxt) - p)
        o_ref[:, c0:c1] = p


def _inproj(xf, lnw, w_pad, mu_pad, seq_len, shift_cols, tm=512, tn=512):
    n, d = xf.shape
    npad = w_pad.shape[1]
    assert seq_len % tm == 0 and npad % tn == 0
    tps = seq_len // tm
    nb8 = n // 8
    return pl.pallas_call(
        functools.partial(_inproj_kernel, tiles_per_seq=tps, shift_cols=shift_cols, tn=tn),
        grid=(n // tm,),
        in_specs=[
            pl.BlockSpec((tm, d), lambda i: (i, 0)),
            pl.BlockSpec((8, d), lambda i: (jnp.maximum(i * (tm // 8) - 1, 0), 0)),
            pl.BlockSpec((8, d), lambda i: (jnp.minimum((i + 1) * (tm // 8), nb8 - 1), 0)),
            pl.BlockSpec((1, d), lambda i: (0, 0)),
            pl.BlockSpec((d, npad), lambda i: (0, 0)),
            pl.BlockSpec((1, npad), lambda i: (0, 0)),
        ],
        out_specs=pl.BlockSpec((tm, npad), lambda i: (i, 0)),
        out_shape=jax.ShapeDtypeStruct((n, npad), F32),
        compiler_params=_cparams(("parallel",)),
        name="inproj_shift",
    )(xf, xf, xf, lnw, w_pad, mu_pad)


RWKV_CPS = 2


def _rwkv_prep(d, r_ref, k_ref, v_ref, l_ref, w0_ref, w2_ref, a0_ref, a2_ref,
               kk_ref, ka_ref, rk_ref, blk_ref, bd_ref):
    rev = d == 1
    c = CHUNK
    r = r_ref[...]
    k = k_ref[...]
    lo = l_ref[...]
    blk = blk_ref[...]

    zw = w0_ref[d] + _mm(jnp.tanh(lo[:, 0:LANES]), w2_ref[d])
    lw = -jnp.exp(-_softplus(-zw) - 0.5)
    a = _sigmoid(a0_ref[d] + _mm(lo[:, LANES:2 * LANES], a2_ref[d]))
    kkr = k * kk_ref[...]
    ss = _mm_ones(kkr * kkr, blk, terms=1)
    kk = kkr / jnp.maximum(jnp.sqrt(ss), 1e-12)
    kd = k * (1.0 + (a - 1.0) * ka_ref[...])
    b = kk * a
    bd_ref[...] = _mm_ones(r * kd * rk_ref[...], blk, terms=2)

    cum = _seg_cumsum(lw, c, rev)
    n_blk = r.shape[0] // c
    tot = [cum[j * c:j * c + 1, :] if rev else cum[j * c + c - 1:j * c + c, :] for j in range(n_blk)]
    ginv = jnp.exp(-cum)
    gend = jnp.concatenate([jnp.exp(tot[j] - cum[j * c:(j + 1) * c]) for j in range(n_blk)], axis=0)
    return dict(at=-kk * jnp.exp(cum - lw), rt=r * jnp.exp(cum), bt=b * ginv, kt=kd * ginv,
                bh=b * gend, kh=kd * gend, gc=[jnp.exp(t) for t in tot], v=v_ref[...])


def _rwkv_kernel(rf, kf, vf, lf, rb, kb, vb, lb, w0, w2, a0, a2, kk, ka, rk, blk,
                 yf, yb, bdf, bdb, m_scr):
    @pl.when(pl.program_id(1) == 0)
    def _():
        m_scr[...] = jnp.zeros(m_scr.shape, F32)

    c = CHUNK
    n_blk = rf.shape[0] // c
    n_pair = RWKV_WIDTH // LANES
    prep = [_rwkv_prep(0, rf, kf, vf, lf, w0, w2, a0, a2, kk, ka, rk, blk, bdf),
            _rwkv_prep(1, rb, kb, vb, lb, w0, w2, a0, a2, kk, ka, rk, blk, bdb)]
    y_refs = [yf, yb]

    row = lax.broadcasted_iota(jnp.int32, (c, LANES), 0)
    lane = lax.broadcasted_iota(jnp.int32, (c, LANES), 1)
    s_idx = lane & (RWKV_HEAD_DIM - 1)
    m0 = lane < RWKV_HEAD_DIM
    strict = [s_idx < row, s_idx > row]
    incl = [s_idx <= row, s_idx >= row]
    diag = s_idx == row

    def bd2(y):
        return _blockdiag2(y, m0)

    def fold(z):
        return jnp.where(m0, z[:c], 0.0) + jnp.where(m0, 0.0, z[c:])

    chains = [(d, p, j) for j in range(n_blk) for p in range(n_pair) for d in (0, 1)]
    ds = [d for d, _, _ in chains]

    def get(name):
        return [prep[d][name][j * c:(j + 1) * c, p * LANES:(p + 1) * LANES] for d, p, j in chains]

    at, rt, bt, kt, bh, kh, vp = (get(n) for n in ("at", "rt", "bt", "kt", "bh", "kh", "v"))
    gc = [prep[d]["gc"][j][:, p * LANES:(p + 1) * LANES] for d, p, j in chains]
    x2 = [jnp.concatenate([a_, r_], axis=0) for a_, r_ in zip(at, rt)]
    sbk = [_mm(x, jnp.concatenate([bd2(b_), bd2(k_)], axis=0), 1, 1) for x, b_, k_ in zip(x2, bt, kt)]
    sb = [z[:, :LANES] for z in sbk]
    sk = [z[:, LANES:] for z in sbk]
    lab = [jnp.where(strict[d], z[:c], 0.0) for d, z in zip(ds, sb)]
    prb = [jnp.where(incl[d], z[c:], 0.0) for d, z in zip(ds, sb)]
    lak = [jnp.where(strict[d], z[:c], 0.0) for d, z in zip(ds, sk)]
    prk = [jnp.where(incl[d], z[c:], 0.0) for d, z in zip(ds, sk)]
    bdv = [bd2(y) for y in vp]
    xa = at
    xvpv = [_mm(jnp.concatenate([x, z], axis=0), y) for x, z, y in zip(lak, prk, bdv)]
    xv = [t[:c] for t in xvpv]
    pv = [t[c:] for t in xvpv]
    lp = lab
    n_it = 6
    for it in range(n_it):
        upd = [_mm(l_, jnp.concatenate([bd2(a_), bd2(v_)], axis=1)) for l_, a_, v_ in zip(lp, xa, xv)]
        xa = [a_ + u[:, :LANES] for a_, u in zip(xa, upd)]
        xv = [v_ + u[:, LANES:] for v_, u in zip(xv, upd)]
        if it < n_it - 1:
            lp = [_mm(l_, bd2(l_)) for l_ in lp]
    px = [_mm(x, jnp.concatenate([bd2(a_), bd2(v_)], axis=1)) for x, a_, v_ in zip(prb, xa, xv)]
    zb = [_mm(x, jnp.concatenate([a_, v_], axis=1), 0, 0) for x, a_, v_ in zip(bh, xa, xv)]
    zk = [_mm(x, y, 0, 0) for x, y in zip(kh, vp)]
    qg, yl, hh = {}, {}, {}
    for i, key in enumerate(chains):
        qp = rt[i] + px[i][:, :LANES]
        g = jnp.where(diag, gc[i], 0.0) + fold(zb[i][:, :LANES])
        qg[key] = jnp.concatenate([qp, g], axis=0)
        yl[key] = px[i][:, LANES:] + pv[i]
        hh[key] = fold(zb[i][:, LANES:]) + fold(zk[i])
    state = {(d, p): m_scr[d, p] for p in range(n_pair) for d in (0, 1)}
    for step in range(n_blk):
        for p in range(n_pair):
            for d in (0, 1):
                j = n_blk - 1 - step if d == 1 else step
                yg = _mm(qg[(d, p, j)], bd2(state[(d, p)]))
                y_refs[d][j * c:(j + 1) * c, p * LANES:(p + 1) * LANES] = yg[:c] + yl[(d, p, j)]
                state[(d, p)] = yg[c:] + hh[(d, p, j)]
    for (d, p), m in state.items():
        m_scr[d, p] = m


def _rwkv_scan(p, batch, seq_len, w0, w2, a0, a2, kk, ka, rk, blk):
    n = p.shape[0]
    rows = RWKV_CPS * CHUNK
    assert seq_len % rows == 0
    ns = seq_len // rows
    w = RWKV_WIDTH

    def fwd(col, width):
        return pl.BlockSpec((rows, width), lambda b, c: (b * ns + c, col // width))

    def bwd(col, width):
        return pl.BlockSpec((rows, width), lambda b, c: (b * ns + ns - 1 - c, col // width))

    def full(a):
        return pl.BlockSpec(a.shape, lambda b, c: (0,) * a.ndim)

    lora_w = 2 * LANES
    in_specs = [fwd(COL_R, w), fwd(COL_K, w), fwd(COL_V, w), fwd(COL_LORA, lora_w),
                bwd(COL_R, w), bwd(COL_K, w), bwd(COL_V, w), bwd(COL_LORA, lora_w),
                full(w0), full(w2), full(a0), full(a2), full(kk), full(ka), full(rk), full(blk)]
    ospec_f = pl.BlockSpec((rows, w), lambda b, c: (b * ns + c, 0))
    ospec_b = pl.BlockSpec((rows, w), lambda b, c: (b * ns + ns - 1 - c, 0))
    osh = jax.ShapeDtypeStruct((n, w), F32)
    return pl.pallas_call(
        _rwkv_kernel,
        grid=(batch, ns),
        in_specs=in_specs,
        out_specs=[ospec_f, ospec_b, ospec_f, ospec_b],
        out_shape=[osh, osh, osh, osh],
        scratch_shapes=[pltpu.VMEM((2, w // LANES, CHUNK, LANES), F32)],
        compiler_params=_cparams(("parallel", "arbitrary")),
        name="rwkv7_scan",
    )(p, p, p, p, p, p, p, p, w0, w2, a0, a2, kk, ka, rk, blk)


GLA_CPS = 8


def _gla_gates(d, g_ref, gw_ref, gb_ref, cum_ref):
    c = CHUNK
    xg = _mm(g_ref[...], gw_ref[d]) + gb_ref[d]
    g = -_softplus(-xg) / GLA_TAU
    out = []
    for j in range(g.shape[0] // c):
        hi, mid, _ = _split3(g[j * c:(j + 1) * c])
        out.append(_mm(cum_ref[d], jnp.concatenate([hi, mid], axis=0)))
    return out


def _gla_kernel(gf, qf, kf, vf, gb_, qb, kb, vb, gw, gbias, cum, of, ob, st_scr):
    @pl.when(pl.program_id(1) == 0)
    def _():
        st_scr[...] = jnp.zeros(st_scr.shape, F32)

    c = CHUNK
    n_blk = qf.shape[0] // c
    n_pair = GLA_K_WIDTH // LANES
    o_refs = [of, ob]
    cds = [_gla_gates(0, gf, gw, gbias, cum), _gla_gates(1, gb_, gw, gbias, cum)]
    qs = [qf[...] * (GLA_DK ** -0.5), qb[...] * (GLA_DK ** -0.5)]
    ks = [kf[...], kb[...]]
    vs = [vf[...], vb[...]]

    row = lax.broadcasted_iota(jnp.int32, (c, LANES), 0)
    lane = lax.broadcasted_iota(jnp.int32, (c, LANES), 1)
    s_idx = lane & (GLA_DK - 1)
    m0 = lane < GLA_DK
    diag = row == s_idx
    wide_row = lax.broadcasted_iota(jnp.int32, (c, GLA_K_WIDTH), 0)
    wide_prog = [wide_row, c - 1 - wide_row]

    chains = [(d, p, j) for j in range(n_blk) for p in range(n_pair) for d in (0, 1)]

    def blk_rows(x, j):
        return x[j * c:(j + 1) * c]

    def pair(x, p):
        return x[:, p * LANES:(p + 1) * LANES]

    def nt(x, y):
        return _mm(x, _blockdiag2(y, m0), 1, 1)

    qb_ = {(d, j): blk_rows(qs[d], j) for d in (0, 1) for j in range(n_blk)}
    kb_ = {(d, j): blk_rows(ks[d], j) for d in (0, 1) for j in range(n_blk)}
    a = [jnp.where(diag, nt(pair(qb_[(d, j)], p), pair(kb_[(d, j)], p)), 0.0) for d, p, j in chains]
    s = 1
    lvl = 1
    while s < c:
        same = (row // (2 * s)) == (s_idx // (2 * s))
        ql, kl = {}, {}
        for d in (0, 1):
            upper = (wide_prog[d] & (2 * s - 1)) >= s
            for j in range(n_blk):
                e = jnp.exp(-jnp.abs(cds[d][j][lvl * c:(lvl + 1) * c]))
                ql[(d, j)] = qb_[(d, j)] * jnp.where(upper, e, 0.0)
                kl[(d, j)] = kb_[(d, j)] * jnp.where(upper, 0.0, e)
        upd = [jnp.where(same, nt(pair(ql[(d, j)], p), pair(kl[(d, j)], p)), 0.0)
               for d, p, j in chains]
        a = [x + y for x, y in zip(a, upd)]
        s *= 2
        lvl += 1

    qt, kh, gc = {}, {}, {}
    for d in (0, 1):
        for j in range(n_blk):
            bc = cds[d][j][0:c]
            tot = bc[0:1, :] if d == 1 else bc[c - 1:c, :]
            qt[(d, j)] = qb_[(d, j)] * jnp.exp(bc)
            kh[(d, j)] = kb_[(d, j)] * jnp.exp(tot - bc)
            gc[(d, j)] = jnp.exp(tot)

    def vhead(d, j, h):
        return vs[d][j * c:(j + 1) * c, h * GLA_DV:(h + 1) * GLA_DV]

    oa = {key: _mm(_blockdiag2(a[i], m0),
                   jnp.concatenate([vhead(key[0], key[2], 2 * key[1]),
                                    vhead(key[0], key[2], 2 * key[1] + 1)], axis=0))
          for i, key in enumerate(chains)}
    z = {(d, p, j): _mm(jnp.concatenate([vhead(d, j, 2 * p), vhead(d, j, 2 * p + 1)], axis=1),
                        pair(kh[(d, j)], p), 0, 0)
         for d, p, j in chains}
    mv = lax.broadcasted_iota(jnp.int32, (GLA_DV, LANES), 1) < GLA_DK
    state = {(d, p): st_scr[d, p] for p in range(n_pair) for d in (0, 1)}
    for step in range(n_blk):
        for p in range(n_pair):
            for d in (0, 1):
                j = n_blk - 1 - step if d == 1 else step
                st = state[(d, p)]
                o = _mm(_blockdiag2(pair(qt[(d, j)], p), m0), st, 1, 1) + oa[(d, p, j)]
                o_refs[d][j * c:(j + 1) * c, (2 * p) * GLA_DV:(2 * p + 1) * GLA_DV] = o[:c]
                o_refs[d][j * c:(j + 1) * c, (2 * p + 1) * GLA_DV:(2 * p + 2) * GLA_DV] = o[c:]
                zz = z[(d, p, j)]
                state[(d, p)] = (st * pair(gc[(d, j)], p) + jnp.where(mv, zz[:GLA_DV], 0.0)
                                 + jnp.where(mv, 0.0, zz[GLA_DV:]))
    for (d, p), st in state.items():
        st_scr[d, p] = st


def _gla_cumsum_matrices():
    c = CHUNK
    i = np.arange(c)
    out = []
    for rev in (False, True):
        tri = (i[None, :] >= i[:, None]) if rev else (i[None, :] <= i[:, None])
        tri = tri.astype(np.float32)
        mats = [tri]
        s = 1
        while s < c:
            ref = (i // (2 * s)) * (2 * s) + (s if rev else s - 1)
            mats.append(tri - tri[ref])
            s *= 2
        out.append(np.concatenate(mats, axis=0))
    mats = np.stack(out)
    return jnp.asarray(np.concatenate([mats, mats], axis=2), dtype=BF16)


def _gla_scan(p, batch, seq_len, gw, gbias):
    n = p.shape[0]
    rows = GLA_CPS * CHUNK
    assert seq_len % rows == 0
    nc = seq_len // rows

    def fwd(col, width):
        return pl.BlockSpec((rows, width), lambda b, c: (b * nc + c, col // width))

    def bwd(col, width):
        return pl.BlockSpec((rows, width), lambda b, c: (b * nc + nc - 1 - c, col // width))

    def full(a):
        return pl.BlockSpec(a.shape, lambda b, c: (0,) * a.ndim)

    kw, vw = GLA_K_WIDTH, GLA_V_WIDTH
    cum = _gla_cumsum_matrices()
    in_specs = [fwd(COL_GG, LANES), fwd(COL_GQ, kw), fwd(COL_GK, kw), fwd(COL_GV, vw),
                bwd(COL_GG, LANES), bwd(COL_GQ, kw), bwd(COL_GK, kw), bwd(COL_GV, vw),
                full(gw), full(gbias), full(cum)]
    ospec_f = pl.BlockSpec((rows, vw), lambda b, c: (b * nc + c, 0))
    ospec_b = pl.BlockSpec((rows, vw), lambda b, c: (b * nc + nc - 1 - c, 0))
    osh = jax.ShapeDtypeStruct((n, vw), F32)
    return pl.pallas_call(
        _gla_kernel,
        grid=(batch, nc),
        in_specs=in_specs,
        out_specs=[ospec_f, ospec_b],
        out_shape=[osh, osh],
        scratch_shapes=[pltpu.VMEM((2, kw // LANES, GLA_DV, LANES), F32)],
        compiler_params=_cparams(("parallel", "arbitrary")),
        name="gla_scan",
    )(p, p, p, p, p, p, p, p, gw, gbias, cum)


MIX_ROWS = 256


def _mix_out_kernel(x_ref, yf, yb, bdf, bdb, v_ref, gl_ref, of, ob, og_ref,
                    g2_ref, lnw_ref, lnb_ref, blk_ref, nw_ref, wo_ref, ln2_ref, wr_ref, br_ref,
                    o_ref, h_ref, idx_ref, wc_ref, cnt_ref, base):
    blk = blk_ref[...]
    inv = 1.0 / RWKV_HEAD_DIM
    tm = x_ref.shape[0]
    rows = [slice(r0, r0 + MIX_ROWS) for r0 in range(0, tm, MIX_ROWS)]
    y = [yf[r] + yb[r] for r in rows]
    mean = [_mm_ones(t, blk, terms=2) * inv for t in y]
    yc = [a - b for a, b in zip(y, mean)]
    var = [_mm_ones(t * t, blk, terms=1) * inv for t in yc]
    gate = [_mm(_sigmoid(gl_ref[r]), g2_ref[...]) for r in rows]
    y_rw = [(c * lax.rsqrt(s + RWKV_GN_EPS) * lnw_ref[...] + lnb_ref[...]
             + (bdf[r] + bdb[r]) * v_ref[r]) * g
            for c, s, g, r in zip(yc, var, gate, rows)]
    acc = [x_ref[r] + _mm(t, wo_ref[0:RWKV_WIDTH, :]) for t, r in zip(y_rw, rows)]
    nw = nw_ref[...]
    for h in range(GLA_HEADS):
        sl = slice(h * GLA_DV, (h + 1) * GLA_DV)
        w_h = wo_ref[RWKV_WIDTH + h * GLA_DV:RWKV_WIDTH + (h + 1) * GLA_DV, :]
        for i, r in enumerate(rows):
            oh = of[r, sl] + ob[r, sl]
            og = og_ref[r, sl]
            ms = jnp.mean(oh * oh, axis=-1, keepdims=True)
            yg = oh * lax.rsqrt(ms + RMS_EPS) * nw[:, sl] * (og * _sigmoid(og))
            acc[i] = acc[i] + _mm(yg, w_h)
    x1 = jnp.concatenate(acc, axis=0)
    o_ref[...] = x1
    _route(x1, ln2_ref, wr_ref, br_ref, h_ref, idx_ref, wc_ref, cnt_ref, base)


def _mix_out(xf, p, yf, yb, bdf, bdb, of, ob, g2, lnw, lnb, blk, nw, wo, ln2, wr, br, tm=512):
    n, d = xf.shape
    w = RWKV_WIDTH

    def rows(width, col=0):
        return pl.BlockSpec((tm, width), lambda i: (i, col // width))

    def full(a):
        return pl.BlockSpec(a.shape, lambda i: (0,) * a.ndim)

    in_specs = [rows(d), rows(w), rows(w), rows(w), rows(w), rows(w, COL_V), rows(LANES, COL_GL),
                rows(w), rows(w), rows(w, COL_OG),
                full(g2), full(lnw), full(lnb), full(blk), full(nw), full(wo),
                full(ln2), full(wr), full(br)]
    return pl.pallas_call(
        _mix_out_kernel,
        grid=(n // tm,),
        in_specs=in_specs,
        out_specs=[rows(d), rows(d // 2),
                   pl.BlockSpec((8, tm), lambda i: (0, i)),
                   pl.BlockSpec((tm, 8), lambda i: (i, 0)),
                   pl.BlockSpec((N_EXPERTS, LANES), lambda i: (0, 0))],
        out_shape=[jax.ShapeDtypeStruct((n, d), F32),
                   jax.ShapeDtypeStruct((n, d // 2), jnp.uint32),
                   jax.ShapeDtypeStruct((8, n), jnp.int32),
                   jax.ShapeDtypeStruct((n, 8), F32),
                   jax.ShapeDtypeStruct((N_EXPERTS, LANES), F32)],
        scratch_shapes=[pltpu.VMEM((N_EXPERTS, LANES), F32)],
        compiler_params=_cparams(("arbitrary",)),
        name="mix_outproj_router",
    )(xf, yf, yb, bdf, bdb, p, p, of, ob, p, g2, lnw, lnb, blk, nw, wo, ln2, wr, br)


def _route(x1, lnw_ref, wr_ref, br_ref, h_ref, idx_ref, wc_ref, cnt_ref, base):
    @pl.when(pl.program_id(0) == 0)
    def _():
        base[...] = jnp.zeros(base.shape, F32)

    h = _rmsnorm_rows(x1, lnw_ref[...])
    h_ref[...] = _pack_bf16_pairs(h)
    logits = _mm_x3(wr_ref[...], h, 1, 1) + br_ref[...]
    tm = logits.shape[1]
    coarse = logits[0:N_GROUPS, :]
    fine = logits[8:8 + N_EXPERTS, :]
    rowg = lax.broadcasted_iota(jnp.int32, (N_GROUPS, tm), 0)
    cmax = jnp.max(coarse, axis=0, keepdims=True)
    gsel = jnp.min(jnp.where(coarse == cmax, rowg, N_GROUPS), axis=0, keepdims=True)
    pg = 1.0 / jnp.sum(jnp.exp(coarse - cmax), axis=0, keepdims=True)
    sel = jnp.zeros((EXPERTS_PER_GROUP, tm), F32)
    for g in range(N_GROUPS):
        sel = sel + jnp.where(gsel == g, fine[g * EXPERTS_PER_GROUP:(g + 1) * EXPERTS_PER_GROUP, :], 0.0)
    rowe = lax.broadcasted_iota(jnp.int32, (EXPERTS_PER_GROUP, tm), 0)
    l1 = jnp.max(sel, axis=0, keepdims=True)
    i1 = jnp.min(jnp.where(sel == l1, rowe, EXPERTS_PER_GROUP), axis=0, keepdims=True)
    sel2 = jnp.where(rowe == i1, -jnp.inf, sel)
    l2 = jnp.max(sel2, axis=0, keepdims=True)
    i2 = jnp.min(jnp.where(sel2 == l2, rowe, EXPERTS_PER_GROUP), axis=0, keepdims=True)
    t = jnp.exp(l2 - l1)
    w1 = pg / (1.0 + t)
    w2 = pg * t / (1.0 + t)
    e1 = gsel * EXPERTS_PER_GROUP + i1
    e2 = gsel * EXPERTS_PER_GROUP + i2
    rowx = lax.broadcasted_iota(jnp.int32, (N_EXPERTS, tm), 0)
    hit1 = rowx == e1
    hit2 = rowx == e2
    oh = jnp.where(hit1 | hit2, 1.0, 0.0)
    tr = lax.broadcasted_iota(jnp.int32, (tm, tm), 0)
    tc = lax.broadcasted_iota(jnp.int32, (tm, tm), 1)
    before = jnp.where(tr < tc, 1.0, 0.0).astype(BF16)
    pos = base[:, 0:1] + _mm(oh, before)
    r1 = jnp.sum(jnp.where(hit1, pos, 0.0), axis=0, keepdims=True)
    r2 = jnp.sum(jnp.where(hit2, pos, 0.0), axis=0, keepdims=True)
    new_base = base[...] + jnp.sum(oh, axis=1, keepdims=True)
    base[...] = new_base
    cnt_ref[...] = new_base
    zi = jnp.zeros((4, tm), jnp.int32)
    idx_ref[...] = jnp.concatenate([e1, e2, r1.astype(jnp.int32), r2.astype(jnp.int32), zi], axis=0)
    zf = jnp.zeros((6, tm), F32)
    wc_ref[...] = jnp.transpose(jnp.concatenate([w1, w2, zf], axis=0))


MOE_ROWS = 256
ROW_GROUP = 8


def _slots_kernel(idx_ref, ps_ref, o_ref):
    idx = idx_ref[...]
    tm = idx.shape[1]
    rowx = lax.broadcasted_iota(jnp.int32, (N_EXPERTS, tm), 0)
    ps = ps_ref[:, 0:1]
    out = []
    for k in range(2):
        start = jnp.sum(jnp.where(rowx == idx[k:k + 1], ps, 0), axis=0, keepdims=True)
        out.append(start + idx[2 + k:3 + k])
    o_ref[...] = jnp.concatenate(out + [jnp.zeros((6, tm), jnp.int32)], axis=0)


def _slots(idx, pstart, tm=2048):
    n = idx.shape[1]
    tm = min(tm, n)
    ps = jnp.broadcast_to(pstart[:, None], (N_EXPERTS, LANES))
    return pl.pallas_call(
        _slots_kernel,
        grid=(n // tm,),
        in_specs=[pl.BlockSpec((8, tm), lambda i: (0, i)),
                  pl.BlockSpec((N_EXPERTS, LANES), lambda i: (0, 0))],
        out_specs=pl.BlockSpec((8, tm), lambda i: (0, i)),
        out_shape=jax.ShapeDtypeStruct((8, n), jnp.int32),
        compiler_params=_cparams(("parallel",)),
        name="moe_slots",
    )(idx, ps)


def _dispatch_kernel(pstart_ref, pend_ref, s1_ref, s2_ref, h_ref, xs_out, zbuf, sem, zsem):
    tm = s1_ref.shape[0]
    slot_refs = (s1_ref, s2_ref)

    @pl.when(pl.program_id(0) == 0)
    def _():
        zbuf[...] = jnp.zeros(zbuf.shape, zbuf.dtype)

        def zero_copy(e):
            first = pl.multiple_of(pend_ref[e] - MOE_ROWS, MOE_ROWS)
            return pltpu.make_async_copy(zbuf, xs_out.at[pl.ds(first, MOE_ROWS)], zsem)

        def zstart(e, carry):
            @pl.when(pend_ref[e] > pstart_ref[e])
            def _():
                zero_copy(e).start()
            return carry

        def zwait(e, carry):
            @pl.when(pend_ref[e] > pstart_ref[e])
            def _():
                zero_copy(e).wait()
            return carry

        lax.fori_loop(0, N_EXPERTS, zstart, 0)
        lax.fori_loop(0, N_EXPERTS, zwait, 0)

        def tail_copy(b):
            return pltpu.make_async_copy(
                zbuf, xs_out.at[pl.ds(pl.multiple_of(b * MOE_ROWS, MOE_ROWS), MOE_ROWS)], zsem)

        def tstart(b, carry):
            tail_copy(b).start()
            return carry

        def twait(b, carry):
            tail_copy(b).wait()
            return carry

        first_free = pend_ref[N_EXPERTS - 1] // MOE_ROWS
        n_blocks = xs_out.shape[0] // MOE_ROWS
        lax.fori_loop(first_free, n_blocks, tstart, 0)
        lax.fori_loop(first_free, n_blocks, twait, 0)

    def row_copy(g, u, k):
        slot = slot_refs[k][g * ROW_GROUP + u]
        return pltpu.make_async_copy(h_ref.at[g, pl.ds(u, 1)], xs_out.at[pl.ds(slot, 1)], sem)

    def start(g, carry):
        for u in range(ROW_GROUP):
            row_copy(g, u, 0).start()
            row_copy(g, u, 1).start(priority=1)
        return carry

    def wait(g, carry):
        for u in range(ROW_GROUP):
            row_copy(g, u, 0).wait()
            row_copy(g, u, 1).wait()
        return carry

    lax.fori_loop(0, tm // ROW_GROUP, start, 0)
    lax.fori_loop(0, tm // ROW_GROUP, wait, 0)


def _dispatch(pstart, pend, slots, h2, n_rows, tm=1024):
    n, d = h2.shape
    return pl.pallas_call(
        _dispatch_kernel,
        grid_spec=pltpu.PrefetchScalarGridSpec(
            num_scalar_prefetch=2,
            grid=(n // tm,),
            in_specs=[pl.BlockSpec((tm,), lambda i, ps, pe: (i,), memory_space=pltpu.SMEM),
                      pl.BlockSpec((tm,), lambda i, ps, pe: (i,), memory_space=pltpu.SMEM),
                      pl.BlockSpec((tm // ROW_GROUP, ROW_GROUP, d), lambda i, ps, pe: (i, 0, 0))],
            out_specs=pl.BlockSpec(memory_space=pl.ANY),
            scratch_shapes=[pltpu.VMEM((MOE_ROWS, d), h2.dtype), pltpu.SemaphoreType.DMA(()),
                            pltpu.SemaphoreType.DMA(())]),
        out_shape=jax.ShapeDtypeStruct((n_rows, d), h2.dtype),
        compiler_params=_cparams(("arbitrary",)),
        name="moe_dispatch",
    )(pstart, pend, slots[0], slots[1], h2.reshape(n // ROW_GROUP, ROW_GROUP, d))


def _expert_kernel(be_ref, nu_ref, eslot_ref, enext_ref, x_ref, wg_hbm, wu_hbm, wd_hbm, o_ref,
                   wg_f, wu_f, wd_f, wg_b, wu_b, wd_b, sem):
    i = pl.program_id(0)
    used = i < nu_ref[0]
    e = be_ref[i]
    new_expert = (i == 0) | (e != be_ref[jnp.maximum(i - 1, 0)])

    def fetch(expert, slot):
        return [pltpu.make_async_copy(src.at[expert], dst.at[slot], sem.at[slot, k])
                for k, (src, dst) in enumerate(((wg_hbm, wg_f), (wu_hbm, wu_f), (wd_hbm, wd_f)))]

    @pl.when(i == 0)
    def _():
        for c in fetch(e, eslot_ref[e]):
            c.start()

    @pl.when(used & new_expert)
    def _():
        slot = eslot_ref[e]
        for c in fetch(e, slot):
            c.wait()
        nxt = enext_ref[e]

        @pl.when(nxt >= 0)
        def _():
            for c in fetch(nxt, 1 - slot):
                c.start()

        wg_b[...] = wg_f[slot].astype(BF16)
        wu_b[...] = wu_f[slot].astype(BF16)
        wd_b[...] = wd_f[slot].astype(BF16)

    @pl.when(used)
    def _():
        x_lo, x_hi = (t.astype(BF16) for t in _unpack_bf16_pairs(x_ref[...]))
        half = x_lo.shape[1]

        def proj(w_b):
            return (jnp.dot(x_lo, w_b[0:half, :], preferred_element_type=F32)
                    + jnp.dot(x_hi, w_b[half:, :], preferred_element_type=F32))

        a = proj(wg_b)
        u = proj(wu_b)
        hh = (a * _sigmoid(a)) * u
        y = jnp.dot(hh.astype(BF16), wd_b[...], preferred_element_type=F32)
        o_ref[...] = _pack_bf16_pairs(y)

    @pl.when(i >= nu_ref[0])
    def _():
        o_ref[...] = jnp.zeros(o_ref.shape, o_ref.dtype)


def _experts(block_e, n_used, eslot, enext, xs, wg, wu, wd):
    n_rows, dh = xs.shape
    _, d, de = wg.shape
    return pl.pallas_call(
        _expert_kernel,
        grid_spec=pltpu.PrefetchScalarGridSpec(
            num_scalar_prefetch=4,
            grid=(n_rows // MOE_ROWS,),
            in_specs=[pl.BlockSpec((MOE_ROWS, dh), lambda i, be, nu, es, en: (jnp.minimum(i, nu[0] - 1), 0)),
                      pl.BlockSpec(memory_space=pl.ANY),
                      pl.BlockSpec(memory_space=pl.ANY),
                      pl.BlockSpec(memory_space=pl.ANY)],
            out_specs=pl.BlockSpec((MOE_ROWS, dh), lambda i, be, nu, es, en: (i, 0)),
            scratch_shapes=[pltpu.VMEM((2, d, de), F32), pltpu.VMEM((2, d, de), F32),
                            pltpu.VMEM((2, de, d), F32),
                            pltpu.VMEM((d, de), BF16), pltpu.VMEM((d, de), BF16),
                            pltpu.VMEM((de, d), BF16), pltpu.SemaphoreType.DMA((2, 3))]),
        out_shape=jax.ShapeDtypeStruct((n_rows, dh), jnp.uint32),
        compiler_params=_cparams(("arbitrary",)),
        name="moe_experts",
    )(block_e, n_used, eslot, enext, xs, wg, wu, wd)


def _combine_kernel(s1_ref, s2_ref, ys_hbm, x_ref, w_ref, lnf_ref, o_ref, y1, y2, sem, *,
                    final_norm):
    tm = s1_ref.shape[0]
    slot_refs = (s1_ref, s2_ref)

    def row_copy(g, u, k):
        buf = y1 if k == 0 else y2
        slot = slot_refs[k][g * ROW_GROUP + u]
        return pltpu.make_async_copy(ys_hbm.at[pl.ds(slot, 1)], buf.at[g, pl.ds(u, 1)], sem)

    def start(g, carry):
        for u in range(ROW_GROUP):
            row_copy(g, u, 0).start()
            row_copy(g, u, 1).start(priority=1)
        return carry

    def wait(g, carry):
        for u in range(ROW_GROUP):
            row_copy(g, u, 0).wait()
            row_copy(g, u, 1).wait()
        return carry

    lax.fori_loop(0, tm // ROW_GROUP, start, 0)
    lax.fori_loop(0, tm // ROW_GROUP, wait, 0)
    w = w_ref[...]
    dh = y1.shape[2]
    a_lo, a_hi = _unpack_bf16_pairs(y1[...].reshape(tm, dh))
    b_lo, b_hi = _unpack_bf16_pairs(y2[...].reshape(tm, dh))
    moe = jnp.concatenate([w[:, 0:1] * a_lo + w[:, 1:2] * b_lo,
                           w[:, 0:1] * a_hi + w[:, 1:2] * b_hi], axis=1)
    y = x_ref[...] + moe
    o_ref[...] = _rmsnorm_rows(y, lnf_ref[...]) if final_norm else y


def _combine(slots, ys, x1, wc, lnf, final_norm, tm=1024):
    n, d = x1.shape
    return pl.pallas_call(
        functools.partial(_combine_kernel, final_norm=final_norm),
        grid=(n // tm,),
        in_specs=[pl.BlockSpec((tm,), lambda i: (i,), memory_space=pltpu.SMEM),
                  pl.BlockSpec((tm,), lambda i: (i,), memory_space=pltpu.SMEM),
                  pl.BlockSpec(memory_space=pl.ANY),
                  pl.BlockSpec((tm, d), lambda i: (i, 0)),
                  pl.BlockSpec((tm, 8), lambda i: (i, 0)),
                  pl.BlockSpec((1, d), lambda i: (0, 0))],
        out_specs=pl.BlockSpec((tm, d), lambda i: (i, 0)),
        scratch_shapes=[pltpu.VMEM((tm // ROW_GROUP, ROW_GROUP, d // 2), jnp.uint32),
                        pltpu.VMEM((tm // ROW_GROUP, ROW_GROUP, d // 2), jnp.uint32),
                        pltpu.SemaphoreType.DMA(())],
        out_shape=jax.ShapeDtypeStruct((n, d), F32),
        compiler_params=_cparams(("arbitrary",)),
        name="moe_combine",
    )(slots[0], slots[1], ys, x1, wc, lnf)


def _moe(h2, idx, wc, counts, wg, wu, wd, x1, lnf, final_norm):
    n = h2.shape[0]
    n_rows = 2 * n + N_EXPERTS * MOE_ROWS
    cnt = counts[:, 0].astype(jnp.int32)
    padded = ((cnt + MOE_ROWS - 1) // MOE_ROWS) * MOE_ROWS
    pend = jnp.cumsum(padded)
    pstart = pend - padded
    blk_first = jnp.arange(n_rows // MOE_ROWS, dtype=jnp.int32) * MOE_ROWS
    block_e = jnp.minimum(jnp.sum(pend[None, :] <= blk_first[:, None], axis=1), N_EXPERTS - 1)
    n_used = (pend[-1:] // MOE_ROWS).astype(jnp.int32)
    has = padded > 0
    eslot = ((jnp.cumsum(has) - 1) % 2).astype(jnp.int32)
    ee = jnp.arange(N_EXPERTS, dtype=jnp.int32)
    later = jnp.where((ee[None, :] > ee[:, None]) & has[None, :], ee[None, :], N_EXPERTS)
    enext = jnp.min(later, axis=1)
    enext = jnp.where(enext == N_EXPERTS, -1, enext).astype(jnp.int32)
    slots = _slots(idx, pstart)
    xs = _dispatch(pstart, pend, slots, h2, n_rows)
    ys = _experts(block_e.astype(jnp.int32), n_used, eslot, enext, xs, wg, wu, wd)
    return _combine(slots, ys, x1, wc, lnf, final_norm)


def _pad_rows(a, before, total):
    return jnp.pad(a, ((before, total - before - a.shape[0]), (0, 0)))


def _layer(xf, batch, seq_len, ln1_w, w_in, rw_mu, rw_w0_f, rw_w2_f, rw_w0_b, rw_w2_b, rw_a0_f,
           rw_a2_f, rw_a0_b, rw_a2_b, rw_g2, rw_k_k, rw_k_a, rw_r_k, rw_ln_w, rw_ln_b, gla_gw2_f,
           gla_gb_f, gla_gw2_b, gla_gb_b, gla_norm_w, w_out, ln2_w, moe_w_coarse, moe_b_coarse,
           moe_w_fine, moe_b_fine, moe_w_gate, moe_w_up, moe_w_down):
    d = xf.shape[1]
    rw_cols = 3 * RWKV_WIDTH + 4 * LORA + GATE_LORA
    w_rw, w_gla = w_in[:, :rw_cols], w_in[:, rw_cols:]
    o = 0
    w_gq = w_gla[:, o:o + GLA_K_WIDTH]; o += GLA_K_WIDTH
    w_gk = w_gla[:, o:o + GLA_K_WIDTH]; o += GLA_K_WIDTH
    w_gv = w_gla[:, o:o + GLA_V_WIDTH]; o += GLA_V_WIDTH
    w_gg = w_gla[:, o:o + 2 * GLA_GATE_LORA]; o += 2 * GLA_GATE_LORA
    w_og = w_gla[:, o:o + GLA_V_WIDTH]
    w_gg = jnp.pad(w_gg, ((0, 0), (0, LANES - 2 * GLA_GATE_LORA)))
    w_pad = jnp.concatenate([w_rw, w_gg, w_gq, w_gk, w_gv, w_og], axis=1).astype(BF16)
    mu_pad = jnp.pad(rw_mu, (0, NP - rw_cols))[None, :]

    p = _inproj(xf, ln1_w[None, :], w_pad, mu_pad, seq_len, rw_cols)

    head = lax.broadcasted_iota(jnp.int32, (RWKV_WIDTH, RWKV_WIDTH), 0) // RWKV_HEAD_DIM
    blk = (head == head.T).astype(BF16)
    w0 = jnp.stack([rw_w0_f, rw_w0_b])[:, None, :]
    a0 = jnp.stack([rw_a0_f, rw_a0_b])[:, None, :]
    w2 = jnp.stack([_pad_rows(rw_w2_f, 0, LANES), _pad_rows(rw_w2_b, LORA, LANES)]).astype(BF16)
    a2 = jnp.stack([_pad_rows(rw_a2_f, 0, LANES), _pad_rows(rw_a2_b, LORA, LANES)]).astype(BF16)
    yf, yb, bdf, bdb = _rwkv_scan(p, batch, seq_len, w0, w2, a0, a2, rw_k_k[None, :],
                                  rw_k_a[None, :], rw_r_k.reshape(1, -1), blk)

    gw = jnp.stack([_pad_rows(gla_gw2_f, 0, LANES),
                    _pad_rows(gla_gw2_b, GLA_GATE_LORA, LANES)]).astype(BF16)
    gbias = jnp.stack([gla_gb_f, gla_gb_b])[:, None, :]
    of, ob = _gla_scan(p, batch, seq_len, gw, gbias)

    wr = jnp.concatenate([moe_w_coarse.T, jnp.zeros((8 - N_GROUPS, d), F32), moe_w_fine.T], axis=0)
    br = jnp.concatenate([moe_b_coarse, jnp.zeros((8 - N_GROUPS,), F32), moe_b_fine])[:, None]
    x1, h2, idx, wc, counts = _mix_out(
        xf, p, yf, yb, bdf, bdb, of, ob, rw_g2.astype(BF16), rw_ln_w[None, :], rw_ln_b[None, :], blk,
        gla_norm_w[None, :], w_out.astype(BF16), ln2_w[None, :], wr, br)
    return h2, idx, wc, counts, x1


def kernel(x, ln1_w, w_in, rw_mu, rw_w0_f, rw_w2_f, rw_w0_b, rw_w2_b, rw_a0_f, rw_a2_f, rw_a0_b, rw_a2_b, rw_g2, rw_k_k, rw_k_a, rw_r_k, rw_ln_w, rw_ln_b, gla_gw2_f, gla_gb_f, gla_gw2_b, gla_gb_b, gla_norm_w, w_out, ln2_w, moe_w_coarse, moe_b_coarse, moe_w_fine, moe_b_fine, moe_w_gate, moe_w_up, moe_w_down, ln_f_w):
    batch, seq_len, d = x.shape
    xf = x.reshape(batch * seq_len, d)
    depth = w_in.shape[0]
    for l in range(depth):
        h2, idx, wc, counts, x1 = _layer(
            xf, batch, seq_len, ln1_w[l], w_in[l], rw_mu[l], rw_w0_f[l], rw_w2_f[l], rw_w0_b[l],
            rw_w2_b[l], rw_a0_f[l], rw_a2_f[l], rw_a0_b[l], rw_a2_b[l], rw_g2[l], rw_k_k[l],
            rw_k_a[l], rw_r_k[l], rw_ln_w[l], rw_ln_b[l], gla_gw2_f[l], gla_gb_f[l], gla_gw2_b[l],
            gla_gb_b[l], gla_norm_w[l], w_out[l], ln2_w[l], moe_w_coarse[l], moe_b_coarse[l],
            moe_w_fine[l], moe_b_fine[l], moe_w_gate[l], moe_w_up[l], moe_w_down[l])
        xf = _moe(h2, idx, wc, counts, moe_w_gate[l], moe_w_up[l], moe_w_down[l], x1,
                  ln_f_w[None, :], l == depth - 1)
    return xf.reshape(batch, seq_len, d)
```
